```python
import math
import jax, jax.numpy as jnp
from jax import lax
import numpy as np

D_MODEL = 1024
BATCH = 8
SEQ = 2048
DEPTH = 4

GRID_W = 64
CTX_LEN = 256

N_MIXERS = 3
MIXER_HYENA = 0
MIXER_MLA = 1
MIXER_GDN = 2
N_HYENA = (DEPTH + 2) // 3
N_MLA = (DEPTH + 1) // 3
N_GDN = DEPTH // 3

DEEPNORM_ALPHA = (2 * DEPTH) ** 0.25
DEEPNORM_BETA = (8 * DEPTH) ** -0.25
LN_EPS = 1e-5
RMS_EPS = 1e-6

D_FF = -(-8 * D_MODEL // (3 * 256)) * 256

HYENA_SHORT = 3
HYENA_EMB = 33
HYENA_FILTER_HIDDEN = 64
HYENA_DECAY_TARGET = 1e-2
HYENA_FAST_DECAY_PCT = 0.3
HYENA_SLOW_DECAY_PCT = 1.5

MLA_HEADS = 8
MLA_Q_LORA = 384
MLA_KV_LORA = 256
MLA_NOPE = 128
MLA_ROPE = 64
MLA_V = 128
MLA_SCALE = (MLA_NOPE + MLA_ROPE) ** -0.5
ROPE_THETA = 10000.0
Q_BLOCK = 128

GDN_HEADS = 8
GDN_DK = 128
GDN_DV = 128
GDN_SHORT = 3
GDN_CHUNK = 64

kernel_name = "hybrid_hyena_mla_gdn_flow_backbone"


def layer_norm(x, g, b):
    xf = x.astype(jnp.float32)
    mu = jnp.mean(xf, axis=-1, keepdims=True)
    var = jnp.mean(jnp.square(xf - mu), axis=-1, keepdims=True)
    return ((xf - mu) * lax.rsqrt(var + LN_EPS) * g + b).astype(x.dtype)


def rms_norm(x, g):
    xf = x.astype(jnp.float32)
    return (xf * lax.rsqrt(jnp.mean(jnp.square(xf), axis=-1, keepdims=True) + RMS_EPS) * g).astype(x.dtype)


def l2_normalize(x):
    xf = x.astype(jnp.float32)
    return (xf * lax.rsqrt(jnp.sum(xf * xf, axis=-1, keepdims=True) + RMS_EPS)).astype(x.dtype)


def centred_depthwise_conv(u, w):
    width, chans = w.shape
    return lax.conv_general_dilated(
        u, w[:, None, :].astype(u.dtype), window_strides=(1,),
        padding=[((width - 1) // 2, width // 2)],
        dimension_numbers=("NWC", "WIO", "NWC"), feature_group_count=chans)


def adaln(cvec, w, b):
    return jnp.split(jax.nn.silu(cvec) @ w + b, 6, axis=-1)


def swiglu(h, w_in, w_out):
    gate, up = jnp.split(h @ w_in, 2, axis=-1)
    return (jax.nn.silu(gate) * up) @ w_out


def mixer_residual_and_ffn(s, y_mix, m, g, b, w_in, w_out):
    s = layer_norm(DEEPNORM_ALPHA * s + m[2] * y_mix, g[0], b[0])
    f = swiglu(s * (1.0 + m[4]) + m[3], w_in, w_out)
    return layer_norm(DEEPNORM_ALPHA * s + m[5] * f, g[1], b[1])


def hyena_position_features(length):
    t01 = jnp.linspace(0.0, 1.0, length, dtype=jnp.float32)[:, None]
    bands = (HYENA_EMB - 1) // 2
    w = (2.0 * math.pi / length) * jnp.arange(length, dtype=jnp.float32)
    f = jnp.linspace(1e-4, bands - 1, bands, dtype=jnp.float32)
    ang = w[:, None] * f[None, :]
    return jnp.concatenate([t01, jnp.cos(ang), -jnp.sin(ang)], axis=-1)


def hyena_decay_window(length):
    max_decay = math.log(HYENA_DECAY_TARGET) / HYENA_FAST_DECAY_PCT
    min_decay = math.log(HYENA_DECAY_TARGET) / HYENA_SLOW_DECAY_PCT
    deltas = jnp.abs(jnp.linspace(min_decay, max_decay, D_MODEL, dtype=jnp.float32))
    t = jnp.linspace(0.0, 1.0, length, dtype=jnp.float32)
    return jnp.exp(-t[:, None] * deltas[None, :])


def hyena_filters(length, p):
    f32 = jnp.float32
    freq = p["freq"].astype(f32)
    z = hyena_position_features(length)
    z = jnp.sin(freq * (z @ p["fw1"].astype(f32) + p["fb1"].astype(f32)))
    z = jnp.sin(freq * (z @ p["fw2"].astype(f32) + p["fb2"].astype(f32)))
    z = jnp.sin(freq * (z @ p["fw3"].astype(f32) + p["fb3"].astype(f32)))
    k_fwd, k_bwd = jnp.split(z @ p["fw4"].astype(f32), 2, axis=-1)
    window = hyena_decay_window(length)
    return k_fwd * window, k_bwd * window


def two_sided_fft_conv(u, k_fwd, k_bwd):
    length = u.shape[1]
    pad = ((0, length), (0, 0))
    taps = jnp.pad(k_fwd, pad) + jnp.roll(jnp.pad(k_bwd, pad)[::-1], 1, axis=0)
    spec = jnp.fft.rfft(u, n=2 * length, axis=1) * jnp.fft.rfft(taps, axis=0)[None]
    return jnp.fft.irfft(spec, n=2 * length, axis=1)[:, :length]


def hyena_mixer(h, p):
    length = h.shape[1]
    z = centred_depthwise_conv(h @ p["w_in"] + p["b_in"], p["conv_w"]) + p["conv_b"]
    x0, x1, v = jnp.split(z, 3, axis=-1)
    k_fwd, k_bwd = hyena_filters(length, p)
    u = (x1 * v).astype(jnp.float32)
    y = two_sided_fft_conv(u, k_fwd, k_bwd) + u * p["skip"].astype(jnp.float32)
    return (x0 * y.astype(h.dtype)) @ p["w_out"] + p["b_out"]


def axial_rope(t):
    length = t.shape[1]
    rows = length // GRID_W
    row, col = jnp.meshgrid(jnp.arange(rows, dtype=jnp.float32), jnp.arange(GRID_W, dtype=jnp.float32), indexing="ij")
    half = MLA_ROPE // 2
    quarter = half // 2
    inv_freq = ROPE_THETA ** (-jnp.arange(quarter, dtype=jnp.float32) / quarter)

    def rotate(seg, pos):
        ang = (pos[:, None] * inv_freq[None, :])[:, None, :]
        cos, sin = jnp.cos(ang), jnp.sin(ang)
        a, b = seg[..., :quarter], seg[..., quarter:]
        return jnp.concatenate([a * cos - b * sin, b * cos + a * sin], axis=-1)

    tf = t.astype(jnp.float32)
    out = jnp.concatenate([rotate(tf[..., :half], row.reshape(-1)), rotate(tf[..., half:], col.reshape(-1))], axis=-1)
    return out.astype(t.dtype)


def mla_project(h, p, with_pos):
    bsz, length, _ = h.shape
    cq, ckv, k_rope = jnp.split(h @ p["w_in"], [MLA_Q_LORA, MLA_Q_LORA + MLA_KV_LORA], axis=-1)
    q = (rms_norm(cq, p["q_norm"]) @ p["w_uq"]).reshape(bsz, length, MLA_HEADS, MLA_NOPE + MLA_ROPE)
    kv = (rms_norm(ckv, p["kv_norm"]) @ p["w_ukv"]).reshape(bsz, length, MLA_HEADS, MLA_NOPE + MLA_V)
    q_nope, q_rope = jnp.split(q, [MLA_NOPE], axis=-1)
    k_nope, v = jnp.split(kv, [MLA_NOPE], axis=-1)
    k_rope = k_rope[:, :, None, :]
    if with_pos:
        q_rope = axial_rope(q_rope)
        k_rope = axial_rope(k_rope)
    q = jnp.concatenate([q_nope, q_rope], axis=-1)
    k = jnp.concatenate([k_nope, jnp.broadcast_to(k_rope, (bsz, length, MLA_HEADS, MLA_ROPE))], axis=-1)
    to_heads = lambda a: a.transpose(0, 2, 1, 3)
    return to_heads(q), to_heads(k), to_heads(v)


def blocked_softmax_attention(q, k, v):
    bsz, heads, lq, dqk = q.shape
    dv = v.shape[-1]
    nblk = lq // Q_BLOCK
    q_blocks = q.reshape(bsz, heads, nblk, Q_BLOCK, dqk).transpose(2, 0, 1, 3, 4)

    def attend(qb):
        s = jnp.einsum("bhqd,bhkd->bhqk", qb, k).astype(jnp.float32) * MLA_SCALE
        probs = jax.nn.softmax(s, axis=-1).astype(v.dtype)
        return jnp.einsum("bhqk,bhkd->bhqd", probs, v)

    o = lax.map(attend, q_blocks)
    return o.transpose(1, 2, 0, 3, 4).reshape(bsz, heads, lq, dv)


def merge_heads(o):
    bsz, heads, length, dv = o.shape
    return o.transpose(0, 2, 1, 3).reshape(bsz, length, heads * dv)


def mla_mixer(h_lat, h_ctx, p, with_ctx_out):
    qc, kc, vc = mla_project(h_ctx, p, with_pos=False)
    ql, kl, vl = mla_project(h_lat, p, with_pos=True)
    o_lat = blocked_softmax_attention(ql, jnp.concatenate([kc, kl], axis=2), jnp.concatenate([vc, vl], axis=2))
    y_lat = merge_heads(o_lat) @ p["w_out"]
    y_ctx = merge_heads(blocked_softmax_attention(qc, kc, vc)) @ p["w_out"] if with_ctx_out else None
    return y_lat, y_ctx


def chunk_gated_delta(q, k, v, g, beta, s0):
    bsz, heads, length, dk = q.shape
    dv = v.shape[-1]
    n = length // GDN_CHUNK
    out_dtype = v.dtype

    def chunked(t):
        return t.astype(jnp.float32).reshape(bsz, heads, n, GDN_CHUNK, *t.shape[3:])

    q, k, v, g, beta = chunked(q), chunked(k), chunked(v), chunked(g), chunked(beta)
    cum = jnp.cumsum(g, axis=-1)
    pos = jnp.arange(GDN_CHUNK)
    incl = pos[:, None] >= pos[None, :]
    strict = pos[:, None] > pos[None, :]
    gamma = jnp.exp(jnp.where(incl, cum[..., :, None] - cum[..., None, :], -jnp.inf))
    k_beta = k * beta[..., None]
    a_mat = jnp.where(strict, jnp.einsum("bhncd,bhnsd->bhncs", k_beta, k) * gamma, 0.0)
    rhs = jnp.concatenate([v * beta[..., None], k_beta * jnp.exp(cum)[..., None]], axis=-1)
    sol = lax.linalg.triangular_solve(a_mat + jnp.eye(GDN_CHUNK, dtype=jnp.float32), rhs, left_side=True, lower=True)
    u, w = sol[..., :dv], sol[..., dv:]
    qk = jnp.where(incl, jnp.einsum("bhncd,bhnsd->bhncs", q, k) * gamma, 0.0)
    q_dec = q * jnp.exp(cum)[..., None]
    g_last = cum[..., -1]
    k_dec = k * jnp.exp(g_last[..., None] - cum)[..., None]

    def step(state, xs):
        u_i, w_i, qd_i, qk_i, kd_i, gl_i = xs
        v_new = u_i - jnp.einsum("bhck,bhkv->bhcv", w_i, state)
        o_i = jnp.einsum("bhck,bhkv->bhcv", qd_i, state) + jnp.einsum("bhcs,bhsv->bhcv", qk_i, v_new)
        state = state * jnp.exp(gl_i)[..., None, None] + jnp.einsum("bhck,bhcv->bhkv", kd_i, v_new)
        return state, o_i

    xs = tuple(jnp.moveaxis(t, 2, 0) for t in (u, w, q_dec, qk, k_dec, g_last))
    state, o = lax.scan(step, s0.astype(jnp.float32), xs)
    o = jnp.moveaxis(o, 0, 2).reshape(bsz, heads, length, dv)
    return o.astype(out_dtype), state


def gdn_project(h, p):
    bsz, length, _ = h.shape
    hk, hv, nh = GDN_HEADS * GDN_DK, GDN_HEADS * GDN_DV, GDN_HEADS
    qkv, gate, a, b = jnp.split(h @ p["w_in"], [2 * hk + hv, 2 * hk + 2 * hv, 2 * hk + 2 * hv + 2 * nh], axis=-1)
    qkv = jax.nn.silu(centred_depthwise_conv(qkv, p["conv_w"]))
    q, k, v = jnp.split(qkv, [hk, 2 * hk], axis=-1)
    q = l2_normalize(q.reshape(bsz, length, nh, GDN_DK)).transpose(0, 2, 1, 3) * GDN_DK ** -0.5
    k = l2_normalize(k.reshape(bsz, length, nh, GDN_DK)).transpose(0, 2, 1, 3)
    v = v.reshape(bsz, length, nh, GDN_DV).transpose(0, 2, 1, 3)
    a = a.astype(jnp.float32).reshape(bsz, length, 2, nh).transpose(2, 0, 3, 1)
    beta = jax.nn.sigmoid(b.astype(jnp.float32).reshape(bsz, length, 2, nh).transpose(2, 0, 3, 1))
    g = -jnp.exp(p["a_log"].astype(jnp.float32))[:, None, :, None] * jax.nn.softplus(
        a + p["dt_bias"].astype(jnp.float32)[:, None, :, None])
    return q, k, v, g, beta, gate


def gdn_bidirectional(q, k, v, g, beta, s_fwd, s_bwd):
    o_f, s_f = chunk_gated_delta(q, k, v, g[0], beta[0], s_fwd)
    flip = lambda t: jnp.flip(t, axis=2)
    o_b, s_b = chunk_gated_delta(flip(q), flip(k), flip(v), flip(g[1]), flip(beta[1]), s_bwd)
    return o_f + flip(o_b), s_f, s_b


def gdn_output(o, gate, p):
    bsz, heads, length, _ = o.shape
    o = rms_norm(o.transpose(0, 2, 1, 3), p["o_norm"]) * jax.nn.silu(gate.reshape(bsz, length, heads, GDN_DV))
    return o.reshape(bsz, length, heads * GDN_DV) @ p["w_out"]


def gdn_mixer(h_lat, h_ctx, p, with_ctx_out):
    qc, kc, vc, gc, bc, zc = gdn_project(h_ctx, p)
    s0 = jnp.zeros((h_ctx.shape[0], GDN_HEADS, GDN_DK, GDN_DV), jnp.float32)
    o_ctx, s_fwd, s_bwd = gdn_bidirectional(qc, kc, vc, gc, bc, s0, s0)
    ql, kl, vl, gl, bl, zl = gdn_project(h_lat, p)
    o_lat, _, _ = gdn_bidirectional(ql, kl, vl, gl, bl, s_fwd, s_bwd)
    y_lat = gdn_output(o_lat, zl, p)
    y_ctx = gdn_output(o_ctx, zc, p) if with_ctx_out else None
    return y_lat, y_ctx


def setup_inputs(seed: int = 0) -> dict:
    key = jax.random.key(seed)
    ks = iter(jax.random.split(key, 48))
    f32 = jnp.float32
    nrm = lambda shape, scale: scale * jax.random.normal(next(ks), shape, f32)
    d = D_MODEL
    hf = HYENA_FILTER_HIDDEN
    gdn_in = 2 * GDN_HEADS * GDN_DK + 2 * GDN_HEADS * GDN_DV + 4 * GDN_HEADS
    gdn_conv = 2 * GDN_HEADS * GDN_DK + GDN_HEADS * GDN_DV
    a_init = jax.random.uniform(next(ks), (N_GDN, 2, GDN_HEADS), f32, 1.0, 16.0)
    dt = jnp.exp(jax.random.uniform(next(ks), (N_GDN, 2, GDN_HEADS), f32, math.log(1e-3), math.log(1e-1)))
    return {
        "x": nrm((BATCH, SEQ, d), 1.0),
        "c": nrm((BATCH, d), 1.0),
        "ctx": nrm((BATCH, CTX_LEN, d), 1.0),
        "c_ctx": nrm((d,), 1.0),
        "mod_w": nrm((DEPTH, d, 6 * d), d ** -0.5),
        "mod_b": nrm((DEPTH, 6 * d), 0.01),
        "ln_g": 1.0 + nrm((DEPTH, 2, d), 0.02),
        "ln_b": nrm((DEPTH, 2, d), 0.02),
        "ffn_w_in": nrm((DEPTH, d, 2 * D_FF), d ** -0.5),
        "ffn_w_out": nrm((DEPTH, D_FF, d), DEEPNORM_BETA * D_FF ** -0.5),
        "hy_w_in": nrm((N_HYENA, d, 3 * d), d ** -0.5),
        "hy_b_in": nrm((N_HYENA, 3 * d), 0.02),
        "hy_conv_w": nrm((N_HYENA, HYENA_SHORT, 3 * d), HYENA_SHORT ** -0.5),
        "hy_conv_b": nrm((N_HYENA, 3 * d), 0.02),
        "hy_fw1": nrm((N_HYENA, HYENA_EMB, hf), HYENA_EMB ** -0.5),
        "hy_fb1": nrm((N_HYENA, hf), 0.02),
        "hy_fw2": nrm((N_HYENA, hf, hf), hf ** -0.5),
        "hy_fb2": nrm((N_HYENA, hf), 0.02),
        "hy_fw3": nrm((N_HYENA, hf, hf), hf ** -0.5),
        "hy_fb3": nrm((N_HYENA, hf), 0.02),
        "hy_fw4": nrm((N_HYENA, hf, 2 * d), 0.05 * hf ** -0.5),
        "hy_freq": 1.0 + nrm((N_HYENA, hf), 0.02),
        "hy_skip": nrm((N_HYENA, d), 0.1),
        "hy_w_out": nrm((N_HYENA, d, d), DEEPNORM_BETA * d ** -0.5),
        "hy_b_out": nrm((N_HYENA, d), 0.02),
        "mla_w_in": nrm((N_MLA, d, MLA_Q_LORA + MLA_KV_LORA + MLA_ROPE), d ** -0.5),
        "mla_q_norm": 1.0 + nrm((N_MLA, MLA_Q_LORA), 0.02),
        "mla_kv_norm": 1.0 + nrm((N_MLA, MLA_KV_LORA), 0.02),
        "mla_w_uq": nrm((N_MLA, MLA_Q_LORA, MLA_HEADS * (MLA_NOPE + MLA_ROPE)), MLA_Q_LORA ** -0.5),
        "mla_w_ukv": nrm((N_MLA, MLA_KV_LORA, MLA_HEADS * (MLA_NOPE + MLA_V)), MLA_KV_LORA ** -0.5),
        "mla_w_out": nrm((N_MLA, MLA_HEADS * MLA_V, d), DEEPNORM_BETA * (MLA_HEADS * MLA_V) ** -0.5),
        "gdn_w_in": nrm((N_GDN, d, gdn_in), d ** -0.5),
        "gdn_conv_w": nrm((N_GDN, GDN_SHORT, gdn_conv), GDN_SHORT ** -0.5),
        "gdn_a_log": jnp.log(a_init),
        "gdn_dt_bias": dt + jnp.log(-jnp.expm1(-dt)),
        "gdn_o_norm": 1.0 + nrm((N_GDN, GDN_DV), 0.02),
        "gdn_w_out": nrm((N_GDN, GDN_HEADS * GDN_DV, d), DEEPNORM_BETA * (GDN_HEADS * GDN_DV) ** -0.5),
    }


def reference(x, c, ctx, c_ctx, mod_w, mod_b, ln_g, ln_b, ffn_w_in, ffn_w_out,
              hy_w_in, hy_b_in, hy_conv_w, hy_conv_b, hy_fw1, hy_fb1, hy_fw2, hy_fb2, hy_fw3, hy_fb3,
              hy_fw4, hy_freq, hy_skip, hy_w_out, hy_b_out,
              mla_w_in, mla_q_norm, mla_kv_norm, mla_w_uq, mla_w_ukv, mla_w_out,
              gdn_w_in, gdn_conv_w, gdn_a_log, gdn_dt_bias, gdn_o_norm, gdn_w_out):
    for i in range(DEPTH):
        kind, j = i % N_MIXERS, i // N_MIXERS
        ctx_out = any(l % N_MIXERS != MIXER_HYENA for l in range(i + 1, DEPTH))
        ml = [m[:, None, :] for m in adaln(c, mod_w[i], mod_b[i])]
        h_lat = x * (1.0 + ml[1]) + ml[0]
        if kind != MIXER_HYENA or ctx_out:
            mc = adaln(c_ctx, mod_w[i], mod_b[i])
            h_ctx = ctx * (1.0 + mc[1]) + mc[0]
        if kind == MIXER_HYENA:
            p = {"w_in": hy_w_in[j], "b_in": hy_b_in[j], "conv_w": hy_conv_w[j], "conv_b": hy_conv_b[j],
                 "fw1": hy_fw1[j], "fb1": hy_fb1[j], "fw2": hy_fw2[j], "fb2": hy_fb2[j],
                 "fw3": hy_fw3[j], "fb3": hy_fb3[j], "fw4": hy_fw4[j], "freq": hy_freq[j],
                 "skip": hy_skip[j], "w_out": hy_w_out[j], "b_out": hy_b_out[j]}
            y_lat = hyena_mixer(h_lat, p)
            y_ctx = hyena_mixer(h_ctx, p) if ctx_out else None
        elif kind == MIXER_MLA:
            p = {"w_in": mla_w_in[j], "q_norm": mla_q_norm[j], "kv_norm": mla_kv_norm[j],
                 "w_uq": mla_w_uq[j], "w_ukv": mla_w_ukv[j], "w_out": mla_w_out[j]}
            y_lat, y_ctx = mla_mixer(h_lat, h_ctx, p, ctx_out)
        else:
            p = {"w_in": gdn_w_in[j], "conv_w": gdn_conv_w[j], "a_log": gdn_a_log[j],
                 "dt_bias": gdn_dt_bias[j], "o_norm": gdn_o_norm[j], "w_out": gdn_w_out[j]}
            y_lat, y_ctx = gdn_mixer(h_lat, h_ctx, p, ctx_out)
        x = mixer_residual_and_ffn(x, y_lat, ml, ln_g[i], ln_b[i], ffn_w_in[i], ffn_w_out[i])
        if ctx_out:
            ctx = mixer_residual_and_ffn(ctx, y_ctx, mc, ln_g[i], ln_b[i], ffn_w_in[i], ffn_w_out[i])
    return x
```

```python
import functools
import math

import jax
import jax.numpy as jnp
from jax import lax
from jax.experimental import pallas as pl
from jax.experimental.pallas import tpu as pltpu

F32 = jnp.float32
BF16 = jnp.bfloat16

D_MODEL = 1024
DEPTH = 4
GRID_W = 64
N_MIXERS = 3
MIXER_HYENA = 0
MIXER_MLA = 1
DEEPNORM_ALPHA = (2 * DEPTH) ** 0.25
LN_EPS = 1e-5
RMS_EPS = 1e-6
HYENA_EMB = 33
HYENA_DECAY_TARGET = 1e-2
HYENA_FAST_DECAY_PCT = 0.3
HYENA_SLOW_DECAY_PCT = 1.5
MLA_HEADS = 8
MLA_Q_LORA = 384
MLA_KV_LORA = 256
MLA_NOPE = 128
MLA_ROPE = 64
MLA_V = 128
MLA_SCALE = (MLA_NOPE + MLA_ROPE) ** -0.5
MLA_HEAD_PAD = 256
ROPE_THETA = 10000.0
GDN_HEADS = 8
GDN_DK = 128
GDN_DV = 128
GDN_CHUNK = 64

V7X_LANES = 128
V7X_VMEM_LIMIT_BYTES = 56 * 1024 * 1024


def _mm_kernel(x_ref, w_ref, b_ref, o_ref):
    acc = jnp.dot(x_ref[...].astype(BF16), w_ref[...], preferred_element_type=F32)
    o_ref[...] = (acc + b_ref[...]).astype(o_ref.dtype)


def _pick_tile(n, candidates):
    for c in candidates:
        if n % c == 0:
            return c
    return n


def matmul(x, w, b=None, out_dtype=F32):
    m, k = x.shape
    n = w.shape[1]
    n_pad = -(-n // V7X_LANES) * V7X_LANES
    w = w.astype(BF16)
    b = jnp.zeros((n,), F32) if b is None else b.astype(F32)
    if n_pad != n:
        w = jnp.pad(w, ((0, 0), (0, n_pad - n)))
        b = jnp.pad(b, (0, n_pad - n))
    m_pad = -(-m // 8) * 8
    if m_pad != m:
        x = jnp.pad(x, ((0, m_pad - m), (0, 0)))
    tm = _pick_tile(m_pad, (1024, 512, 256, 128, 64, 32, 16, 8)) if k <= 1024 else _pick_tile(m_pad, (512, 256, 128, 64, 32, 16, 8))
    tn = _pick_tile(n_pad, (1024, 768, 512, 384, 256, 128))
    out = pl.pallas_call(
        _mm_kernel,
        grid=(m_pad // tm, n_pad // tn),
        in_specs=[
            pl.BlockSpec((tm, k), lambda i, j: (i, 0)),
            pl.BlockSpec((k, tn), lambda i, j: (0, j)),
            pl.BlockSpec((1, tn), lambda i, j: (0, j)),
        ],
        out_specs=pl.BlockSpec((tm, tn), lambda i, j: (i, j)),
        out_shape=jax.ShapeDtypeStruct((m_pad, n_pad), out_dtype),
        compiler_params=pltpu.CompilerParams(
            dimension_semantics=("arbitrary", "arbitrary"),
            vmem_limit_bytes=V7X_VMEM_LIMIT_BYTES),
        name="dense_projection",
    )(x, w, b.reshape(1, n_pad))
    return out[:m, :n]


def matmul_tokens(h, w, b=None):
    lead = h.shape[:-1]
    return matmul(h.reshape(-1, h.shape[-1]), w, b).reshape(*lead, w.shape[1])


def _attention_kernel(q_ref, k_ref, v_ref, o_ref):
    q = q_ref[0]
    k = k_ref[0]
    s = lax.dot_general(q, k, (((1,), (1,)), ((), ())), preferred_element_type=F32)
    m = jnp.max(s, axis=-1, keepdims=True)
    p = jnp.exp(s - m)
    l = jnp.sum(p, axis=-1, keepdims=True)
    o = jnp.dot(p.astype(BF16), v_ref[0], preferred_element_type=F32)
    o_ref[0] = (o / l).astype(o_ref.dtype)


def attention(q, k, v):
    bsz, lq, _ = q.shape
    lk = k.shape[1]
    tq = _pick_tile(lq, (512, 256, 128))
    return pl.pallas_call(
        _attention_kernel,
        grid=(bsz, MLA_HEADS, lq // tq),
        in_specs=[
            pl.BlockSpec((1, tq, MLA_HEAD_PAD), lambda b, h, i: (b, i, h)),
            pl.BlockSpec((1, lk, MLA_HEAD_PAD), lambda b, h, i: (b, 0, h)),
            pl.BlockSpec((1, lk, MLA_V), lambda b, h, i: (b, 0, h)),
        ],
        out_specs=pl.BlockSpec((1, tq, MLA_V), lambda b, h, i: (b, i, h)),
        out_shape=jax.ShapeDtypeStruct((bsz, lq, MLA_HEADS * MLA_V), F32),
        compiler_params=pltpu.CompilerParams(
            dimension_semantics=("arbitrary", "arbitrary", "arbitrary"),
            vmem_limit_bytes=V7X_VMEM_LIMIT_BYTES),
        name="mla_attention",
    )(q, k, v)


def layer_norm(x, g, b):
    mu = jnp.mean(x, axis=-1, keepdims=True)
    var = jnp.mean(jnp.square(x - mu), axis=-1, keepdims=True)
    return (x - mu) * lax.rsqrt(var + LN_EPS) * g + b


def rms_norm(x, g):
    return x * lax.rsqrt(jnp.mean(jnp.square(x), axis=-1, keepdims=True) + RMS_EPS) * g


def l2_normalize(x):
    return x * lax.rsqrt(jnp.sum(x * x, axis=-1, keepdims=True) + RMS_EPS)


def centred_depthwise_conv3(u, w):
    zero = jnp.zeros_like(u[:, :1])
    prev = jnp.concatenate([zero, u[:, :-1]], axis=1)
    nxt = jnp.concatenate([u[:, 1:], zero], axis=1)
    return prev * w[0] + u * w[1] + nxt * w[2]


def adaln(cvec, w, b):
    out = matmul(jax.nn.silu(cvec), w, b)
    return jnp.split(out, 6, axis=-1)


def swiglu(h, w_in, w_out):
    gate, up = jnp.split(matmul_tokens(h, w_in), 2, axis=-1)
    return matmul_tokens(jax.nn.silu(gate) * up, w_out)


def mixer_residual_and_ffn(s, y_mix, m, g, b, w_in, w_out):
    s = layer_norm(DEEPNORM_ALPHA * s + m[2] * y_mix, g[0], b[0])
    f = swiglu(s * (1.0 + m[4]) + m[3], w_in, w_out)
    return layer_norm(DEEPNORM_ALPHA * s + m[5] * f, g[1], b[1])


def hyena_position_features(length):
    t01 = jnp.linspace(0.0, 1.0, length, dtype=F32)[:, None]
    bands = (HYENA_EMB - 1) // 2
    w = (2.0 * math.pi / length) * jnp.arange(length, dtype=F32)
    f = jnp.linspace(1e-4, bands - 1, bands, dtype=F32)
    ang = w[:, None] * f[None, :]
    return jnp.concatenate([t01, jnp.cos(ang), -jnp.sin(ang)], axis=-1)


def hyena_decay_window(length):
    max_decay = math.log(HYENA_DECAY_TARGET) / HYENA_FAST_DECAY_PCT
    min_decay = math.log(HYENA_DECAY_TARGET) / HYENA_SLOW_DECAY_PCT
    deltas = jnp.abs(jnp.linspace(min_decay, max_decay, D_MODEL, dtype=F32))
    t = jnp.linspace(0.0, 1.0, length, dtype=F32)
    return jnp.exp(-t[:, None] * deltas[None, :])


def hyena_filters(length, p):
    hp = lax.Precision.HIGHEST
    freq = p["freq"]
    z = hyena_position_features(length)
    z = jnp.sin(freq * (jnp.dot(z, p["fw1"], precision=hp) + p["fb1"]))
    z = jnp.sin(freq * (jnp.dot(z, p["fw2"], precision=hp) + p["fb2"]))
    z = jnp.sin(freq * (jnp.dot(z, p["fw3"], precision=hp) + p["fb3"]))
    k_fwd, k_bwd = jnp.split(jnp.dot(z, p["fw4"], precision=hp), 2, axis=-1)
    window = hyena_decay_window(length)
    return k_fwd * window, k_bwd * window


def two_sided_fft_conv(u, k_fwd, k_bwd):
    length = u.shape[1]
    pad = ((0, length), (0, 0))
    taps = jnp.pad(k_fwd, pad) + jnp.roll(jnp.pad(k_bwd, pad)[::-1], 1, axis=0)
    spec = jnp.fft.rfft(u, n=2 * length, axis=1) * jnp.fft.rfft(taps, axis=0)[None]
    return jnp.fft.irfft(spec, n=2 * length, axis=1)[:, :length]


def hyena_mixer(h, p):
    length = h.shape[1]
    z = centred_depthwise_conv3(matmul_tokens(h, p["w_in"], p["b_in"]), p["conv_w"]) + p["conv_b"]
    x0, x1, v = jnp.split(z, 3, axis=-1)
    k_fwd, k_bwd = hyena_filters(length, p)
    u = x1 * v
    y = two_sided_fft_conv(u, k_fwd, k_bwd) + u * p["skip"]
    return matmul_tokens(x0 * y, p["w_out"], p["b_out"])


def axial_rope(t):
    length = t.shape[1]
    rows = length // GRID_W
    row, col = jnp.meshgrid(jnp.arange(rows, dtype=F32), jnp.arange(GRID_W, dtype=F32), indexing="ij")
    half = MLA_ROPE // 2
    quarter = half // 2
    inv_freq = ROPE_THETA ** (-jnp.arange(quarter, dtype=F32) / quarter)

    def rotate(seg, pos):
        ang = (pos[:, None] * inv_freq[None, :])[:, None, :]
        cos, sin = jnp.cos(ang), jnp.sin(ang)
        a, b = seg[..., :quarter], seg[..., quarter:]
        return jnp.concatenate([a * cos - b * sin, b * cos + a * sin], axis=-1)

    return jnp.concatenate([rotate(t[..., :half], row.reshape(-1)), rotate(t[..., half:], col.reshape(-1))], axis=-1)


def mla_project(h, p, with_pos):
    bsz, length, _ = h.shape
    cq, ckv, k_rope = jnp.split(matmul_tokens(h, p["w_in"]), [MLA_Q_LORA, MLA_Q_LORA + MLA_KV_LORA], axis=-1)
    q = matmul_tokens(rms_norm(cq, p["q_norm"]), p["w_uq"]).reshape(bsz, length, MLA_HEADS, MLA_NOPE + MLA_ROPE)
    kv = matmul_tokens(rms_norm(ckv, p["kv_norm"]), p["w_ukv"]).reshape(bsz, length, MLA_HEADS, MLA_NOPE + MLA_V)
    q_nope, q_rope = jnp.split(q, [MLA_NOPE], axis=-1)
    k_nope, v = jnp.split(kv, [MLA_NOPE], axis=-1)
    k_rope = k_rope[:, :, None, :]
    if with_pos:
        q_rope = axial_rope(q_rope)
        k_rope = axial_rope(k_rope)
    pad = jnp.zeros((bsz, length, MLA_HEADS, MLA_HEAD_PAD - MLA_NOPE - MLA_ROPE), F32)
    q = jnp.concatenate([q_nope, q_rope, pad], axis=-1) * MLA_SCALE
    k = jnp.concatenate([k_nope, jnp.broadcast_to(k_rope, (bsz, length, MLA_HEADS, MLA_ROPE)), pad], axis=-1)
    flat = lambda a: a.reshape(bsz, length, -1).astype(BF16)
    return flat(q), flat(k), flat(v)


def mla_mixer(h_lat, h_ctx, p, with_ctx_out):
    qc, kc, vc = mla_project(h_ctx, p, with_pos=False)
    ql, kl, vl = mla_project(h_lat, p, with_pos=True)
    o_lat = attention(ql, jnp.concatenate([kc, kl], axis=1), jnp.concatenate([vc, vl], axis=1))
    y_lat = matmul_tokens(o_lat, p["w_out"])
    y_ctx = matmul_tokens(attention(qc, kc, vc), p["w_out"]) if with_ctx_out else None
    return y_lat, y_ctx


def chunk_gated_delta(q, k, v, g, beta, s0):
    bsz, heads, length, dk = q.shape
    dv = v.shape[-1]
    n = length // GDN_CHUNK

    def chunked(t):
        return t.reshape(bsz, heads, n, GDN_CHUNK, *t.shape[3:])

    q, k, v, g, beta = chunked(q), chunked(k), chunked(v), chunked(g), chunked(beta)
    cum = jnp.cumsum(g, axis=-1)
    pos = jnp.arange(GDN_CHUNK)
    incl = pos[:, None] >= pos[None, :]
    strict = pos[:, None] > pos[None, :]
    gamma = jnp.exp(jnp.where(incl, cum[..., :, None] - cum[..., None, :], -jnp.inf))
    k_beta = k * beta[..., None]
    a_mat = jnp.where(strict, jnp.einsum("bhncd,bhnsd->bhncs", k_beta, k) * gamma, 0.0)
    rhs = jnp.concatenate([v * beta[..., None], k_beta * jnp.exp(cum)[..., None]], axis=-1)
    sol = lax.linalg.triangular_solve(a_mat + jnp.eye(GDN_CHUNK, dtype=F32), rhs, left_side=True, lower=True)
    u, w = sol[..., :dv], sol[..., dv:]
    qk = jnp.where(incl, jnp.einsum("bhncd,bhnsd->bhncs", q, k) * gamma, 0.0)
    q_dec = q * jnp.exp(cum)[..., None]
    g_last = cum[..., -1]
    k_dec = k * jnp.exp(g_last[..., None] - cum)[..., None]

    def step(state, xs):
        u_i, w_i, qd_i, qk_i, kd_i, gl_i = xs
        v_new = u_i - jnp.einsum("bhck,bhkv->bhcv", w_i, state)
        o_i = jnp.einsum("bhck,bhkv->bhcv", qd_i, state) + jnp.einsum("bhcs,bhsv->bhcv", qk_i, v_new)
        state = state * jnp.exp(gl_i)[..., None, None] + jnp.einsum("bhck,bhcv->bhkv", kd_i, v_new)
        return state, o_i

    xs = tuple(jnp.moveaxis(t, 2, 0) for t in (u, w, q_dec, qk, k_dec, g_last))
    state, o = lax.scan(step, s0, xs)
    o = jnp.moveaxis(o, 0, 2).reshape(bsz, heads, length, dv)
    return o, state


def gdn_project(h, p):
    bsz, length, _ = h.shape
    hk, hv, nh = GDN_HEADS * GDN_DK, GDN_HEADS * GDN_DV, GDN_HEADS
    qkv, gate, a, b = jnp.split(matmul_tokens(h, p["w_in"]),
                                [2 * hk + hv, 2 * hk + 2 * hv, 2 * hk + 2 * hv + 2 * nh], axis=-1)
    qkv = jax.nn.silu(centred_depthwise_conv3(qkv, p["conv_w"]))
    q, k, v = jnp.split(qkv, [hk, 2 * hk], axis=-1)
    q = l2_normalize(q.reshape(bsz, length, nh, GDN_DK)).transpose(0, 2, 1, 3) * GDN_DK ** -0.5
    k = l2_normalize(k.reshape(bsz, length, nh, GDN_DK)).transpose(0, 2, 1, 3)
    v = v.reshape(bsz, length, nh, GDN_DV).transpose(0, 2, 1, 3)
    a = a.reshape(bsz, length, 2, nh).transpose(2, 0, 3, 1)
    beta = jax.nn.sigmoid(b.reshape(bsz, length, 2, nh).transpose(2, 0, 3, 1))
    g = -jnp.exp(p["a_log"])[:, None, :, None] * jax.nn.softplus(a + p["dt_bias"][:, None, :, None])
    return q, k, v, g, beta, gate


def gdn_bidirectional(q, k, v, g, beta, s_fwd, s_bwd):
    o_f, s_f = chunk_gated_delta(q, k, v, g[0], beta[0], s_fwd)
    flip = lambda t: jnp.flip(t, axis=2)
    o_b, s_b = chunk_gated_delta(flip(q), flip(k), flip(v), flip(g[1]), flip(beta[1]), s_bwd)
    return o_f + flip(o_b), s_f, s_b


def gdn_output(o, gate, p):
    bsz, heads, length, _ = o.shape
    o = rms_norm(o.transpose(0, 2, 1, 3), p["o_norm"]) * jax.nn.silu(gate.reshape(bsz, length, heads, GDN_DV))
    return matmul_tokens(o.reshape(bsz, length, heads * GDN_DV), p["w_out"])


def gdn_mixer(h_lat, h_ctx, p, with_ctx_out):
    qc, kc, vc, gc, bc, zc = gdn_project(h_ctx, p)
    s0 = jnp.zeros((h_ctx.shape[0], GDN_HEADS, GDN_DK, GDN_DV), F32)
    o_ctx, s_fwd, s_bwd = gdn_bidirectional(qc, kc, vc, gc, bc, s0, s0)
    ql, kl, vl, gl, bl, zl = gdn_project(h_lat, p)
    o_lat, _, _ = gdn_bidirectional(ql, kl, vl, gl, bl, s_fwd, s_bwd)
    y_lat = gdn_output(o_lat, zl, p)
    y_ctx = gdn_output(o_ctx, zc, p) if with_ctx_out else None
    return y_lat, y_ctx


def kernel(x, c, ctx, c_ctx, mod_w, mod_b, ln_g, ln_b, ffn_w_in, ffn_w_out, hy_w_in, hy_b_in, hy_conv_w, hy_conv_b, hy_fw1, hy_fb1, hy_fw2, hy_fb2, hy_fw3, hy_fb3, hy_fw4, hy_freq, hy_skip, hy_w_out, hy_b_out, mla_w_in, mla_q_norm, mla_kv_norm, mla_w_uq, mla_w_ukv, mla_w_out, gdn_w_in, gdn_conv_w, gdn_a_log, gdn_dt_bias, gdn_o_norm, gdn_w_out):
    for i in range(DEPTH):
        kind, j = i % N_MIXERS, i // N_MIXERS
        ctx_out = any(l % N_MIXERS != MIXER_HYENA for l in range(i + 1, DEPTH))
        mods = adaln(jnp.concatenate([c, c_ctx[None]], axis=0), mod_w[i], mod_b[i])
        ml = [m[:-1, None, :] for m in mods]
        mc = [m[-1] for m in mods]
        h_lat = x * (1.0 + ml[1]) + ml[0]
        if kind != MIXER_HYENA or ctx_out:
            h_ctx = ctx * (1.0 + mc[1]) + mc[0]
        if kind == MIXER_HYENA:
            p = {"w_in": hy_w_in[j], "b_in": hy_b_in[j], "conv_w": hy_conv_w[j], "conv_b": hy_conv_b[j],
                 "fw1": hy_fw1[j], "fb1": hy_fb1[j], "fw2": hy_fw2[j], "fb2": hy_fb2[j],
                 "fw3": hy_fw3[j], "fb3": hy_fb3[j], "fw4": hy_fw4[j], "freq": hy_freq[j],
                 "skip": hy_skip[j], "w_out": hy_w_out[j], "b_out": hy_b_out[j]}
            y_lat = hyena_mixer(h_lat, p)
            y_ctx = hyena_mixer(h_ctx, p) if ctx_out else None
        elif kind == MIXER_MLA:
            p = {"w_in": mla_w_in[j], "q_norm": mla_q_norm[j], "kv_norm": mla_kv_norm[j],
                 "w_uq": mla_w_uq[j], "w_ukv": mla_w_ukv[j], "w_out": mla_w_out[j]}
            y_lat, y_ctx = mla_mixer(h_lat, h_ctx, p, ctx_out)
        else:
            p = {"w_in": gdn_w_in[j], "conv_w": gdn_conv_w[j], "a_log": gdn_a_log[j],
                 "dt_bias": gdn_dt_bias[j], "o_norm": gdn_o_norm[j], "w_out": gdn_w_out[j]}
            y_lat, y_ctx = gdn_mixer(h_lat, h_ctx, p, ctx_out)
        x = mixer_residual_and_ffn(x, y_lat, ml, ln_g[i], ln_b[i], ffn_w_in[i], ffn_w_out[i])
        if ctx_out:
            ctx = mixer_residual_and_ffn(ctx, y_ctx, mc, ln_g[i], ln_b[i], ffn_w_in[i], ffn_w_out[i])
    return x
```

```python
import functools
import math

import numpy as np

import jax
import jax.numpy as jnp
from jax import lax
from jax.experimental import pallas as pl
from jax.experimental.pallas import tpu as pltpu

F32 = jnp.float32
BF16 = jnp.bfloat16
HIGHEST = lax.Precision.HIGHEST

D_MODEL = 1024
DEPTH = 4
GRID_W = 64
N_MIXERS = 3
MIXER_HYENA = 0
MIXER_MLA = 1
DEEPNORM_ALPHA = (2 * DEPTH) ** 0.25
LN_EPS = 1e-5
RMS_EPS = 1e-6
D_FF = -(-8 * D_MODEL // (3 * 256)) * 256
HYENA_EMB = 33
HYENA_FILTER_HIDDEN = 64
HYENA_DECAY_TARGET = 1e-2
HYENA_FAST_DECAY_PCT = 0.3
HYENA_SLOW_DECAY_PCT = 1.5
MLA_HEADS = 8
MLA_Q_LORA = 384
MLA_KV_LORA = 256
MLA_NOPE = 128
MLA_ROPE = 64
MLA_V = 128
MLA_SCALE = (MLA_NOPE + MLA_ROPE) ** -0.5
MLA_QK = 256
ROPE_THETA = 10000.0
GDN_HEADS = 8
GDN_DK = 128
GDN_DV = 128
GDN_CHUNK = 64

V7X_LANES = 128
V7X_MXU_DIM = 256
V7X_VMEM_LIMIT_BYTES = 56 * 1024 * 1024
ROW_TILE = 1024


def _params(n_axes):
    return pltpu.CompilerParams(dimension_semantics=("arbitrary",) * n_axes,
                                vmem_limit_bytes=V7X_VMEM_LIMIT_BYTES)


def _mod_row_map(tile, seq, batch):
    return lambda i: (jnp.minimum(i * tile // seq, batch), 0, 0)


def _adaln_kernel(c_ref, w_ref, b_ref, o_ref):
    x = c_ref[...]
    x = (x * jax.nn.sigmoid(x)).astype(BF16)
    o_ref[0] = jnp.dot(x, w_ref[0].astype(BF16), preferred_element_type=F32) + b_ref[0]


def adaln_all(cc, mod_w, mod_b):
    r, d = cc.shape
    depth, _, n = mod_w.shape
    tn = 1536
    return pl.pallas_call(
        _adaln_kernel,
        grid=(depth, n // tn),
        in_specs=[pl.BlockSpec((r, d), lambda l, j: (0, 0)),
                  pl.BlockSpec((1, d, tn), lambda l, j: (l, 0, j)),
                  pl.BlockSpec((1, 1, tn), lambda l, j: (l, 0, j))],
        out_specs=pl.BlockSpec((1, r, tn), lambda l, j: (l, 0, j)),
        out_shape=jax.ShapeDtypeStruct((depth, r, n), F32),
        compiler_params=_params(2), name="adaln_table",
    )(cc, mod_w, mod_b.reshape(depth, 1, n))


def _modulate_kernel(s_ref, scale_ref, shift_ref, h_ref):
    h_ref[...] = (s_ref[...] * (1.0 + scale_ref[0]) + shift_ref[0]).astype(h_ref.dtype)


def modulate(s, scale, shift, seq, batch):
    rows, d = s.shape
    tm = ROW_TILE
    mod_spec = pl.BlockSpec((1, 1, d), _mod_row_map(tm, seq, batch))
    return pl.pallas_call(
        _modulate_kernel,
        grid=(rows // tm,),
        in_specs=[pl.BlockSpec((tm, d), lambda i: (i, 0)), mod_spec, mod_spec],
        out_specs=pl.BlockSpec((tm, d), lambda i: (i, 0)),
        out_shape=jax.ShapeDtypeStruct((rows, d), BF16),
        compiler_params=_params(1), name="modulate",
    )(s, scale, shift)


def _proj_kernel(x_ref, w_ref, o_ref):
    o_ref[...] = jnp.dot(x_ref[...], w_ref[...], preferred_element_type=F32).astype(o_ref.dtype)


def project(x, w, out_dtype=F32, tm=ROW_TILE, tn=None):
    m, k = x.shape
    n = w.shape[1]
    tn = n if tn is None else tn
    return pl.pallas_call(
        _proj_kernel,
        grid=(m // tm, n // tn),
        in_specs=[pl.BlockSpec((tm, k), lambda i, j: (i, 0)),
                  pl.BlockSpec((k, tn), lambda i, j: (0, j))],
        out_specs=pl.BlockSpec((tm, tn), lambda i, j: (i, j)),
        out_shape=jax.ShapeDtypeStruct((m, n), out_dtype),
        compiler_params=_params(2), name="projection",
    )(x, w)


def _out_ln_kernel(x_ref, w_ref, b_ref, s_ref, gate_ref, lng_ref, lnb_ref, scale_ref, shift_ref,
                   s_out_ref, h_out_ref):
    y = jnp.dot(x_ref[...].astype(BF16), w_ref[...], preferred_element_type=F32) + b_ref[...]
    z = DEEPNORM_ALPHA * s_ref[...] + gate_ref[0] * y
    mu = jnp.mean(z, axis=-1, keepdims=True)
    zc = z - mu
    var = jnp.mean(zc * zc, axis=-1, keepdims=True)
    sn = zc * lax.rsqrt(var + LN_EPS) * lng_ref[...] + lnb_ref[...]
    s_out_ref[...] = sn
    h_out_ref[...] = (sn * (1.0 + scale_ref[0]) + shift_ref[0]).astype(h_out_ref.dtype)


def out_ln(x, w, b, s, gate, ln_g, ln_b, scale, shift, rows, seq, batch, tm):
    k = x.shape[1]
    d = w.shape[1]
    mod_spec = pl.BlockSpec((1, 1, d), _mod_row_map(tm, seq, batch))
    vec_spec = pl.BlockSpec((1, d), lambda i: (0, 0))
    row_spec = pl.BlockSpec((tm, d), lambda i: (i, 0))
    return pl.pallas_call(
        _out_ln_kernel,
        grid=(rows // tm,),
        in_specs=[pl.BlockSpec((tm, k), lambda i: (i, 0)),
                  pl.BlockSpec((k, d), lambda i: (0, 0)),
                  vec_spec, row_spec, mod_spec, vec_spec, vec_spec, mod_spec, mod_spec],
        out_specs=[row_spec, row_spec],
        out_shape=[jax.ShapeDtypeStruct((rows, d), F32), jax.ShapeDtypeStruct((rows, d), BF16)],
        compiler_params=_params(1), name="out_proj_ln",
    )(x, w, b.reshape(1, d), s, gate, ln_g.reshape(1, d), ln_b.reshape(1, d), scale, shift)


def _ffn_in_kernel(x_ref, w_ref, a_ref):
    x = x_ref[...]
    dff = a_ref.shape[1]
    for c in range(dff // V7X_MXU_DIM):
        lo = c * V7X_MXU_DIM
        g = jnp.dot(x, w_ref[:, lo:lo + V7X_MXU_DIM], preferred_element_type=F32)
        u = jnp.dot(x, w_ref[:, dff + lo:dff + lo + V7X_MXU_DIM], preferred_element_type=F32)
        a_ref[:, lo:lo + V7X_MXU_DIM] = (g * jax.nn.sigmoid(g) * u).astype(a_ref.dtype)


def ffn_in(h, w_in, rows):
    k = h.shape[1]
    dff = w_in.shape[1] // 2
    tm = ROW_TILE
    return pl.pallas_call(
        _ffn_in_kernel,
        grid=(rows // tm,),
        in_specs=[pl.BlockSpec((tm, k), lambda i: (i, 0)),
                  pl.BlockSpec((k, 2 * dff), lambda i: (0, 0))],
        out_specs=pl.BlockSpec((tm, dff), lambda i: (i, 0)),
        out_shape=jax.ShapeDtypeStruct((rows, dff), BF16),
        compiler_params=_params(1), name="ffn_swiglu_in",
    )(h, w_in)


def _hyena_in_kernel(x_ref, w0_ref, w1_ref, w2_ref, b0_ref, b1_ref, b2_ref, cw0_ref, cw1_ref, cw2_ref,
                     cb0_ref, cb1_ref, cb2_ref, x0_ref, u_ref):
    x = x_ref[...]
    length = x.shape[0]
    row = lax.broadcasted_iota(jnp.int32, (length, 1), 0)

    def branch(w_ref, b_ref, cw_ref, cb_ref):
        z = jnp.dot(x, w_ref[...], preferred_element_type=F32) + b_ref[...]
        prev = jnp.where(row == 0, 0.0, pltpu.roll(z, 1, 0))
        nxt = jnp.where(row == length - 1, 0.0, pltpu.roll(z, length - 1, 0))
        cw = cw_ref[...]
        return prev * cw[0:1] + z * cw[1:2] + nxt * cw[2:3] + cb_ref[...]

    x0 = branch(w0_ref, b0_ref, cw0_ref, cb0_ref)
    x1 = branch(w1_ref, b1_ref, cw1_ref, cb1_ref)
    v = branch(w2_ref, b2_ref, cw2_ref, cb2_ref)
    x0_ref[...] = x0.astype(x0_ref.dtype)
    u_ref[...] = (x1 * v).astype(u_ref.dtype)


def hyena_in(h, w_in, b_in, conv_w, conv_b, length, n_seq, row_offset):
    k = h.shape[1]
    d = w_in.shape[1] // 3
    tn = 256
    nb = d // tn
    off = row_offset // length
    w_specs = [pl.BlockSpec((k, tn), lambda b, j, q=q: (0, j + q * nb)) for q in range(3)]
    v_specs = [pl.BlockSpec((1, tn), lambda b, j, q=q: (0, j + q * nb)) for q in range(3)]
    cw_specs = [pl.BlockSpec((3, tn), lambda b, j, q=q: (0, j + q * nb)) for q in range(3)]
    out_spec = pl.BlockSpec((length, tn), lambda b, j: (b, j))
    b2, cb2 = b_in.reshape(1, 3 * d), conv_b.reshape(1, 3 * d)
    return pl.pallas_call(
        _hyena_in_kernel,
        grid=(n_seq, nb),
        in_specs=[pl.BlockSpec((length, k), lambda b, j: (b + off, 0))] + w_specs + v_specs + cw_specs + v_specs,
        out_specs=[out_spec, out_spec],
        out_shape=[jax.ShapeDtypeStruct((n_seq * length, d), BF16)] * 2,
        compiler_params=_params(2), name="hyena_in_proj_conv3",
    )(h, w_in, w_in, w_in, b2, b2, b2, conv_w, conv_w, conv_w, cb2, cb2, cb2)


def _hyena_filter_kernel(feat_ref, w1_ref, b1_ref, w2_ref, b2_ref, w3_ref, b3_ref, w4_ref, freq_ref, win_ref, k_ref):
    freq = freq_ref[...]
    z = jnp.sin(freq * (jnp.dot(feat_ref[...], w1_ref[...], precision=HIGHEST, preferred_element_type=F32) + b1_ref[...]))
    z = jnp.sin(freq * (jnp.dot(z, w2_ref[...], precision=HIGHEST, preferred_element_type=F32) + b2_ref[...]))
    z = jnp.sin(freq * (jnp.dot(z, w3_ref[...], precision=HIGHEST, preferred_element_type=F32) + b3_ref[...]))
    taps = jnp.dot(z, w4_ref[...], precision=HIGHEST, preferred_element_type=F32)
    d = win_ref.shape[1]
    win = win_ref[...]
    k_fwd = taps[:, :d] * win
    k_bwd = taps[:, d:] * win
    tile = k_fwd.shape[0]
    first = (lax.broadcasted_iota(jnp.int32, (tile, 1), 0) + pl.program_id(0) * tile) == 0
    k_ref[:, :d] = jnp.where(first, k_fwd + k_bwd, k_fwd).astype(k_ref.dtype)
    k_ref[:, d:] = jnp.where(first, 0.0, k_bwd).astype(k_ref.dtype)


def _hyena_features(length):
    t01 = np.linspace(0.0, 1.0, length, dtype=np.float32)[:, None]
    bands = (HYENA_EMB - 1) // 2
    w = (np.float32(2.0 * math.pi / length) * np.arange(length, dtype=np.float32)).astype(np.float32)
    f = np.linspace(1e-4, bands - 1, bands, dtype=np.float32)
    ang = (w[:, None] * f[None, :]).astype(np.float32)
    feat = np.concatenate([t01, np.cos(ang), -np.sin(ang)], axis=-1).astype(np.float32)
    return np.pad(feat, ((0, 0), (0, V7X_LANES - HYENA_EMB)))


def _hyena_window(length):
    max_decay = math.log(HYENA_DECAY_TARGET) / HYENA_FAST_DECAY_PCT
    min_decay = math.log(HYENA_DECAY_TARGET) / HYENA_SLOW_DECAY_PCT
    deltas = np.abs(np.linspace(min_decay, max_decay, D_MODEL, dtype=np.float32))
    t = np.linspace(0.0, 1.0, length, dtype=np.float32)
    return np.exp(-t[:, None] * deltas[None, :]).astype(np.float32)


def hyena_filter_taps(length, p):
    hid = V7X_LANES
    pad_h = hid - HYENA_FILTER_HIDDEN
    padc = lambda a: jnp.pad(a, ((0, 0), (0, pad_h)))
    w1 = jnp.pad(p["fw1"], ((0, V7X_LANES - HYENA_EMB), (0, pad_h)))
    w2 = jnp.pad(p["fw2"], ((0, pad_h), (0, pad_h)))
    w3 = jnp.pad(p["fw3"], ((0, pad_h), (0, pad_h)))
    w4 = jnp.pad(p["fw4"], ((0, pad_h), (0, 0)))
    vec = lambda a: padc(a.reshape(1, -1))
    args = (jnp.asarray(_hyena_features(length)), w1, vec(p["fb1"]), w2, vec(p["fb2"]), w3, vec(p["fb3"]), w4,
            vec(p["freq"]), jnp.asarray(_hyena_window(length)))
    tl = min(256, length)
    full = lambda a: pl.BlockSpec(a.shape, lambda i: (0, 0))
    return pl.pallas_call(
        _hyena_filter_kernel,
        grid=(length // tl,),
        in_specs=[pl.BlockSpec((tl, hid), lambda i: (i, 0))] + [full(a) for a in args[1:9]]
                 + [pl.BlockSpec((tl, D_MODEL), lambda i: (i, 0))],
        out_specs=pl.BlockSpec((tl, 2 * D_MODEL), lambda i: (i, 0)),
        out_shape=jax.ShapeDtypeStruct((length, 2 * D_MODEL), BF16),
        compiler_params=_params(1), name="hyena_filter",
    )(*args)


@functools.lru_cache(maxsize=None)
def _negacyclic_dft(length):
    n = 2 * length
    k = np.arange(length, dtype=np.int64)
    turns = ((2 * k[:, None] + 1) * k[None, :]) % (2 * n)
    ang = turns.astype(np.float64) * (2.0 * math.pi / (2 * n))
    c, s = np.cos(ang), np.sin(ang)
    inv = 2.0 / n
    return c, s, (c.T * inv), (s.T * inv)


def _dft_spec_kernel(c_ref, s_ref, u_ref, p_ref):
    u = u_ref[...]
    p_ref[0] = jnp.dot(c_ref[...], u, preferred_element_type=F32)
    p_ref[1] = jnp.dot(s_ref[...], u, preferred_element_type=F32)


def _dft_fwd_kernel(c_ref, s_ref, u_ref, hr_ref, hi_ref, y_ref):
    u = u_ref[...]
    ur = jnp.dot(c_ref[...], u, preferred_element_type=F32)
    us = jnp.dot(s_ref[...], u, preferred_element_type=F32)
    hr, hi = hr_ref[...], hi_ref[...]
    y_ref[0, 0] = (ur * hr + us * hi).astype(y_ref.dtype)
    y_ref[0, 1] = (us * hr - ur * hi).astype(y_ref.dtype)


def _dft_inv_kernel(ct_ref, st_ref, y_ref, u_ref, x0_ref, skip_ref, o_ref):
    y = (jnp.dot(ct_ref[...], y_ref[0, 0], preferred_element_type=F32)
         + jnp.dot(st_ref[...], y_ref[0, 1], preferred_element_type=F32))
    u = u_ref[...].astype(F32)
    o_ref[...] = ((y + u * skip_ref[...]) * x0_ref[...].astype(F32)).astype(o_ref.dtype)


def hyena_long_conv(x0, u, taps, skip, length, n_seq):
    d = u.shape[1]
    c, s, ct, st = (jnp.asarray(a, dtype=BF16) for a in _negacyclic_dft(length))
    tf = min(512, length)
    nf = length // tf
    spec = pl.pallas_call(
        _dft_spec_kernel,
        grid=(nf,),
        in_specs=[pl.BlockSpec((tf, length), lambda i: (i, 0)),
                  pl.BlockSpec((tf, length), lambda i: (i, 0)),
                  pl.BlockSpec((length, 2 * d), lambda i: (0, 0))],
        out_specs=pl.BlockSpec((2, tf, 2 * d), lambda i: (0, i, 0)),
        out_shape=jax.ShapeDtypeStruct((2, length, 2 * d), F32),
        compiler_params=_params(1), name="hyena_filter_spectrum",
    )(c, s, taps)
    h_re = spec[0, :, :d] + spec[0, :, d:]
    h_im = spec[1, :, d:] - spec[1, :, :d]
    y = pl.pallas_call(
        _dft_fwd_kernel,
        grid=(n_seq, nf),
        in_specs=[pl.BlockSpec((tf, length), lambda b, i: (i, 0)),
                  pl.BlockSpec((tf, length), lambda b, i: (i, 0)),
                  pl.BlockSpec((length, d), lambda b, i: (b, 0)),
                  pl.BlockSpec((tf, d), lambda b, i: (i, 0)),
                  pl.BlockSpec((tf, d), lambda b, i: (i, 0))],
        out_specs=pl.BlockSpec((1, 2, tf, d), lambda b, i: (b, 0, i, 0)),
        out_shape=jax.ShapeDtypeStruct((n_seq, 2, length, d), BF16),
        compiler_params=_params(2), name="hyena_dft_forward",
    )(c, s, u, h_re, h_im)
    return pl.pallas_call(
        _dft_inv_kernel,
        grid=(n_seq, nf),
        in_specs=[pl.BlockSpec((tf, length), lambda b, i: (i, 0)),
                  pl.BlockSpec((tf, length), lambda b, i: (i, 0)),
                  pl.BlockSpec((1, 2, length, d), lambda b, i: (b, 0, 0, 0)),
                  pl.BlockSpec((tf, d), lambda b, i: (b * nf + i, 0)),
                  pl.BlockSpec((tf, d), lambda b, i: (b * nf + i, 0)),
                  pl.BlockSpec((1, d), lambda b, i: (0, 0))],
        out_specs=pl.BlockSpec((tf, d), lambda b, i: (b * nf + i, 0)),
        out_shape=jax.ShapeDtypeStruct((n_seq * length, d), BF16),
        compiler_params=_params(2), name="hyena_dft_inverse",
    )(ct, st, y, u, x0, skip.reshape(1, d))


def hyena_mixer_core(h, p, length, n_seq, row_offset):
    x0, u = hyena_in(h, p["w_in"], p["b_in"], p["conv_w"], p["conv_b"], length, n_seq, row_offset)
    taps = hyena_filter_taps(length, p)
    return hyena_long_conv(x0, u, taps, p["skip"], length, n_seq)


def _rope_swap_columns():
    half, quarter = MLA_ROPE // 2, MLA_ROPE // 4
    r = np.arange(MLA_ROPE)
    within = r % half
    src = np.where(within < quarter, r + quarter, r - quarter)
    sign = np.where(within < quarter, -1.0, 1.0).astype(np.float32)
    return src, sign


@functools.lru_cache(maxsize=None)
def _rope_tables(seq, ctx_rows):
    half, quarter = MLA_ROPE // 2, MLA_ROPE // 4
    inv_freq = (np.float32(ROPE_THETA) ** (-np.arange(quarter, dtype=np.float32) / quarter)).astype(np.float32)
    t = np.arange(seq)
    row, col = (t // GRID_W).astype(np.float32), (t % GRID_W).astype(np.float32)
    r = np.arange(MLA_ROPE)
    pos = np.where((r // half)[None, :] == 0, row[:, None], col[:, None]).astype(np.float32)
    ang = (pos * inv_freq[r % quarter][None, :]).astype(np.float32).astype(np.float64)
    cos = np.concatenate([np.cos(ang), np.ones((ctx_rows, MLA_ROPE))], axis=0)
    sin = np.concatenate([np.sin(ang), np.zeros((ctx_rows, MLA_ROPE))], axis=0)
    ones = np.ones((seq + ctx_rows, MLA_NOPE))
    tab_q = (np.concatenate([ones, cos, sin], axis=1) * MLA_SCALE).astype(np.float32)
    tab_k = np.concatenate([cos, sin], axis=1).astype(np.float32)
    return tab_q, tab_k


def _rms_in_kernel(x_ref, g_ref):
    x = x_ref[...]
    return (x * lax.rsqrt(jnp.mean(x * x, axis=-1, keepdims=True) + RMS_EPS) * g_ref[...]).astype(BF16)


def _mla_q_kernel(x_ref, g_ref, w_ref, tab_ref, q_ref):
    acc = jnp.dot(_rms_in_kernel(x_ref, g_ref), w_ref[...], preferred_element_type=F32)
    tab = tab_ref[...]
    for h in range(MLA_HEADS):
        q_ref[:, h * MLA_QK:(h + 1) * MLA_QK] = (acc[:, h * MLA_QK:(h + 1) * MLA_QK] * tab).astype(q_ref.dtype)


def _mla_kv_kernel(x_ref, g_ref, w_ref, rope_ref, tab_ref, kv_ref, kr_ref):
    kv_ref[...] = jnp.dot(_rms_in_kernel(x_ref, g_ref), w_ref[...], preferred_element_type=F32).astype(kv_ref.dtype)
    prod = rope_ref[...] * tab_ref[...]
    kr_ref[...] = (prod + pltpu.roll(prod, MLA_ROPE, 1)).astype(kr_ref.dtype)


def mla_qkv(h, p, rows, seq, batch):
    tm = ROW_TILE
    src, sign = _rope_swap_columns()
    w_in = p["w_in"]
    o_kv, o_rope = MLA_Q_LORA, MLA_Q_LORA + MLA_KV_LORA
    w_rope = w_in[:, o_rope:]
    w_cat = jnp.concatenate([w_in[:, o_kv:o_rope], w_rope, w_rope[:, src] * sign, w_in[:, :o_kv]], axis=1).astype(BF16)
    z = project(h[:rows], w_cat)
    w_uq = p["w_uq"].reshape(MLA_Q_LORA, MLA_HEADS, MLA_NOPE + MLA_ROPE)
    w_uq_rope = w_uq[:, :, MLA_NOPE:]
    w_uq = jnp.concatenate([w_uq, w_uq_rope[:, :, src] * sign], axis=2).reshape(MLA_Q_LORA, MLA_HEADS * MLA_QK).astype(BF16)
    tab_q, tab_k = _rope_tables(seq, tm)
    lat_tiles = batch * seq // tm
    tab_map = lambda i: (jnp.where(i < lat_tiles, i % (seq // tm), seq // tm), 0)
    n_tiles = rows // tm
    q = pl.pallas_call(
        _mla_q_kernel,
        grid=(n_tiles,),
        in_specs=[pl.BlockSpec((tm, MLA_Q_LORA), lambda i: (i, 1)),
                  pl.BlockSpec((1, MLA_Q_LORA), lambda i: (0, 0)),
                  pl.BlockSpec((MLA_Q_LORA, MLA_HEADS * MLA_QK), lambda i: (0, 0)),
                  pl.BlockSpec((tm, MLA_QK), tab_map)],
        out_specs=pl.BlockSpec((tm, MLA_HEADS * MLA_QK), lambda i: (i, 0)),
        out_shape=jax.ShapeDtypeStruct((rows, MLA_HEADS * MLA_QK), BF16),
        compiler_params=_params(1), name="mla_q_proj",
    )(z, p["q_norm"].reshape(1, -1), w_uq, jnp.asarray(tab_q))
    kv, kr = pl.pallas_call(
        _mla_kv_kernel,
        grid=(n_tiles,),
        in_specs=[pl.BlockSpec((tm, MLA_KV_LORA), lambda i: (i, 0)),
                  pl.BlockSpec((1, MLA_KV_LORA), lambda i: (0, 0)),
                  pl.BlockSpec((MLA_KV_LORA, MLA_HEADS * (MLA_NOPE + MLA_V)), lambda i: (0, 0)),
                  pl.BlockSpec((tm, 2 * MLA_ROPE), lambda i: (i, 2)),
                  pl.BlockSpec((tm, 2 * MLA_ROPE), tab_map)],
        out_specs=[pl.BlockSpec((tm, MLA_HEADS * (MLA_NOPE + MLA_V)), lambda i: (i, 0)),
                   pl.BlockSpec((tm, 2 * MLA_ROPE), lambda i: (i, 0))],
        out_shape=[jax.ShapeDtypeStruct((rows, MLA_HEADS * (MLA_NOPE + MLA_V)), BF16),
                   jax.ShapeDtypeStruct((rows, 2 * MLA_ROPE), BF16)],
        compiler_params=_params(1), name="mla_kv_proj",
    )(z, p["kv_norm"].reshape(1, -1), p["w_ukv"].astype(BF16), z, jnp.asarray(tab_k))
    return q, kv, kr


def _scores(q, kv_ref, kr_ref):
    k = jnp.concatenate([kv_ref[:, :MLA_NOPE], kr_ref[...]], axis=-1)
    return lax.dot_general(q, k, (((1,), (1,)), ((), ())), preferred_element_type=F32)


def _attn_lat_kernel(q_ref, kvl_ref, krl_ref, kvc_ref, krc_ref, o_ref):
    q = q_ref[...]
    s_l = _scores(q, kvl_ref, krl_ref)
    s_c = _scores(q, kvc_ref, krc_ref)
    m = jnp.maximum(jnp.max(s_l, axis=-1, keepdims=True), jnp.max(s_c, axis=-1, keepdims=True))
    p_l = jnp.exp(s_l - m)
    p_c = jnp.exp(s_c - m)
    l = jnp.sum(p_l, axis=-1, keepdims=True) + jnp.sum(p_c, axis=-1, keepdims=True)
    o = (jnp.dot(p_l.astype(BF16), kvl_ref[:, MLA_NOPE:], preferred_element_type=F32)
         + jnp.dot(p_c.astype(BF16), kvc_ref[:, MLA_NOPE:], preferred_element_type=F32))
    o_ref[...] = (o / l).astype(o_ref.dtype)


def _attn_ctx_kernel(q_ref, kvc_ref, krc_ref, o_ref):
    s = _scores(q_ref[...], kvc_ref, krc_ref)
    m = jnp.max(s, axis=-1, keepdims=True)
    p = jnp.exp(s - m)
    l = jnp.sum(p, axis=-1, keepdims=True)
    o = jnp.dot(p.astype(BF16), kvc_ref[:, MLA_NOPE:], preferred_element_type=F32)
    o_ref[...] = (o / l).astype(o_ref.dtype)


def mla_attention(q, kv, kr, batch, seq, ctx_len, with_ctx_out):
    tq = 512
    nq = seq // tq
    ctx_blk = batch * seq // ctx_len
    hw = MLA_NOPE + MLA_V
    o_lat = pl.pallas_call(
        _attn_lat_kernel,
        grid=(batch, MLA_HEADS, nq),
        in_specs=[pl.BlockSpec((tq, MLA_QK), lambda b, h, i: (b * nq + i, h)),
                  pl.BlockSpec((seq, hw), lambda b, h, i: (b, h)),
                  pl.BlockSpec((seq, 2 * MLA_ROPE), lambda b, h, i: (b, 0)),
                  pl.BlockSpec((ctx_len, hw), lambda b, h, i: (ctx_blk + b, h)),
                  pl.BlockSpec((ctx_len, 2 * MLA_ROPE), lambda b, h, i: (ctx_blk + b, 0))],
        out_specs=pl.BlockSpec((tq, MLA_V), lambda b, h, i: (b * nq + i, h)),
        out_shape=jax.ShapeDtypeStruct((batch * seq, MLA_HEADS * MLA_V), BF16),
        compiler_params=_params(3), name="mla_attention_latent",
    )(q, kv, kr, kv, kr)
    if not with_ctx_out:
        return o_lat
    o_ctx = pl.pallas_call(
        _attn_ctx_kernel,
        grid=(batch, MLA_HEADS),
        in_specs=[pl.BlockSpec((ctx_len, MLA_QK), lambda b, h: (ctx_blk + b, h)),
                  pl.BlockSpec((ctx_len, hw), lambda b, h: (ctx_blk + b, h)),
                  pl.BlockSpec((ctx_len, 2 * MLA_ROPE), lambda b, h: (ctx_blk + b, 0))],
        out_specs=pl.BlockSpec((ctx_len, MLA_V), lambda b, h: (b, h)),
        out_shape=jax.ShapeDtypeStruct((batch * ctx_len, MLA_HEADS * MLA_V), BF16),
        compiler_params=_params(2), name="mla_attention_context",
    )(q, kv, kr)
    return jnp.concatenate([o_lat, o_ctx], axis=0)


def rms_norm(x, g):
    return x * lax.rsqrt(jnp.mean(jnp.square(x), axis=-1, keepdims=True) + RMS_EPS) * g


def l2_normalize(x):
    return x * lax.rsqrt(jnp.sum(x * x, axis=-1, keepdims=True) + RMS_EPS)


def centred_depthwise_conv3(u, w):
    zero = jnp.zeros_like(u[:, :1])
    prev = jnp.concatenate([zero, u[:, :-1]], axis=1)
    nxt = jnp.concatenate([u[:, 1:], zero], axis=1)
    return prev * w[0] + u * w[1] + nxt * w[2]


def chunk_gated_delta(q, k, v, g, beta, s0):
    bsz, heads, length, dk = q.shape
    dv = v.shape[-1]
    n = length // GDN_CHUNK

    def chunked(t):
        return t.reshape(bsz, heads, n, GDN_CHUNK, *t.shape[3:])

    q, k, v, g, beta = chunked(q), chunked(k), chunked(v), chunked(g), chunked(beta)
    cum = jnp.cumsum(g, axis=-1)
    pos = jnp.arange(GDN_CHUNK)
    incl = pos[:, None] >= pos[None, :]
    strict = pos[:, None] > pos[None, :]
    gamma = jnp.exp(jnp.where(incl, cum[..., :, None] - cum[..., None, :], -jnp.inf))
    k_beta = k * beta[..., None]
    a_mat = jnp.where(strict, jnp.einsum("bhncd,bhnsd->bhncs", k_beta, k) * gamma, 0.0)
    rhs = jnp.concatenate([v * beta[..., None], k_beta * jnp.exp(cum)[..., None]], axis=-1)
    sol = lax.linalg.triangular_solve(a_mat + jnp.eye(GDN_CHUNK, dtype=F32), rhs, left_side=True, lower=True)
    u, w = sol[..., :dv], sol[..., dv:]
    qk = jnp.where(incl, jnp.einsum("bhncd,bhnsd->bhncs", q, k) * gamma, 0.0)
    q_dec = q * jnp.exp(cum)[..., None]
    g_last = cum[..., -1]
    k_dec = k * jnp.exp(g_last[..., None] - cum)[..., None]

    def step(state, xs):
        u_i, w_i, qd_i, qk_i, kd_i, gl_i = xs
        v_new = u_i - jnp.einsum("bhck,bhkv->bhcv", w_i, state)
        o_i = jnp.einsum("bhck,bhkv->bhcv", qd_i, state) + jnp.einsum("bhcs,bhsv->bhcv", qk_i, v_new)
        state = state * jnp.exp(gl_i)[..., None, None] + jnp.einsum("bhck,bhcv->bhkv", kd_i, v_new)
        return state, o_i

    xs = tuple(jnp.moveaxis(t, 2, 0) for t in (u, w, q_dec, qk, k_dec, g_last))
    state, o = lax.scan(step, s0, xs)
    o = jnp.moveaxis(o, 0, 2).reshape(bsz, heads, length, dv)
    return o, state


def gdn_project(z, p):
    bsz, length, _ = z.shape
    hk, hv, nh = GDN_HEADS * GDN_DK, GDN_HEADS * GDN_DV, GDN_HEADS
    qkv, gate, a, b = jnp.split(z, [2 * hk + hv, 2 * hk + 2 * hv, 2 * hk + 2 * hv + 2 * nh], axis=-1)
    qkv = jax.nn.silu(centred_depthwise_conv3(qkv, p["conv_w"]))
    q, k, v = jnp.split(qkv, [hk, 2 * hk], axis=-1)
    q = l2_normalize(q.reshape(bsz, length, nh, GDN_DK)).transpose(0, 2, 1, 3) * GDN_DK ** -0.5
    k = l2_normalize(k.reshape(bsz, length, nh, GDN_DK)).transpose(0, 2, 1, 3)
    v = v.reshape(bsz, length, nh, GDN_DV).transpose(0, 2, 1, 3)
    a = a.reshape(bsz, length, 2, nh).transpose(2, 0, 3, 1)
    beta = jax.nn.sigmoid(b.reshape(bsz, length, 2, nh).transpose(2, 0, 3, 1))
    g = -jnp.exp(p["a_log"])[:, None, :, None] * jax.nn.softplus(a + p["dt_bias"][:, None, :, None])
    return q, k, v, g, beta, gate


def gdn_bidirectional(q, k, v, g, beta, s_fwd, s_bwd):
    o_f, s_f = chunk_gated_delta(q, k, v, g[0], beta[0], s_fwd)
    flip = lambda t: jnp.flip(t, axis=2)
    o_b, s_b = chunk_gated_delta(flip(q), flip(k), flip(v), flip(g[1]), flip(beta[1]), s_bwd)
    return o_f + flip(o_b), s_f, s_b


def gdn_gated_output(o, gate, p):
    bsz, heads, length, _ = o.shape
    o = rms_norm(o.transpose(0, 2, 1, 3), p["o_norm"]) * jax.nn.silu(gate.reshape(bsz, length, heads, GDN_DV))
    return o.reshape(bsz * length, heads * GDN_DV)


def gdn_mixer_core(h, p, batch, seq, ctx_len):
    n_in = p["w_in"].shape[1]
    n_pad = -(-n_in // V7X_LANES) * V7X_LANES
    w_in = jnp.pad(p["w_in"], ((0, 0), (0, n_pad - n_in))).astype(BF16)
    z = project(h, w_in, tn=n_pad // 3)[:, :n_in]
    z_lat = z[:batch * seq].reshape(batch, seq, n_in)
    z_ctx = z[batch * seq:].reshape(batch, ctx_len, n_in)
    qc, kc, vc, gc, bc, _ = gdn_project(z_ctx, p)
    s0 = jnp.zeros((batch, GDN_HEADS, GDN_DK, GDN_DV), F32)
    _, s_fwd, s_bwd = gdn_bidirectional(qc, kc, vc, gc, bc, s0, s0)
    ql, kl, vl, gl, bl, zl = gdn_project(z_lat, p)
    o_lat, _, _ = gdn_bidirectional(ql, kl, vl, gl, bl, s_fwd, s_bwd)
    return gdn_gated_output(o_lat, zl, p)


def kernel(x, c, ctx, c_ctx, mod_w, mod_b, ln_g, ln_b, ffn_w_in, ffn_w_out, hy_w_in, hy_b_in, hy_conv_w, hy_conv_b, hy_fw1, hy_fb1, hy_fw2, hy_fb2, hy_fw3, hy_fb3, hy_fw4, hy_freq, hy_skip, hy_w_out, hy_b_out, mla_w_in, mla_q_norm, mla_kv_norm, mla_w_uq, mla_w_ukv, mla_w_out, gdn_w_in, gdn_conv_w, gdn_a_log, gdn_dt_bias, gdn_o_norm, gdn_w_out):
    batch, seq, d = x.shape
    ctx_len = ctx.shape[1]
    t_lat, t_ctx = batch * seq, batch * ctx_len
    t_all = t_lat + t_ctx
    assert t_lat % ROW_TILE == 0 and t_ctx % ROW_TILE == 0 and seq % ROW_TILE == 0

    n_mod = -(-(batch + 1) // 8) * 8
    cc = jnp.concatenate([c, c_ctx[None], jnp.zeros((n_mod - batch - 1, d), F32)], axis=0)
    mods = adaln_all(cc, mod_w, mod_b).reshape(DEPTH, n_mod, 6, 1, d).transpose(0, 2, 1, 3, 4)
    zero_bias = jnp.zeros((d,), F32)

    s = jnp.concatenate([x.reshape(t_lat, d), ctx.reshape(t_ctx, d)], axis=0)
    h = modulate(s, mods[0, 1], mods[0, 0], seq, batch)
    rows = t_all
    for i in range(DEPTH):
        kind, j = i % N_MIXERS, i // N_MIXERS
        ctx_out = any(l % N_MIXERS != MIXER_HYENA for l in range(i + 1, DEPTH))
        rows_out = t_all if ctx_out else t_lat
        if kind == MIXER_HYENA:
            p = {"w_in": hy_w_in[j].astype(BF16), "b_in": hy_b_in[j], "conv_w": hy_conv_w[j], "conv_b": hy_conv_b[j],
                 "fw1": hy_fw1[j], "fb1": hy_fb1[j], "fw2": hy_fw2[j], "fb2": hy_fb2[j],
                 "fw3": hy_fw3[j], "fb3": hy_fb3[j], "fw4": hy_fw4[j], "freq": hy_freq[j], "skip": hy_skip[j]}
            y = hyena_mixer_core(h, p, seq, batch, 0)
            if ctx_out:
                y = jnp.concatenate([y, hyena_mixer_core(h, p, ctx_len, batch, t_lat)], axis=0)
            w_out, b_out = hy_w_out[j], hy_b_out[j]
        elif kind == MIXER_MLA:
            p = {"w_in": mla_w_in[j], "q_norm": mla_q_norm[j], "kv_norm": mla_kv_norm[j],
                 "w_uq": mla_w_uq[j], "w_ukv": mla_w_ukv[j]}
            q, kv, kr = mla_qkv(h, p, t_all, seq, batch)
            y = mla_attention(q, kv, kr, batch, seq, ctx_len, ctx_out)
            w_out, b_out = mla_w_out[j], zero_bias
        else:
            assert not ctx_out
            p = {"w_in": gdn_w_in[j], "conv_w": gdn_conv_w[j], "a_log": gdn_a_log[j],
                 "dt_bias": gdn_dt_bias[j], "o_norm": gdn_o_norm[j]}
            y = gdn_mixer_core(h, p, batch, seq, ctx_len)
            w_out, b_out = gdn_w_out[j], zero_bias
        s, h = out_ln(y, w_out.astype(BF16), b_out, s, mods[i, 2], ln_g[i, 0], ln_b[i, 0], mods[i, 4], mods[i, 3],
                      rows_out, seq, batch, ROW_TILE)
        a = ffn_in(h, ffn_w_in[i].astype(BF16), rows_out)
        nxt = min(i + 1, DEPTH - 1)
        s, h = out_ln(a, ffn_w_out[i].astype(BF16), zero_bias, s, mods[i, 5], ln_g[i, 1], ln_b[i, 1],
                      mods[nxt, 1], mods[nxt, 0], rows_out, seq, batch, ROW_TILE // 2)
        rows = rows_out
    del rows
    return s[:t_lat].reshape(batch, seq, d)
```

```python
import functools
import math

import numpy as np

import jax
import jax.numpy as jnp
from jax import lax
from jax.experimental import pallas as pl
from jax.experimental.pallas import tpu as pltpu

F32 = jnp.float32
BF16 = jnp.bfloat16
HIGHEST = lax.Precision.HIGHEST

D_MODEL = 1024
DEPTH = 4
GRID_W = 64
N_MIXERS = 3
MIXER_HYENA = 0
MIXER_MLA = 1
DEEPNORM_ALPHA = (2 * DEPTH) ** 0.25
LN_EPS = 1e-5
RMS_EPS = 1e-6
D_FF = -(-8 * D_MODEL // (3 * 256)) * 256
HYENA_EMB = 33
HYENA_FILTER_HIDDEN = 64
HYENA_DECAY_TARGET = 1e-2
HYENA_FAST_DECAY_PCT = 0.3
HYENA_SLOW_DECAY_PCT = 1.5
MLA_HEADS = 8
MLA_Q_LORA = 384
MLA_KV_LORA = 256
MLA_NOPE = 128
MLA_ROPE = 64
MLA_V = 128
MLA_SCALE = (MLA_NOPE + MLA_ROPE) ** -0.5
MLA_QK = 256
ROPE_THETA = 10000.0
GDN_HEADS = 8
GDN_DK = 128
GDN_DV = 128
GDN_CHUNK = 64

V7X_LANES = 128
V7X_MXU_DIM = 256
V7X_VMEM_LIMIT_BYTES = 56 * 1024 * 1024
ROW_TILE = 1024


def _params(n_axes):
    return pltpu.CompilerParams(dimension_semantics=("arbitrary",) * n_axes,
                                vmem_limit_bytes=V7X_VMEM_LIMIT_BYTES)


def _mod_row_map(tile, seq, batch):
    return lambda i: (jnp.minimum(i * tile // seq, batch), 0, 0)


def _adaln_kernel(c_ref, w_ref, b_ref, o_ref):
    x = c_ref[...]
    x = (x * jax.nn.sigmoid(x)).astype(BF16)
    o_ref[0] = jnp.dot(x, w_ref[0].astype(BF16), preferred_element_type=F32) + b_ref[0]


def adaln_all(cc, mod_w, mod_b):
    r, d = cc.shape
    depth, _, n = mod_w.shape
    tn = 1536
    return pl.pallas_call(
        _adaln_kernel,
        grid=(depth, n // tn),
        in_specs=[pl.BlockSpec((r, d), lambda l, j: (0, 0)),
                  pl.BlockSpec((1, d, tn), lambda l, j: (l, 0, j)),
                  pl.BlockSpec((1, 1, tn), lambda l, j: (l, 0, j))],
        out_specs=pl.BlockSpec((1, r, tn), lambda l, j: (l, 0, j)),
        out_shape=jax.ShapeDtypeStruct((depth, r, n), F32),
        compiler_params=_params(2), name="adaln_table",
    )(cc, mod_w, mod_b.reshape(depth, 1, n))


def _modulate_kernel(s_ref, scale_ref, shift_ref, h_ref):
    h_ref[...] = (s_ref[...] * (1.0 + scale_ref[0]) + shift_ref[0]).astype(h_ref.dtype)


def modulate(s, scale, shift, seq, batch):
    rows, d = s.shape
    tm = ROW_TILE
    mod_spec = pl.BlockSpec((1, 1, d), _mod_row_map(tm, seq, batch))
    return pl.pallas_call(
        _modulate_kernel,
        grid=(rows // tm,),
        in_specs=[pl.BlockSpec((tm, d), lambda i: (i, 0)), mod_spec, mod_spec],
        out_specs=pl.BlockSpec((tm, d), lambda i: (i, 0)),
        out_shape=jax.ShapeDtypeStruct((rows, d), BF16),
        compiler_params=_params(1), name="modulate",
    )(s, scale, shift)


def _proj_kernel(x_ref, w_ref, o_ref):
    o_ref[...] = jnp.dot(x_ref[...], w_ref[...], preferred_element_type=F32).astype(o_ref.dtype)


def project(x, w, out_dtype=F32, tm=ROW_TILE, tn=None):
    m, k = x.shape
    n = w.shape[1]
    tn = n if tn is None else tn
    return pl.pallas_call(
        _proj_kernel,
        grid=(m // tm, n // tn),
        in_specs=[pl.BlockSpec((tm, k), lambda i, j: (i, 0)),
                  pl.BlockSpec((k, tn), lambda i, j: (0, j))],
        out_specs=pl.BlockSpec((tm, tn), lambda i, j: (i, j)),
        out_shape=jax.ShapeDtypeStruct((m, n), out_dtype),
        compiler_params=_params(2), name="projection",
    )(x, w)


def _out_ln_kernel(x_ref, w_ref, b_ref, s_ref, gate_ref, lng_ref, lnb_ref, scale_ref, shift_ref,
                   s_out_ref, h_out_ref):
    y = jnp.dot(x_ref[...].astype(BF16), w_ref[...], preferred_element_type=F32) + b_ref[...]
    z = DEEPNORM_ALPHA * s_ref[...] + gate_ref[0] * y
    mu = jnp.mean(z, axis=-1, keepdims=True)
    zc = z - mu
    var = jnp.mean(zc * zc, axis=-1, keepdims=True)
    sn = zc * lax.rsqrt(var + LN_EPS) * lng_ref[...] + lnb_ref[...]
    s_out_ref[...] = sn
    h_out_ref[...] = (sn * (1.0 + scale_ref[0]) + shift_ref[0]).astype(h_out_ref.dtype)


def out_ln(x, w, b, s, gate, ln_g, ln_b, scale, shift, rows, seq, batch, tm):
    k = x.shape[1]
    d = w.shape[1]
    mod_spec = pl.BlockSpec((1, 1, d), _mod_row_map(tm, seq, batch))
    vec_spec = pl.BlockSpec((1, d), lambda i: (0, 0))
    row_spec = pl.BlockSpec((tm, d), lambda i: (i, 0))
    return pl.pallas_call(
        _out_ln_kernel,
        grid=(rows // tm,),
        in_specs=[pl.BlockSpec((tm, k), lambda i: (i, 0)),
                  pl.BlockSpec((k, d), lambda i: (0, 0)),
                  vec_spec, row_spec, mod_spec, vec_spec, vec_spec, mod_spec, mod_spec],
        out_specs=[row_spec, row_spec],
        out_shape=[jax.ShapeDtypeStruct((rows, d), F32), jax.ShapeDtypeStruct((rows, d), BF16)],
        compiler_params=_params(1), name="out_proj_ln",
    )(x, w, b.reshape(1, d), s, gate, ln_g.reshape(1, d), ln_b.reshape(1, d), scale, shift)


def _ffn_in_kernel(x_ref, w_ref, a_ref):
    x = x_ref[...]
    dff = a_ref.shape[1]
    for c in range(dff // V7X_MXU_DIM):
        lo = c * V7X_MXU_DIM
        g = jnp.dot(x, w_ref[:, lo:lo + V7X_MXU_DIM], preferred_element_type=F32)
        u = jnp.dot(x, w_ref[:, dff + lo:dff + lo + V7X_MXU_DIM], preferred_element_type=F32)
        a_ref[:, lo:lo + V7X_MXU_DIM] = (g * jax.nn.sigmoid(g) * u).astype(a_ref.dtype)


def ffn_in(h, w_in, rows):
    k = h.shape[1]
    dff = w_in.shape[1] // 2
    tm = ROW_TILE
    return pl.pallas_call(
        _ffn_in_kernel,
        grid=(rows // tm,),
        in_specs=[pl.BlockSpec((tm, k), lambda i: (i, 0)),
                  pl.BlockSpec((k, 2 * dff), lambda i: (0, 0))],
        out_specs=pl.BlockSpec((tm, dff), lambda i: (i, 0)),
        out_shape=jax.ShapeDtypeStruct((rows, dff), BF16),
        compiler_params=_params(1), name="ffn_swiglu_in",
    )(h, w_in)


def _hyena_in_kernel(x_ref, w0_ref, w1_ref, w2_ref, b0_ref, b1_ref, b2_ref, cw0_ref, cw1_ref, cw2_ref,
                     cb0_ref, cb1_ref, cb2_ref, x0_ref, u_ref):
    x = x_ref[...]
    length = x.shape[0]
    row = lax.broadcasted_iota(jnp.int32, (length, 1), 0)

    def branch(w_ref, b_ref, cw_ref, cb_ref):
        z = jnp.dot(x, w_ref[...], preferred_element_type=F32) + b_ref[...]
        prev = jnp.where(row == 0, 0.0, pltpu.roll(z, 1, 0))
        nxt = jnp.where(row == length - 1, 0.0, pltpu.roll(z, length - 1, 0))
        cw = cw_ref[...]
        return prev * cw[0:1] + z * cw[1:2] + nxt * cw[2:3] + cb_ref[...]

    x0 = branch(w0_ref, b0_ref, cw0_ref, cb0_ref)
    x1 = branch(w1_ref, b1_ref, cw1_ref, cb1_ref)
    v = branch(w2_ref, b2_ref, cw2_ref, cb2_ref)
    x0_ref[...] = x0.astype(x0_ref.dtype)
    u_ref[...] = (x1 * v).astype(u_ref.dtype)


def hyena_in(h, w_in, b_in, conv_w, conv_b, length, n_seq, row_offset):
    k = h.shape[1]
    d = w_in.shape[1] // 3
    tn = 256
    nb = d // tn
    off = row_offset // length
    w_specs = [pl.BlockSpec((k, tn), lambda b, j, q=q: (0, j + q * nb)) for q in range(3)]
    v_specs = [pl.BlockSpec((1, tn), lambda b, j, q=q: (0, j + q * nb)) for q in range(3)]
    cw_specs = [pl.BlockSpec((3, tn), lambda b, j, q=q: (0, j + q * nb)) for q in range(3)]
    out_spec = pl.BlockSpec((length, tn), lambda b, j: (b, j))
    b2, cb2 = b_in.reshape(1, 3 * d), conv_b.reshape(1, 3 * d)
    return pl.pallas_call(
        _hyena_in_kernel,
        grid=(n_seq, nb),
        in_specs=[pl.BlockSpec((length, k), lambda b, j: (b + off, 0))] + w_specs + v_specs + cw_specs + v_specs,
        out_specs=[out_spec, out_spec],
        out_shape=[jax.ShapeDtypeStruct((n_seq * length, d), BF16)] * 2,
        compiler_params=_params(2), name="hyena_in_proj_conv3",
    )(h, w_in, w_in, w_in, b2, b2, b2, conv_w, conv_w, conv_w, cb2, cb2, cb2)


def _hyena_filter_kernel(feat_ref, w1_ref, b1_ref, w2_ref, b2_ref, w3_ref, b3_ref, w4_ref, freq_ref, win_ref, k_ref):
    freq = freq_ref[...]
    z = jnp.sin(freq * (jnp.dot(feat_ref[...], w1_ref[...], precision=HIGHEST, preferred_element_type=F32) + b1_ref[...]))
    z = jnp.sin(freq * (jnp.dot(z, w2_ref[...], precision=HIGHEST, preferred_element_type=F32) + b2_ref[...]))
    z = jnp.sin(freq * (jnp.dot(z, w3_ref[...], precision=HIGHEST, preferred_element_type=F32) + b3_ref[...]))
    taps = jnp.dot(z, w4_ref[...], precision=HIGHEST, preferred_element_type=F32)
    d = win_ref.shape[1]
    win = win_ref[...]
    k_fwd = taps[:, :d] * win
    k_bwd = taps[:, d:] * win
    tile = k_fwd.shape[0]
    first = (lax.broadcasted_iota(jnp.int32, (tile, 1), 0) + pl.program_id(0) * tile) == 0
    k_ref[:, :d] = jnp.where(first, k_fwd + k_bwd, k_fwd).astype(k_ref.dtype)
    k_ref[:, d:] = jnp.where(first, 0.0, k_bwd).astype(k_ref.dtype)


def _hyena_features(length):
    t01 = np.linspace(0.0, 1.0, length, dtype=np.float32)[:, None]
    bands = (HYENA_EMB - 1) // 2
    w = (np.float32(2.0 * math.pi / length) * np.arange(length, dtype=np.float32)).astype(np.float32)
    f = np.linspace(1e-4, bands - 1, bands, dtype=np.float32)
    ang = (w[:, None] * f[None, :]).astype(np.float32)
    feat = np.concatenate([t01, np.cos(ang), -np.sin(ang)], axis=-1).astype(np.float32)
    return np.pad(feat, ((0, 0), (0, V7X_LANES - HYENA_EMB)))


def _hyena_window(length):
    max_decay = math.log(HYENA_DECAY_TARGET) / HYENA_FAST_DECAY_PCT
    min_decay = math.log(HYENA_DECAY_TARGET) / HYENA_SLOW_DECAY_PCT
    deltas = np.abs(np.linspace(min_decay, max_decay, D_MODEL, dtype=np.float32))
    t = np.linspace(0.0, 1.0, length, dtype=np.float32)
    return np.exp(-t[:, None] * deltas[None, :]).astype(np.float32)


def hyena_filter_taps(length, p):
    hid = V7X_LANES
    pad_h = hid - HYENA_FILTER_HIDDEN
    padc = lambda a: jnp.pad(a, ((0, 0), (0, pad_h)))
    w1 = jnp.pad(p["fw1"], ((0, V7X_LANES - HYENA_EMB), (0, pad_h)))
    w2 = jnp.pad(p["fw2"], ((0, pad_h), (0, pad_h)))
    w3 = jnp.pad(p["fw3"], ((0, pad_h), (0, pad_h)))
    w4 = jnp.pad(p["fw4"], ((0, pad_h), (0, 0)))
    vec = lambda a: padc(a.reshape(1, -1))
    args = (jnp.asarray(_hyena_features(length)), w1, vec(p["fb1"]), w2, vec(p["fb2"]), w3, vec(p["fb3"]), w4,
            vec(p["freq"]), jnp.asarray(_hyena_window(length)))
    tl = min(256, length)
    full = lambda a: pl.BlockSpec(a.shape, lambda i: (0, 0))
    return pl.pallas_call(
        _hyena_filter_kernel,
        grid=(length // tl,),
        in_specs=[pl.BlockSpec((tl, hid), lambda i: (i, 0))] + [full(a) for a in args[1:9]]
                 + [pl.BlockSpec((tl, D_MODEL), lambda i: (i, 0))],
        out_specs=pl.BlockSpec((tl, 2 * D_MODEL), lambda i: (i, 0)),
        out_shape=jax.ShapeDtypeStruct((length, 2 * D_MODEL), BF16),
        compiler_params=_params(1), name="hyena_filter",
    )(*args)


@functools.lru_cache(maxsize=None)
def _negacyclic_dft(length):
    n = 2 * length
    k = np.arange(length, dtype=np.int64)
    turns = ((2 * k[:, None] + 1) * k[None, :]) % (2 * n)
    ang = turns.astype(np.float64) * (2.0 * math.pi / (2 * n))
    c, s = np.cos(ang), np.sin(ang)
    inv = 2.0 / n
    return c, s, (c.T * inv), (s.T * inv)


def _dft_spec_kernel(c_ref, s_ref, u_ref, p_ref):
    u = u_ref[...]
    p_ref[0] = jnp.dot(c_ref[...], u, preferred_element_type=F32)
    p_ref[1] = jnp.dot(s_ref[...], u, preferred_element_type=F32)


def _dft_fwd_kernel(c_ref, s_ref, u_ref, hr_ref, hi_ref, y_ref):
    u = u_ref[...]
    ur = jnp.dot(c_ref[...], u, preferred_element_type=F32)
    us = jnp.dot(s_ref[...], u, preferred_element_type=F32)
    hr, hi = hr_ref[...], hi_ref[...]
    y_ref[0, 0] = (ur * hr + us * hi).astype(y_ref.dtype)
    y_ref[0, 1] = (us * hr - ur * hi).astype(y_ref.dtype)


def _dft_inv_kernel(ct_ref, st_ref, y_ref, u_ref, x0_ref, skip_ref, o_ref):
    y = (jnp.dot(ct_ref[...], y_ref[0, 0], preferred_element_type=F32)
         + jnp.dot(st_ref[...], y_ref[0, 1], preferred_element_type=F32))
    u = u_ref[...].astype(F32)
    o_ref[...] = ((y + u * skip_ref[...]) * x0_ref[...].astype(F32)).astype(o_ref.dtype)


def hyena_long_conv(x0, u, taps, skip, length, n_seq):
    d = u.shape[1]
    c, s, ct, st = (jnp.asarray(a, dtype=BF16) for a in _negacyclic_dft(length))
    tf = min(512, length)
    nf = length // tf
    spec = pl.pallas_call(
        _dft_spec_kernel,
        grid=(nf,),
        in_specs=[pl.BlockSpec((tf, length), lambda i: (i, 0)),
                  pl.BlockSpec((tf, length), lambda i: (i, 0)),
                  pl.BlockSpec((length, 2 * d), lambda i: (0, 0))],
        out_specs=pl.BlockSpec((2, tf, 2 * d), lambda i: (0, i, 0)),
        out_shape=jax.ShapeDtypeStruct((2, length, 2 * d), F32),
        compiler_params=_params(1), name="hyena_filter_spectrum",
    )(c, s, taps)
    h_re = spec[0, :, :d] + spec[0, :, d:]
    h_im = spec[1, :, d:] - spec[1, :, :d]
    y = pl.pallas_call(
        _dft_fwd_kernel,
        grid=(n_seq, nf),
        in_specs=[pl.BlockSpec((tf, length), lambda b, i: (i, 0)),
                  pl.BlockSpec((tf, length), lambda b, i: (i, 0)),
                  pl.BlockSpec((length, d), lambda b, i: (b, 0)),
                  pl.BlockSpec((tf, d), lambda b, i: (i, 0)),
                  pl.BlockSpec((tf, d), lambda b, i: (i, 0))],
        out_specs=pl.BlockSpec((1, 2, tf, d), lambda b, i: (b, 0, i, 0)),
        out_shape=jax.ShapeDtypeStruct((n_seq, 2, length, d), BF16),
        compiler_params=_params(2), name="hyena_dft_forward",
    )(c, s, u, h_re, h_im)
    return pl.pallas_call(
        _dft_inv_kernel,
        grid=(n_seq, nf),
        in_specs=[pl.BlockSpec((tf, length), lambda b, i: (i, 0)),
                  pl.BlockSpec((tf, length), lambda b, i: (i, 0)),
                  pl.BlockSpec((1, 2, length, d), lambda b, i: (b, 0, 0, 0)),
                  pl.BlockSpec((tf, d), lambda b, i: (b * nf + i, 0)),
                  pl.BlockSpec((tf, d), lambda b, i: (b * nf + i, 0)),
                  pl.BlockSpec((1, d), lambda b, i: (0, 0))],
        out_specs=pl.BlockSpec((tf, d), lambda b, i: (b * nf + i, 0)),
        out_shape=jax.ShapeDtypeStruct((n_seq * length, d), BF16),
        compiler_params=_params(2), name="hyena_dft_inverse",
    )(ct, st, y, u, x0, skip.reshape(1, d))


def hyena_mixer_core(h, p, length, n_seq, row_offset):
    x0, u = hyena_in(h, p["w_in"], p["b_in"], p["conv_w"], p["conv_b"], length, n_seq, row_offset)
    taps = hyena_filter_taps(length, p)
    return hyena_long_conv(x0, u, taps, p["skip"], length, n_seq)


def _rope_swap_columns():
    half, quarter = MLA_ROPE // 2, MLA_ROPE // 4
    r = np.arange(MLA_ROPE)
    within = r % half
    src = np.where(within < quarter, r + quarter, r - quarter)
    sign = np.where(within < quarter, -1.0, 1.0).astype(np.float32)
    return src, sign


@functools.lru_cache(maxsize=None)
def _rope_tables(seq, ctx_rows):
    half, quarter = MLA_ROPE // 2, MLA_ROPE // 4
    inv_freq = (np.float32(ROPE_THETA) ** (-np.arange(quarter, dtype=np.float32) / quarter)).astype(np.float32)
    t = np.arange(seq)
    row, col = (t // GRID_W).astype(np.float32), (t % GRID_W).astype(np.float32)
    r = np.arange(MLA_ROPE)
    pos = np.where((r // half)[None, :] == 0, row[:, None], col[:, None]).astype(np.float32)
    ang = (pos * inv_freq[r % quarter][None, :]).astype(np.float32).astype(np.float64)
    cos = np.concatenate([np.cos(ang), np.ones((ctx_rows, MLA_ROPE))], axis=0)
    sin = np.concatenate([np.sin(ang), np.zeros((ctx_rows, MLA_ROPE))], axis=0)
    ones = np.ones((seq + ctx_rows, MLA_NOPE))
    tab_q = (np.concatenate([ones, cos, sin], axis=1) * MLA_SCALE).astype(np.float32)
    tab_k = np.concatenate([cos, sin], axis=1).astype(np.float32)
    return tab_q, tab_k


def _rms_in_kernel(x_ref, g_ref):
    x = x_ref[...]
    return (x * lax.rsqrt(jnp.mean(x * x, axis=-1, keepdims=True) + RMS_EPS) * g_ref[...]).astype(BF16)


def _mla_q_kernel(x_ref, g_ref, w_ref, tab_ref, q_ref):
    acc = jnp.dot(_rms_in_kernel(x_ref, g_ref), w_ref[...], preferred_element_type=F32)
    tab = tab_ref[...]
    for h in range(MLA_HEADS):
        q_ref[:, h * MLA_QK:(h + 1) * MLA_QK] = (acc[:, h * MLA_QK:(h + 1) * MLA_QK] * tab).astype(q_ref.dtype)


def _mla_kv_kernel(x_ref, g_ref, w_ref, rope_ref, tab_ref, k_ref, v_ref):
    acc = jnp.dot(_rms_in_kernel(x_ref, g_ref), w_ref[...], preferred_element_type=F32)
    prod = rope_ref[...] * tab_ref[...]
    k_rot = (prod + pltpu.roll(prod, MLA_ROPE, 1)).astype(k_ref.dtype)
    hw = MLA_NOPE + MLA_V
    for h in range(MLA_HEADS):
        k_ref[:, h * MLA_QK:h * MLA_QK + MLA_NOPE] = acc[:, h * hw:h * hw + MLA_NOPE].astype(k_ref.dtype)
        k_ref[:, h * MLA_QK + MLA_NOPE:(h + 1) * MLA_QK] = k_rot
        v_ref[:, h * MLA_V:(h + 1) * MLA_V] = acc[:, h * hw + MLA_NOPE:(h + 1) * hw].astype(v_ref.dtype)


def mla_qkv(h, p, rows, seq, batch):
    tm = ROW_TILE
    src, sign = _rope_swap_columns()
    w_in = p["w_in"]
    o_kv, o_rope = MLA_Q_LORA, MLA_Q_LORA + MLA_KV_LORA
    w_rope = w_in[:, o_rope:]
    w_cat = jnp.concatenate([w_in[:, o_kv:o_rope], w_rope, w_rope[:, src] * sign, w_in[:, :o_kv]], axis=1).astype(BF16)
    z = project(h[:rows], w_cat)
    w_uq = p["w_uq"].reshape(MLA_Q_LORA, MLA_HEADS, MLA_NOPE + MLA_ROPE)
    w_uq_rope = w_uq[:, :, MLA_NOPE:]
    w_uq = jnp.concatenate([w_uq, w_uq_rope[:, :, src] * sign], axis=2).reshape(MLA_Q_LORA, MLA_HEADS * MLA_QK).astype(BF16)
    tab_q, tab_k = _rope_tables(seq, tm)
    lat_tiles = batch * seq // tm
    tab_map = lambda i: (jnp.where(i < lat_tiles, i % (seq // tm), seq // tm), 0)
    n_tiles = rows // tm
    q = pl.pallas_call(
        _mla_q_kernel,
        grid=(n_tiles,),
        in_specs=[pl.BlockSpec((tm, MLA_Q_LORA), lambda i: (i, 1)),
                  pl.BlockSpec((1, MLA_Q_LORA), lambda i: (0, 0)),
                  pl.BlockSpec((MLA_Q_LORA, MLA_HEADS * MLA_QK), lambda i: (0, 0)),
                  pl.BlockSpec((tm, MLA_QK), tab_map)],
        out_specs=pl.BlockSpec((tm, MLA_HEADS * MLA_QK), lambda i: (i, 0)),
        out_shape=jax.ShapeDtypeStruct((rows, MLA_HEADS * MLA_QK), BF16),
        compiler_params=_params(1), name="mla_q_proj",
    )(z, p["q_norm"].reshape(1, -1), w_uq, jnp.asarray(tab_q))
    k, v = pl.pallas_call(
        _mla_kv_kernel,
        grid=(n_tiles,),
        in_specs=[pl.BlockSpec((tm, MLA_KV_LORA), lambda i: (i, 0)),
                  pl.BlockSpec((1, MLA_KV_LORA), lambda i: (0, 0)),
                  pl.BlockSpec((MLA_KV_LORA, MLA_HEADS * (MLA_NOPE + MLA_V)), lambda i: (0, 0)),
                  pl.BlockSpec((tm, 2 * MLA_ROPE), lambda i: (i, 2)),
                  pl.BlockSpec((tm, 2 * MLA_ROPE), tab_map)],
        out_specs=[pl.BlockSpec((tm, MLA_HEADS * MLA_QK), lambda i: (i, 0)),
                   pl.BlockSpec((tm, MLA_HEADS * MLA_V), lambda i: (i, 0))],
        out_shape=[jax.ShapeDtypeStruct((rows, MLA_HEADS * MLA_QK), BF16),
                   jax.ShapeDtypeStruct((rows, MLA_HEADS * MLA_V), BF16)],
        compiler_params=_params(1), name="mla_kv_proj",
    )(z, p["kv_norm"].reshape(1, -1), p["w_ukv"].astype(BF16), z, jnp.asarray(tab_k))
    return q, k, v


ATTN_SUB = 128


def _dot_nt(a, b):
    return lax.dot_general(a, b, (((1,), (1,)), ((), ())), preferred_element_type=F32)


def _attention_body(q_ref, key_refs, val_refs, o_ref):
    n_sub = q_ref.shape[0] // ATTN_SUB
    rows = lambda i: slice(i * ATTN_SUB, (i + 1) * ATTN_SUB)
    scores = lambda i: [_dot_nt(q_ref[rows(i), :], k_ref[...]) for k_ref in key_refs]
    pending = scores(0)
    for i in range(n_sub):
        s = pending
        if i + 1 < n_sub:
            pending = scores(i + 1)
        m = functools.reduce(jnp.maximum, [jnp.max(x, axis=-1, keepdims=True) for x in s])
        p = [jnp.exp(x - m) for x in s]
        l = sum(jnp.sum(x, axis=-1, keepdims=True) for x in p)
        o = sum(jnp.dot(x.astype(BF16), v_ref[...], preferred_element_type=F32) for x, v_ref in zip(p, val_refs))
        o_ref[rows(i), :] = (o / l).astype(o_ref.dtype)


def _attn_lat_kernel(q_ref, kl_ref, vl_ref, kc_ref, vc_ref, o_ref):
    _attention_body(q_ref, (kl_ref, kc_ref), (vl_ref, vc_ref), o_ref)


def _attn_ctx_kernel(q_ref, kc_ref, vc_ref, o_ref):
    _attention_body(q_ref, (kc_ref,), (vc_ref,), o_ref)


def mla_attention(q, k, v, batch, seq, ctx_len, with_ctx_out):
    tq = 512
    nq = seq // tq
    ctx_blk = batch * seq // ctx_len
    o_lat = pl.pallas_call(
        _attn_lat_kernel,
        grid=(batch, MLA_HEADS, nq),
        in_specs=[pl.BlockSpec((tq, MLA_QK), lambda b, h, i: (b * nq + i, h)),
                  pl.BlockSpec((seq, MLA_QK), lambda b, h, i: (b, h)),
                  pl.BlockSpec((seq, MLA_V), lambda b, h, i: (b, h)),
                  pl.BlockSpec((ctx_len, MLA_QK), lambda b, h, i: (ctx_blk + b, h)),
                  pl.BlockSpec((ctx_len, MLA_V), lambda b, h, i: (ctx_blk + b, h))],
        out_specs=pl.BlockSpec((tq, MLA_V), lambda b, h, i: (b * nq + i, h)),
        out_shape=jax.ShapeDtypeStruct((batch * seq, MLA_HEADS * MLA_V), BF16),
        compiler_params=_params(3), name="mla_attention_latent",
    )(q, k, v, k, v)
    if not with_ctx_out:
        return o_lat
    o_ctx = pl.pallas_call(
        _attn_ctx_kernel,
        grid=(batch, MLA_HEADS),
        in_specs=[pl.BlockSpec((ctx_len, MLA_QK), lambda b, h: (ctx_blk + b, h)),
                  pl.BlockSpec((ctx_len, MLA_QK), lambda b, h: (ctx_blk + b, h)),
                  pl.BlockSpec((ctx_len, MLA_V), lambda b, h: (ctx_blk + b, h))],
        out_specs=pl.BlockSpec((ctx_len, MLA_V), lambda b, h: (b, h)),
        out_shape=jax.ShapeDtypeStruct((batch * ctx_len, MLA_HEADS * MLA_V), BF16),
        compiler_params=_params(2), name="mla_attention_context",
    )(q, k, v)
    return jnp.concatenate([o_lat, o_ctx], axis=0)


GDN_QKV = 2 * GDN_HEADS * GDN_DK + GDN_HEADS * GDN_DV
GDN_SCALARS = 2 * GDN_HEADS
GDN_COLS = 16
LOG2_CHUNK = int(math.log2(GDN_CHUNK))
NEG_BIG = -1e30
GDN_GROUP = 128
GDN_GROUPS_PER_STEP = 4


def _gdn_in_kernel(x_ref, w_ref, cw_ref, o_ref):
    x = x_ref[...]
    length = x.shape[0]
    row = lax.broadcasted_iota(jnp.int32, (length, 1), 0)
    z = jnp.dot(x, w_ref[...], preferred_element_type=F32)
    prev = jnp.where(row == 0, 0.0, pltpu.roll(z, 1, 0))
    nxt = jnp.where(row == length - 1, 0.0, pltpu.roll(z, length - 1, 0))
    cw = cw_ref[...]
    z = prev * cw[0:1] + z * cw[1:2] + nxt * cw[2:3]
    z = z * jax.nn.sigmoid(z)
    j = pl.program_id(1)
    blocks_per_kind = GDN_HEADS * GDN_DK // z.shape[1]
    q_scale = jnp.where(j < blocks_per_kind, GDN_DK ** -0.5, 1.0)
    for hh in range(z.shape[1] // GDN_DK):
        zh = z[:, hh * GDN_DK:(hh + 1) * GDN_DK]
        zn = zh * lax.rsqrt(jnp.sum(zh * zh, axis=-1, keepdims=True) + RMS_EPS) * q_scale
        o_ref[:, hh * GDN_DK:(hh + 1) * GDN_DK] = jnp.where(j < 2 * blocks_per_kind, zn, zh).astype(o_ref.dtype)


def gdn_in(h, w_qkv, conv_w, length, n_seq, row_offset):
    k = h.shape[1]
    n = w_qkv.shape[1]
    tn = 256
    off = row_offset // length
    return pl.pallas_call(
        _gdn_in_kernel,
        grid=(n_seq, n // tn),
        in_specs=[pl.BlockSpec((length, k), lambda b, j: (b + off, 0)),
                  pl.BlockSpec((k, tn), lambda b, j: (0, j)),
                  pl.BlockSpec((3, tn), lambda b, j: (0, j))],
        out_specs=pl.BlockSpec((length, tn), lambda b, j: (b, j)),
        out_shape=jax.ShapeDtypeStruct((n_seq * length, n), BF16),
        compiler_params=_params(2), name="gdn_in_proj_conv3",
    )(h, w_qkv, conv_w)


def _gdn_gates_kernel(ab_ref, alog_ref, dtb_ref, beta_ref, ecum_ref, edec_ref, cum_ref, egl_ref):
    a = ab_ref[:, :V7X_LANES]
    b = ab_ref[:, V7X_LANES:]
    x = a + dtb_ref[...]
    softplus = jnp.maximum(x, 0.0) + jnp.log(1.0 + jnp.exp(-jnp.abs(x)))
    g = -jnp.exp(alog_ref[...]) * softplus
    tm = g.shape[0]
    r = lax.broadcasted_iota(jnp.int32, (tm, tm), 0)
    c = lax.broadcasted_iota(jnp.int32, (tm, tm), 1)
    same = (r >> LOG2_CHUNK) == (c >> LOG2_CHUNK)
    prefix = jnp.where(same & (c <= r), 1.0, 0.0)
    suffix = jnp.where(same & (c >= r), 1.0, 0.0)
    total = jnp.where(same, 1.0, 0.0)
    dot = lambda m: jnp.dot(m, g, precision=HIGHEST, preferred_element_type=F32)
    lane = lax.broadcasted_iota(jnp.int32, g.shape, 1)
    cum = jnp.where(lane < GDN_HEADS, dot(prefix), dot(suffix))
    g_last = dot(total)
    beta_ref[...] = jax.nn.sigmoid(b)
    ecum_ref[...] = jnp.exp(cum)
    edec_ref[...] = jnp.exp(g_last - cum)
    cum_ref[...] = cum
    egl_ref[...] = jnp.exp(g_last)


def gdn_gates(ab, a_log, dt_bias):
    rows = ab.shape[0]
    tm = 256
    pad = lambda v: jnp.pad(v.reshape(1, -1), ((0, 0), (0, V7X_LANES - GDN_SCALARS)))
    spec = pl.BlockSpec((tm, V7X_LANES), lambda i: (i, 0))
    vec = pl.BlockSpec((1, V7X_LANES), lambda i: (0, 0))
    return pl.pallas_call(
        _gdn_gates_kernel,
        grid=(rows // tm,),
        in_specs=[pl.BlockSpec((tm, 2 * V7X_LANES), lambda i: (i, 0)), vec, vec],
        out_specs=[spec] * 5,
        out_shape=[jax.ShapeDtypeStruct((rows, V7X_LANES), F32)] * 5,
        compiler_params=_params(1), name="gdn_gates",
    )(ab, pad(a_log), pad(dt_bias))


def _bdot(a, b):
    return jnp.dot(a.astype(BF16), b.astype(BF16), preferred_element_type=F32)


def _bdot_nt(a, b):
    return lax.dot_general(a.astype(BF16), b.astype(BF16), (((1,), (1,)), ((), ())), preferred_element_type=F32)


def _bdot_tn(a, b):
    return lax.dot_general(a.astype(BF16), b.astype(BF16), (((0,), (0,)), ((), ())), preferred_element_type=F32)


def _gdn_group_terms(qkv_ref, col_ref, row_ref, problems, want_out):
    ri = lax.broadcasted_iota(jnp.int32, (GDN_GROUP, GDN_GROUP), 0)
    ci = lax.broadcasted_iota(jnp.int32, (GDN_GROUP, GDN_GROUP), 1)
    same = (ri >> LOG2_CHUNK) == (ci >> LOG2_CHUNK)
    incl = [same & (ri >= ci), same & (ri <= ci)]
    strict = [same & (ri > ci), same & (ri < ci)]
    eye = jnp.where(ri == ci, 1.0, 0.0)
    pair = (ri >> 1) == (ci >> 1)
    joins = [((ri >> (lg + 1)) == (ci >> (lg + 1))) & ((ri >> lg) != (ci >> lg)) for lg in range(1, LOG2_CHUNK)]

    def load(gi, d):
        rows = pl.ds(pl.multiple_of(gi * GDN_GROUP, GDN_GROUP), GDN_GROUP)
        base = 5 * d
        return dict(
            q=qkv_ref[0][rows, :].astype(F32), k=qkv_ref[1][rows, :].astype(F32), v=qkv_ref[2][rows, :].astype(F32),
            beta=col_ref[0, rows, base + 0:base + 1], ecum=col_ref[0, rows, base + 1:base + 2],
            edec=col_ref[0, rows, base + 2:base + 3], cum_c=col_ref[0, rows, base + 3:base + 4],
            cum_r=row_ref[0, gi, d:d + 1, :], d=d)

    ps = [load(gi, d) for gi, d in problems]
    for p in ps:
        p["gamma"] = jnp.exp(jnp.where(incl[p["d"]], p["cum_c"] - p["cum_r"], NEG_BIG))
        p["kb"] = p["k"] * p["beta"]
    for p in ps:
        p["a"] = jnp.where(strict[p["d"]], _bdot_nt(p["kb"], p["k"]) * p["gamma"], 0.0)
    for p in ps:
        p["inv"] = eye - jnp.where(pair, p["a"], 0.0)
    for join in joins:
        for p in ps:
            p["t"] = _bdot(jnp.where(join, p["a"], 0.0), p["inv"])
        for p in ps:
            p["inv"] = p["inv"] - _bdot(p["inv"], p["t"])
    for p in ps:
        p["uw"] = _bdot(p["inv"], jnp.concatenate([p["v"] * p["beta"], p["kb"] * p["ecum"]], axis=-1))
        p["kdec"] = p["k"] * p["edec"]
    if want_out:
        for p in ps:
            p["qk"] = jnp.where(incl[p["d"]], _bdot_nt(p["q"], p["k"]) * p["gamma"], 0.0)
    out = []
    for p in ps:
        kw_t, n_t = [], []
        for c in range(GDN_GROUP // GDN_CHUNK):
            sl = slice(c * GDN_CHUNK, (c + 1) * GDN_CHUNK)
            both = _bdot_tn(p["uw"][sl], p["kdec"][sl])
            n_t.append(both[:GDN_DV])
            kw_t.append(both[GDN_DV:])
        q_eff = o_local = None
        if want_out:
            corr = _bdot(p["qk"], p["uw"])
            q_eff, o_local = p["q"] * p["ecum"] - corr[:, GDN_DV:], corr[:, :GDN_DV]
        out.append((kw_t, n_t, q_eff, o_local))
    return out


def _gdn_scan_kernel(ql_ref, kl_ref, vl_ref, qc_ref, kc_ref, vc_ref, coll_ref, colc_ref, rowl_ref, rowc_ref,
                     gate_ref, onorm_ref, y_ref, kw_ref, nt_ref, qeff_ref, oloc_ref, out_ref):
    n_ctx = qc_ref.shape[0] // GDN_CHUNK
    n_lat = ql_ref.shape[0] // GDN_CHUNK
    per_group = GDN_GROUP // GDN_CHUNK

    def precompute(qkv, col_ref, row_ref, slot0, want_out):
        groups = min(GDN_GROUPS_PER_STEP, qkv[0].shape[0] // GDN_GROUP)

        def body(it, carry):
            problems = [(it * groups + gg, d) for gg in range(groups) for d in range(2)]
            terms = _gdn_group_terms(qkv, col_ref, row_ref, problems, want_out)
            for (gi, d), (kw_t, n_t, q_eff, o_local) in zip(problems, terms):
                for c in range(per_group):
                    kw_ref[d, slot0 + gi * per_group + c] = kw_t[c].astype(kw_ref.dtype)
                    nt_ref[d, slot0 + gi * per_group + c] = n_t[c]
                if want_out:
                    rows = pl.ds(pl.multiple_of(gi * GDN_GROUP, GDN_GROUP), GDN_GROUP)
                    qeff_ref[d, rows, :] = q_eff.astype(qeff_ref.dtype)
                    oloc_ref[d, rows, :] = o_local
            return carry
        lax.fori_loop(0, qkv[0].shape[0] // (GDN_GROUP * groups), body, 0)

    precompute((qc_ref, kc_ref, vc_ref), colc_ref, rowc_ref, 0, False)
    precompute((ql_ref, kl_ref, vl_ref), coll_ref, rowl_ref, n_ctx, True)

    def advance(st, d, slot, egl):
        return egl * st - jnp.dot(st.astype(BF16), kw_ref[d, slot], preferred_element_type=F32) + nt_ref[d, slot]

    def chunk_rows(c):
        return pl.ds(pl.multiple_of(c * GDN_CHUNK, GDN_CHUNK), GDN_CHUNK)

    def scan_ctx(step, carry):
        st_f, st_b = carry
        cf, cb = step, n_ctx - 1 - step
        egl_f = colc_ref[0, pl.ds(cf * GDN_CHUNK, 1), 4:5]
        egl_b = colc_ref[0, pl.ds(cb * GDN_CHUNK, 1), 9:10]
        return advance(st_f, 0, cf, egl_f), advance(st_b, 1, cb, egl_b)

    def scan_lat(step, carry):
        st_f, st_b = carry
        cf, cb = step, n_lat - 1 - step
        rf, rb = chunk_rows(cf), chunk_rows(cb)
        out_ref[0, rf, :] = _bdot_nt(qeff_ref[0, rf, :], st_f) + oloc_ref[0, rf, :]
        out_ref[1, rb, :] = _bdot_nt(qeff_ref[1, rb, :], st_b) + oloc_ref[1, rb, :]
        egl_f = coll_ref[0, pl.ds(cf * GDN_CHUNK, 1), 4:5]
        egl_b = coll_ref[0, pl.ds(cb * GDN_CHUNK, 1), 9:10]
        return advance(st_f, 0, n_ctx + cf, egl_f), advance(st_b, 1, n_ctx + cb, egl_b)

    zero = jnp.zeros((GDN_DV, GDN_DK), F32)
    carry = lax.fori_loop(0, n_ctx, scan_ctx, (zero, zero))
    lax.fori_loop(0, n_lat, scan_lat, carry)

    onorm = onorm_ref[...]

    def finish(gi, carry):
        rows = pl.ds(pl.multiple_of(gi * GDN_GROUP, GDN_GROUP), GDN_GROUP)
        o = out_ref[0, rows, :] + out_ref[1, rows, :]
        o = o * lax.rsqrt(jnp.mean(o * o, axis=-1, keepdims=True) + RMS_EPS) * onorm
        gte = gate_ref[rows, :].astype(F32)
        y_ref[rows, :] = (o * gte * jax.nn.sigmoid(gte)).astype(y_ref.dtype)
        return carry

    lax.fori_loop(0, ql_ref.shape[0] // GDN_GROUP, finish, 0)


def gdn_scan(qkv_l, qkv_c, col, row, gate, o_norm, batch, seq, ctx_len):
    n_lat, n_ctx = seq // GDN_CHUNK, ctx_len // GDN_CHUNK
    ctx_blk = batch * seq // ctx_len
    nh = GDN_HEADS
    lat = lambda q: pl.BlockSpec((seq, GDN_DK), lambda b, h, q=q: (b, q * nh + h))
    cx = lambda q: pl.BlockSpec((ctx_len, GDN_DK), lambda b, h, q=q: (b, q * nh + h))
    return pl.pallas_call(
        _gdn_scan_kernel,
        grid=(batch, nh),
        in_specs=[lat(0), lat(1), lat(2), cx(0), cx(1), cx(2),
                  pl.BlockSpec((1, seq, GDN_COLS), lambda b, h: (h, b, 0)),
                  pl.BlockSpec((1, ctx_len, GDN_COLS), lambda b, h: (h, ctx_blk + b, 0)),
                  pl.BlockSpec((1, seq // GDN_GROUP, 2, GDN_GROUP), lambda b, h: (h, b, 0, 0)),
                  pl.BlockSpec((1, ctx_len // GDN_GROUP, 2, GDN_GROUP), lambda b, h: (h, ctx_blk + b, 0, 0)),
                  pl.BlockSpec((seq, GDN_DV), lambda b, h: (b, h)),
                  pl.BlockSpec((1, GDN_DV), lambda b, h: (0, 0))],
        out_specs=pl.BlockSpec((seq, GDN_DV), lambda b, h: (b, h)),
        out_shape=jax.ShapeDtypeStruct((batch * seq, nh * GDN_DV), BF16),
        scratch_shapes=[pltpu.VMEM((2, n_ctx + n_lat, GDN_DK, GDN_DK), BF16),
                        pltpu.VMEM((2, n_ctx + n_lat, GDN_DV, GDN_DK), F32),
                        pltpu.VMEM((2, seq, GDN_DK), BF16),
                        pltpu.VMEM((2, seq, GDN_DV), F32),
                        pltpu.VMEM((2, seq, GDN_DV), F32)],
        compiler_params=_params(2), name="gdn_chunk_scan",
    )(qkv_l, qkv_l, qkv_l, qkv_c, qkv_c, qkv_c, col, col, row, row, gate, o_norm.reshape(1, GDN_DV))


def gdn_mixer_core(h, p, batch, seq, ctx_len):
    t_lat = batch * seq
    w_in = p["w_in"]
    hv = GDN_HEADS * GDN_DV
    w_qkv = w_in[:, :GDN_QKV].astype(BF16)
    qkv_l = gdn_in(h, w_qkv, p["conv_w"], seq, batch, 0)
    qkv_c = gdn_in(h, w_qkv, p["conv_w"], ctx_len, batch, t_lat)
    gate = project(h[:t_lat], w_in[:, GDN_QKV:GDN_QKV + hv].astype(BF16), out_dtype=BF16)
    lane_pad = ((0, 0), (0, V7X_LANES - GDN_SCALARS))
    w_a = jnp.pad(w_in[:, GDN_QKV + hv:GDN_QKV + hv + GDN_SCALARS], lane_pad)
    w_b = jnp.pad(w_in[:, GDN_QKV + hv + GDN_SCALARS:], lane_pad)
    ab = project(h, jnp.concatenate([w_a, w_b], axis=1).astype(BF16))
    parts = gdn_gates(ab, p["a_log"], p["dt_bias"])
    rows = ab.shape[0]
    stacked = jnp.stack([a[:, :GDN_SCALARS] for a in parts], axis=0).reshape(5, rows, 2, GDN_HEADS)
    col = stacked.transpose(3, 1, 2, 0).reshape(GDN_HEADS, rows, 10)
    col = jnp.pad(col, ((0, 0), (0, 0), (0, GDN_COLS - 10)))
    row = stacked[3].reshape(rows // GDN_GROUP, GDN_GROUP, 2, GDN_HEADS).transpose(3, 0, 2, 1)
    return gdn_scan(qkv_l, qkv_c, col, row, gate, p["o_norm"], batch, seq, ctx_len)


def kernel(x, c, ctx, c_ctx, mod_w, mod_b, ln_g, ln_b, ffn_w_in, ffn_w_out, hy_w_in, hy_b_in, hy_conv_w, hy_conv_b, hy_fw1, hy_fb1, hy_fw2, hy_fb2, hy_fw3, hy_fb3, hy_fw4, hy_freq, hy_skip, hy_w_out, hy_b_out, mla_w_in, mla_q_norm, mla_kv_norm, mla_w_uq, mla_w_ukv, mla_w_out, gdn_w_in, gdn_conv_w, gdn_a_log, gdn_dt_bias, gdn_o_norm, gdn_w_out):
    batch, seq, d = x.shape
    ctx_len = ctx.shape[1]
    t_lat, t_ctx = batch * seq, batch * ctx_len
    t_all = t_lat + t_ctx
    assert t_lat % ROW_TILE == 0 and t_ctx % ROW_TILE == 0 and seq % ROW_TILE == 0

    n_mod = -(-(batch + 1) // 8) * 8
    cc = jnp.concatenate([c, c_ctx[None], jnp.zeros((n_mod - batch - 1, d), F32)], axis=0)
    mods = adaln_all(cc, mod_w, mod_b).reshape(DEPTH, n_mod, 6, 1, d).transpose(0, 2, 1, 3, 4)
    zero_bias = jnp.zeros((d,), F32)

    s = jnp.concatenate([x.reshape(t_lat, d), ctx.reshape(t_ctx, d)], axis=0)
    h = modulate(s, mods[0, 1], mods[0, 0], seq, batch)
    rows = t_all
    for i in range(DEPTH):
        kind, j = i % N_MIXERS, i // N_MIXERS
        ctx_out = any(l % N_MIXERS != MIXER_HYENA for l in range(i + 1, DEPTH))
        rows_out = t_all if ctx_out else t_lat
        if kind == MIXER_HYENA:
            p = {"w_in": hy_w_in[j].astype(BF16), "b_in": hy_b_in[j], "conv_w": hy_conv_w[j], "conv_b": hy_conv_b[j],
                 "fw1": hy_fw1[j], "fb1": hy_fb1[j], "fw2": hy_fw2[j], "fb2": hy_fb2[j],
                 "fw3": hy_fw3[j], "fb3": hy_fb3[j], "fw4": hy_fw4[j], "freq": hy_freq[j], "skip": hy_skip[j]}
            y = hyena_mixer_core(h, p, seq, batch, 0)
            if ctx_out:
                y = jnp.concatenate([y, hyena_mixer_core(h, p, ctx_len, batch, t_lat)], axis=0)
            w_out, b_out = hy_w_out[j], hy_b_out[j]
        elif kind == MIXER_MLA:
            p = {"w_in": mla_w_in[j], "q_norm": mla_q_norm[j], "kv_norm": mla_kv_norm[j],
                 "w_uq": mla_w_uq[j], "w_ukv": mla_w_ukv[j]}
            q, k, v = mla_qkv(h, p, t_all, seq, batch)
            y = mla_attention(q, k, v, batch, seq, ctx_len, ctx_out)
            w_out, b_out = mla_w_out[j], zero_bias
        else:
            assert not ctx_out
            p = {"w_in": gdn_w_in[j], "conv_w": gdn_conv_w[j], "a_log": gdn_a_log[j],
                 "dt_bias": gdn_dt_bias[j], "o_norm": gdn_o_norm[j]}
            y = gdn_mixer_core(h, p, batch, seq, ctx_len)
            w_out, b_out = gdn_w_out[j], zero_bias
        s, h = out_ln(y, w_out.astype(BF16), b_out, s, mods[i, 2], ln_g[i, 0], ln_b[i, 0], mods[i, 4], mods[i, 3],
                      rows_out, seq, batch, ROW_TILE)
        a = ffn_in(h, ffn_w_in[i].astype(BF16), rows_out)
        nxt = min(i + 1, DEPTH - 1)
        s, h = out_ln(a, ffn_w_out[i].astype(BF16), zero_bias, s, mods[i, 5], ln_g[i, 1], ln_b[i, 1],
                      mods[nxt, 1], mods[nxt, 0], rows_out, seq, batch, ROW_TILE // 2)
        rows = rows_out
    del rows
    return s[:t_lat].reshape(batch, seq, d)
```

```python
import functools
import math

import numpy as np

import jax
import jax.numpy as jnp
from jax import lax
from jax.experimental import pallas as pl
from jax.experimental.pallas import tpu as pltpu

F32 = jnp.float32
BF16 = jnp.bfloat16
HIGHEST = lax.Precision.HIGHEST

D_MODEL = 1024
DEPTH = 4
GRID_W = 64
N_MIXERS = 3
MIXER_HYENA = 0
MIXER_MLA = 1
DEEPNORM_ALPHA = (2 * DEPTH) ** 0.25
LN_EPS = 1e-5
RMS_EPS = 1e-6
D_FF = -(-8 * D_MODEL // (3 * 256)) * 256
HYENA_EMB = 33
HYENA_FILTER_HIDDEN = 64
HYENA_DECAY_TARGET = 1e-2
HYENA_FAST_DECAY_PCT = 0.3
HYENA_SLOW_DECAY_PCT = 1.5
MLA_HEADS = 8
MLA_Q_LORA = 384
MLA_KV_LORA = 256
MLA_NOPE = 128
MLA_ROPE = 64
MLA_V = 128
MLA_SCALE = (MLA_NOPE + MLA_ROPE) ** -0.5
MLA_QK = 256
ROPE_THETA = 10000.0
GDN_HEADS = 8
GDN_DK = 128
GDN_DV = 128
GDN_CHUNK = 128

V7X_LANES = 128
V7X_MXU_DIM = 256
V7X_VMEM_LIMIT_BYTES = 56 * 1024 * 1024
ROW_TILE = 1024


def _params(n_axes):
    return pltpu.CompilerParams(dimension_semantics=("arbitrary",) * n_axes,
                                vmem_limit_bytes=V7X_VMEM_LIMIT_BYTES)


def _mod_row_map(tile, seq, batch):
    return lambda i: (jnp.minimum(i * tile // seq, batch), 0, 0)


def _adaln_kernel(c_ref, w_ref, b_ref, o_ref):
    x = c_ref[...]
    x = (x * jax.nn.sigmoid(x)).astype(BF16)
    o_ref[0] = jnp.dot(x, w_ref[0].astype(BF16), preferred_element_type=F32) + b_ref[0]


def adaln_all(cc, mod_w, mod_b):
    r, d = cc.shape
    depth, _, n = mod_w.shape
    tn = 1536
    return pl.pallas_call(
        _adaln_kernel,
        grid=(depth, n // tn),
        in_specs=[pl.BlockSpec((r, d), lambda l, j: (0, 0)),
                  pl.BlockSpec((1, d, tn), lambda l, j: (l, 0, j)),
                  pl.BlockSpec((1, 1, tn), lambda l, j: (l, 0, j))],
        out_specs=pl.BlockSpec((1, r, tn), lambda l, j: (l, 0, j)),
        out_shape=jax.ShapeDtypeStruct((depth, r, n), F32),
        compiler_params=_params(2), name="adaln_table",
    )(cc, mod_w, mod_b.reshape(depth, 1, n))


def _assemble_kernel(n_lat_tiles, x_ref, c_ref, scale_ref, shift_ref, s_ref, h_ref):
    s = jnp.where(pl.program_id(0) < n_lat_tiles, x_ref[...], c_ref[...])
    s_ref[...] = s
    h_ref[...] = (s * (1.0 + scale_ref[0]) + shift_ref[0]).astype(h_ref.dtype)


def assemble_stream(x2d, ctx2d, scale, shift, seq, batch):
    t_lat, d = x2d.shape
    tm = ROW_TILE
    n_lat, n_ctx = t_lat // tm, ctx2d.shape[0] // tm
    rows = t_lat + ctx2d.shape[0]
    mod_spec = pl.BlockSpec((1, 1, d), _mod_row_map(tm, seq, batch))
    row_spec = pl.BlockSpec((tm, d), lambda i: (i, 0))
    return pl.pallas_call(
        functools.partial(_assemble_kernel, n_lat),
        grid=(n_lat + n_ctx,),
        in_specs=[pl.BlockSpec((tm, d), lambda i: (jnp.minimum(i, n_lat - 1), 0)),
                  pl.BlockSpec((tm, d), lambda i: (jnp.maximum(i - n_lat, 0), 0)), mod_spec, mod_spec],
        out_specs=[row_spec, row_spec],
        out_shape=[jax.ShapeDtypeStruct((rows, d), F32), jax.ShapeDtypeStruct((rows, d), BF16)],
        compiler_params=_params(1), name="assemble_modulate",
    )(x2d, ctx2d, scale, shift)


def _proj_kernel(x_ref, w_ref, o_ref):
    o_ref[...] = jnp.dot(x_ref[...], w_ref[...], preferred_element_type=F32).astype(o_ref.dtype)


def project(x, w, out_dtype=F32, tm=ROW_TILE, tn=None):
    m, k = x.shape
    n = w.shape[1]
    tn = n if tn is None else tn
    return pl.pallas_call(
        _proj_kernel,
        grid=(m // tm, n // tn),
        in_specs=[pl.BlockSpec((tm, k), lambda i, j: (i, 0)),
                  pl.BlockSpec((k, tn), lambda i, j: (0, j))],
        out_specs=pl.BlockSpec((tm, tn), lambda i, j: (i, j)),
        out_shape=jax.ShapeDtypeStruct((m, n), out_dtype),
        compiler_params=_params(2), name="projection",
    )(x, w)


def _out_ln_kernel(x_ref, w_ref, b_ref, s_ref, gate_ref, lng_ref, lnb_ref, scale_ref, shift_ref,
                   s_out_ref, h_out_ref):
    y = jnp.dot(x_ref[...].astype(BF16), w_ref[...], preferred_element_type=F32) + b_ref[...]
    z = DEEPNORM_ALPHA * s_ref[...] + gate_ref[0] * y
    mu = jnp.mean(z, axis=-1, keepdims=True)
    zc = z - mu
    var = jnp.mean(zc * zc, axis=-1, keepdims=True)
    sn = zc * lax.rsqrt(var + LN_EPS) * lng_ref[...] + lnb_ref[...]
    s_out_ref[...] = sn
    h_out_ref[...] = (sn * (1.0 + scale_ref[0]) + shift_ref[0]).astype(h_out_ref.dtype)


def out_ln(x, w, b, s, gate, ln_g, ln_b, scale, shift, rows, seq, batch, tm):
    k = x.shape[1]
    d = w.shape[1]
    mod_spec = pl.BlockSpec((1, 1, d), _mod_row_map(tm, seq, batch))
    vec_spec = pl.BlockSpec((1, d), lambda i: (0, 0))
    row_spec = pl.BlockSpec((tm, d), lambda i: (i, 0))
    return pl.pallas_call(
        _out_ln_kernel,
        grid=(rows // tm,),
        in_specs=[pl.BlockSpec((tm, k), lambda i: (i, 0)),
                  pl.BlockSpec((k, d), lambda i: (0, 0)),
                  vec_spec, row_spec, mod_spec, vec_spec, vec_spec, mod_spec, mod_spec],
        out_specs=[row_spec, row_spec],
        out_shape=[jax.ShapeDtypeStruct((rows, d), F32), jax.ShapeDtypeStruct((rows, d), BF16)],
        compiler_params=_params(1), name="out_proj_ln",
    )(x, w, b.reshape(1, d), s, gate, ln_g.reshape(1, d), ln_b.reshape(1, d), scale, shift)


def _ffn_in_kernel(x_ref, w_ref, a_ref):
    x = x_ref[...]
    dff = a_ref.shape[1]
    for c in range(dff // V7X_MXU_DIM):
        lo = c * V7X_MXU_DIM
        g = jnp.dot(x, w_ref[:, lo:lo + V7X_MXU_DIM], preferred_element_type=F32)
        u = jnp.dot(x, w_ref[:, dff + lo:dff + lo + V7X_MXU_DIM], preferred_element_type=F32)
        a_ref[:, lo:lo + V7X_MXU_DIM] = (g * jax.nn.sigmoid(g) * u).astype(a_ref.dtype)


def ffn_in(h, w_in, rows):
    k = h.shape[1]
    dff = w_in.shape[1] // 2
    tm = ROW_TILE
    return pl.pallas_call(
        _ffn_in_kernel,
        grid=(rows // tm,),
        in_specs=[pl.BlockSpec((tm, k), lambda i: (i, 0)),
                  pl.BlockSpec((k, 2 * dff), lambda i: (0, 0))],
        out_specs=pl.BlockSpec((tm, dff), lambda i: (i, 0)),
        out_shape=jax.ShapeDtypeStruct((rows, dff), BF16),
        compiler_params=_params(1), name="ffn_swiglu_in",
    )(h, w_in)


V7X_SUBLANES = 8


def _centred_conv3(z, bias, cw, const):
    length = z.shape[0]
    c0, c1, c2 = cw[0:1], cw[1:2], cw[2:3]
    out = pltpu.roll(z, 1, 0) * c0 + z * c1 + pltpu.roll(z, length - 1, 0) * c2
    wrapped_last, wrapped_first = z[length - 1:length], z[0:1]
    if bias is not None:
        out = out + (const + bias * (c0 + c1 + c2))
        wrapped_last, wrapped_first = wrapped_last + bias, wrapped_first + bias
    e = V7X_SUBLANES
    r = lax.broadcasted_iota(jnp.int32, (e, 1), 0)
    top = jnp.where(r == 0, out[0:e] - wrapped_last * c0, out[0:e])
    bot = jnp.where(r == e - 1, out[length - e:] - wrapped_first * c2, out[length - e:])
    return jnp.concatenate([top, out[e:length - e], bot], axis=0)


def _hyena_in_kernel(x_ref, w0_ref, w1_ref, w2_ref, b0_ref, b1_ref, b2_ref, cw0_ref, cw1_ref, cw2_ref,
                     cb0_ref, cb1_ref, cb2_ref, x0_ref, u_ref):
    x = x_ref[...]

    def branch(w_ref, b_ref, cw_ref, cb_ref):
        z = jnp.dot(x, w_ref[...], preferred_element_type=F32)
        return _centred_conv3(z, b_ref[...], cw_ref[...], cb_ref[...])

    x0 = branch(w0_ref, b0_ref, cw0_ref, cb0_ref)
    x1 = branch(w1_ref, b1_ref, cw1_ref, cb1_ref)
    v = branch(w2_ref, b2_ref, cw2_ref, cb2_ref)
    x0_ref[...] = x0.astype(x0_ref.dtype)
    u_ref[...] = (x1 * v).astype(u_ref.dtype)


def hyena_in(h, w_in, b_in, conv_w, conv_b, length, n_seq, row_offset):
    k = h.shape[1]
    d = w_in.shape[1] // 3
    tn = 256
    nb = d // tn
    off = row_offset // length
    w_specs = [pl.BlockSpec((k, tn), lambda b, j, q=q: (0, j + q * nb)) for q in range(3)]
    v_specs = [pl.BlockSpec((1, tn), lambda b, j, q=q: (0, j + q * nb)) for q in range(3)]
    cw_specs = [pl.BlockSpec((3, tn), lambda b, j, q=q: (0, j + q * nb)) for q in range(3)]
    out_spec = pl.BlockSpec((length, tn), lambda b, j: (b, j))
    b2, cb2 = b_in.reshape(1, 3 * d), conv_b.reshape(1, 3 * d)
    return pl.pallas_call(
        _hyena_in_kernel,
        grid=(n_seq, nb),
        in_specs=[pl.BlockSpec((length, k), lambda b, j: (b + off, 0))] + w_specs + v_specs + cw_specs + v_specs,
        out_specs=[out_spec, out_spec],
        out_shape=[jax.ShapeDtypeStruct((n_seq * length, d), BF16)] * 2,
        compiler_params=_params(2), name="hyena_in_proj_conv3",
    )(h, w_in, w_in, w_in, b2, b2, b2, conv_w, conv_w, conv_w, cb2, cb2, cb2)


def _hyena_filter_kernel(feat_ref, w1_ref, b1_ref, w2_ref, b2_ref, w3_ref, b3_ref, w4_ref, freq_ref, win_ref, k_ref):
    freq = freq_ref[...]
    z = jnp.sin(freq * (jnp.dot(feat_ref[...], w1_ref[...], precision=HIGHEST, preferred_element_type=F32) + b1_ref[...]))
    z = jnp.sin(freq * (jnp.dot(z, w2_ref[...], precision=HIGHEST, preferred_element_type=F32) + b2_ref[...]))
    z = jnp.sin(freq * (jnp.dot(z, w3_ref[...], precision=HIGHEST, preferred_element_type=F32) + b3_ref[...]))
    taps = jnp.dot(z, w4_ref[...], precision=HIGHEST, preferred_element_type=F32)
    d = win_ref.shape[1]
    win = win_ref[...]
    k_fwd = taps[:, :d] * win
    k_bwd = taps[:, d:] * win
    tile = k_fwd.shape[0]
    first = (lax.broadcasted_iota(jnp.int32, (tile, 1), 0) + pl.program_id(0) * tile) == 0
    k_ref[:, :d] = jnp.where(first, k_fwd + k_bwd, k_fwd).astype(k_ref.dtype)
    k_ref[:, d:] = jnp.where(first, 0.0, k_bwd).astype(k_ref.dtype)


def _hyena_features(length):
    t01 = np.linspace(0.0, 1.0, length, dtype=np.float32)[:, None]
    bands = (HYENA_EMB - 1) // 2
    w = (np.float32(2.0 * math.pi / length) * np.arange(length, dtype=np.float32)).astype(np.float32)
    f = np.linspace(1e-4, bands - 1, bands, dtype=np.float32)
    ang = (w[:, None] * f[None, :]).astype(np.float32)
    feat = np.concatenate([t01, np.cos(ang), -np.sin(ang)], axis=-1).astype(np.float32)
    return np.pad(feat, ((0, 0), (0, V7X_LANES - HYENA_EMB)))


def _hyena_window(length):
    max_decay = math.log(HYENA_DECAY_TARGET) / HYENA_FAST_DECAY_PCT
    min_decay = math.log(HYENA_DECAY_TARGET) / HYENA_SLOW_DECAY_PCT
    deltas = np.abs(np.linspace(min_decay, max_decay, D_MODEL, dtype=np.float32))
    t = np.linspace(0.0, 1.0, length, dtype=np.float32)
    return np.exp(-t[:, None] * deltas[None, :]).astype(np.float32)


def hyena_filter_taps(length, p):
    hid = V7X_LANES
    pad_h = hid - HYENA_FILTER_HIDDEN
    padc = lambda a: jnp.pad(a, ((0, 0), (0, pad_h)))
    w1 = jnp.pad(p["fw1"], ((0, V7X_LANES - HYENA_EMB), (0, pad_h)))
    w2 = jnp.pad(p["fw2"], ((0, pad_h), (0, pad_h)))
    w3 = jnp.pad(p["fw3"], ((0, pad_h), (0, pad_h)))
    w4 = jnp.pad(p["fw4"], ((0, pad_h), (0, 0)))
    vec = lambda a: padc(a.reshape(1, -1))
    args = (jnp.asarray(_hyena_features(length)), w1, vec(p["fb1"]), w2, vec(p["fb2"]), w3, vec(p["fb3"]), w4,
            vec(p["freq"]), jnp.asarray(_hyena_window(length)))
    tl = min(256, length)
    full = lambda a: pl.BlockSpec(a.shape, lambda i: (0, 0))
    return pl.pallas_call(
        _hyena_filter_kernel,
        grid=(length // tl,),
        in_specs=[pl.BlockSpec((tl, hid), lambda i: (i, 0))] + [full(a) for a in args[1:9]]
                 + [pl.BlockSpec((tl, D_MODEL), lambda i: (i, 0))],
        out_specs=pl.BlockSpec((tl, 2 * D_MODEL), lambda i: (i, 0)),
        out_shape=jax.ShapeDtypeStruct((length, 2 * D_MODEL), BF16),
        compiler_params=_params(1), name="hyena_filter",
    )(*args)


@functools.lru_cache(maxsize=None)
def _negacyclic_dft(length):
    n = 2 * length
    k = np.arange(length, dtype=np.int64)
    turns = ((2 * k[:, None] + 1) * k[None, :]) % (2 * n)
    ang = turns.astype(np.float64) * (2.0 * math.pi / (2 * n))
    c, s = np.cos(ang), np.sin(ang)
    inv = 2.0 / n
    return c, s, (c.T * inv), (s.T * inv)


def _dft_spec_kernel(c_ref, s_ref, u_ref, p_ref):
    u = u_ref[...]
    p_ref[0] = jnp.dot(c_ref[...], u, preferred_element_type=F32)
    p_ref[1] = jnp.dot(s_ref[...], u, preferred_element_type=F32)


def _dft_fwd_kernel(c_ref, s_ref, u_ref, hr_ref, hi_ref, y_ref):
    u = u_ref[...]
    ur = jnp.dot(c_ref[...], u, preferred_element_type=F32)
    us = jnp.dot(s_ref[...], u, preferred_element_type=F32)
    hr, hi = hr_ref[...], hi_ref[...]
    y_ref[0, 0] = (ur * hr + us * hi).astype(y_ref.dtype)
    y_ref[0, 1] = (us * hr - ur * hi).astype(y_ref.dtype)


def _dft_inv_kernel(ct_ref, st_ref, y_ref, u_ref, x0_ref, skip_ref, *rest):
    o_ref = rest[-1]
    y = (jnp.dot(ct_ref[...], y_ref[0, 0], preferred_element_type=F32)
         + jnp.dot(st_ref[...], y_ref[0, 1], preferred_element_type=F32))
    u = u_ref[...].astype(F32)
    o_ref[...] = ((y + u * skip_ref[...]) * x0_ref[...].astype(F32)).astype(o_ref.dtype)


def _into_buffer(into, n_inputs):
    if into is None:
        return [], (), {}
    return [pl.BlockSpec(memory_space=pl.ANY)], (into,), {n_inputs: 0}


def hyena_long_conv(x0, u, taps, skip, length, n_seq, out_rows, row_offset, into):
    d = u.shape[1]
    c, s, ct, st = (jnp.asarray(a, dtype=BF16) for a in _negacyclic_dft(length))
    tf = min(512, length)
    nf = length // tf
    spec = pl.pallas_call(
        _dft_spec_kernel,
        grid=(nf,),
        in_specs=[pl.BlockSpec((tf, length), lambda i: (i, 0)),
                  pl.BlockSpec((tf, length), lambda i: (i, 0)),
                  pl.BlockSpec((length, 2 * d), lambda i: (0, 0))],
        out_specs=pl.BlockSpec((2, tf, 2 * d), lambda i: (0, i, 0)),
        out_shape=jax.ShapeDtypeStruct((2, length, 2 * d), F32),
        compiler_params=_params(1), name="hyena_filter_spectrum",
    )(c, s, taps)
    h_re = spec[0, :, :d] + spec[0, :, d:]
    h_im = spec[1, :, d:] - spec[1, :, :d]
    y = pl.pallas_call(
        _dft_fwd_kernel,
        grid=(n_seq, nf),
        in_specs=[pl.BlockSpec((tf, length), lambda b, i: (i, 0)),
                  pl.BlockSpec((tf, length), lambda b, i: (i, 0)),
                  pl.BlockSpec((length, d), lambda b, i: (b, 0)),
                  pl.BlockSpec((tf, d), lambda b, i: (i, 0)),
                  pl.BlockSpec((tf, d), lambda b, i: (i, 0))],
        out_specs=pl.BlockSpec((1, 2, tf, d), lambda b, i: (b, 0, i, 0)),
        out_shape=jax.ShapeDtypeStruct((n_seq, 2, length, d), BF16),
        compiler_params=_params(2), name="hyena_dft_forward",
    )(c, s, u, h_re, h_im)
    into_specs, into_args, aliases = _into_buffer(into, 6)
    off = row_offset // tf
    return pl.pallas_call(
        _dft_inv_kernel,
        grid=(n_seq, nf),
        in_specs=[pl.BlockSpec((tf, length), lambda b, i: (i, 0)),
                  pl.BlockSpec((tf, length), lambda b, i: (i, 0)),
                  pl.BlockSpec((1, 2, length, d), lambda b, i: (b, 0, 0, 0)),
                  pl.BlockSpec((tf, d), lambda b, i: (b * nf + i, 0)),
                  pl.BlockSpec((tf, d), lambda b, i: (b * nf + i, 0)),
                  pl.BlockSpec((1, d), lambda b, i: (0, 0))] + into_specs,
        out_specs=pl.BlockSpec((tf, d), lambda b, i: (off + b * nf + i, 0)),
        out_shape=jax.ShapeDtypeStruct((out_rows, d), BF16),
        input_output_aliases=aliases,
        compiler_params=_params(2), name="hyena_dft_inverse",
    )(ct, st, y, u, x0, skip.reshape(1, d), *into_args)


def hyena_mixer_core(h, p, length, n_seq, row_offset, out_rows, into=None):
    x0, u = hyena_in(h, p["w_in"], p["b_in"], p["conv_w"], p["conv_b"], length, n_seq, row_offset)
    taps = hyena_filter_taps(length, p)
    return hyena_long_conv(x0, u, taps, p["skip"], length, n_seq, out_rows, row_offset, into)


def _rope_swap_columns():
    half, quarter = MLA_ROPE // 2, MLA_ROPE // 4
    r = np.arange(MLA_ROPE)
    within = r % half
    src = np.where(within < quarter, r + quarter, r - quarter)
    sign = np.where(within < quarter, -1.0, 1.0).astype(np.float32)
    return src, sign


@functools.lru_cache(maxsize=None)
def _rope_tables(seq, ctx_rows):
    half, quarter = MLA_ROPE // 2, MLA_ROPE // 4
    inv_freq = (np.float32(ROPE_THETA) ** (-np.arange(quarter, dtype=np.float32) / quarter)).astype(np.float32)
    t = np.arange(seq)
    row, col = (t // GRID_W).astype(np.float32), (t % GRID_W).astype(np.float32)
    r = np.arange(MLA_ROPE)
    pos = np.where((r // half)[None, :] == 0, row[:, None], col[:, None]).astype(np.float32)
    ang = (pos * inv_freq[r % quarter][None, :]).astype(np.float32).astype(np.float64)
    cos = np.concatenate([np.cos(ang), np.ones((ctx_rows, MLA_ROPE))], axis=0)
    sin = np.concatenate([np.sin(ang), np.zeros((ctx_rows, MLA_ROPE))], axis=0)
    ones = np.ones((seq + ctx_rows, MLA_NOPE))
    tab_q = (np.concatenate([ones, cos, sin], axis=1) * MLA_SCALE).astype(np.float32)
    tab_k = np.concatenate([cos, sin], axis=1).astype(np.float32)
    return tab_q, tab_k


def _rms_in_kernel(x_ref, g_ref):
    x = x_ref[...]
    return (x * lax.rsqrt(jnp.mean(x * x, axis=-1, keepdims=True) + RMS_EPS) * g_ref[...]).astype(BF16)


def _mla_q_kernel(x_ref, g_ref, w_ref, tab_ref, q_ref):
    acc = jnp.dot(_rms_in_kernel(x_ref, g_ref), w_ref[...], preferred_element_type=F32)
    tab = tab_ref[...]
    for h in range(MLA_HEADS):
        q_ref[:, h * MLA_QK:(h + 1) * MLA_QK] = (acc[:, h * MLA_QK:(h + 1) * MLA_QK] * tab).astype(q_ref.dtype)


def _mla_kv_kernel(x_ref, g_ref, w_ref, rope_ref, tab_ref, k_ref, v_ref):
    acc = jnp.dot(_rms_in_kernel(x_ref, g_ref), w_ref[...], preferred_element_type=F32)
    prod = rope_ref[...] * tab_ref[...]
    k_rot = (prod + pltpu.roll(prod, MLA_ROPE, 1)).astype(k_ref.dtype)
    hw = MLA_NOPE + MLA_V
    for h in range(MLA_HEADS):
        k_ref[:, h * MLA_QK:h * MLA_QK + MLA_NOPE] = acc[:, h * hw:h * hw + MLA_NOPE].astype(k_ref.dtype)
        k_ref[:, h * MLA_QK + MLA_NOPE:(h + 1) * MLA_QK] = k_rot
        v_ref[:, h * MLA_V:(h + 1) * MLA_V] = acc[:, h * hw + MLA_NOPE:(h + 1) * hw].astype(v_ref.dtype)


def mla_qkv(h, p, rows, seq, batch):
    tm = ROW_TILE
    src, sign = _rope_swap_columns()
    w_in = p["w_in"]
    o_kv, o_rope = MLA_Q_LORA, MLA_Q_LORA + MLA_KV_LORA
    w_rope = w_in[:, o_rope:]
    w_cat = jnp.concatenate([w_in[:, o_kv:o_rope], w_rope, w_rope[:, src] * sign, w_in[:, :o_kv]], axis=1).astype(BF16)
    z = project(h[:rows], w_cat)
    w_uq = p["w_uq"].reshape(MLA_Q_LORA, MLA_HEADS, MLA_NOPE + MLA_ROPE)
    w_uq_rope = w_uq[:, :, MLA_NOPE:]
    w_uq = jnp.concatenate([w_uq, w_uq_rope[:, :, src] * sign], axis=2).reshape(MLA_Q_LORA, MLA_HEADS * MLA_QK).astype(BF16)
    tab_q, tab_k = _rope_tables(seq, tm)
    lat_tiles = batch * seq // tm
    tab_map = lambda i: (jnp.where(i < lat_tiles, i % (seq // tm), seq // tm), 0)
    n_tiles = rows // tm
    q = pl.pallas_call(
        _mla_q_kernel,
        grid=(n_tiles,),
        in_specs=[pl.BlockSpec((tm, MLA_Q_LORA), lambda i: (i, 1)),
                  pl.BlockSpec((1, MLA_Q_LORA), lambda i: (0, 0)),
                  pl.BlockSpec((MLA_Q_LORA, MLA_HEADS * MLA_QK), lambda i: (0, 0)),
                  pl.BlockSpec((tm, MLA_QK), tab_map)],
        out_specs=pl.BlockSpec((tm, MLA_HEADS * MLA_QK), lambda i: (i, 0)),
        out_shape=jax.ShapeDtypeStruct((rows, MLA_HEADS * MLA_QK), BF16),
        compiler_params=_params(1), name="mla_q_proj",
    )(z, p["q_norm"].reshape(1, -1), w_uq, jnp.asarray(tab_q))
    k, v = pl.pallas_call(
        _mla_kv_kernel,
        grid=(n_tiles,),
        in_specs=[pl.BlockSpec((tm, MLA_KV_LORA), lambda i: (i, 0)),
                  pl.BlockSpec((1, MLA_KV_LORA), lambda i: (0, 0)),
                  pl.BlockSpec((MLA_KV_LORA, MLA_HEADS * (MLA_NOPE + MLA_V)), lambda i: (0, 0)),
                  pl.BlockSpec((tm, 2 * MLA_ROPE), lambda i: (i, 2)),
                  pl.BlockSpec((tm, 2 * MLA_ROPE), tab_map)],
        out_specs=[pl.BlockSpec((tm, MLA_HEADS * MLA_QK), lambda i: (i, 0)),
                   pl.BlockSpec((tm, MLA_HEADS * MLA_V), lambda i: (i, 0))],
        out_shape=[jax.ShapeDtypeStruct((rows, MLA_HEADS * MLA_QK), BF16),
                   jax.ShapeDtypeStruct((rows, MLA_HEADS * MLA_V), BF16)],
        compiler_params=_params(1), name="mla_kv_proj",
    )(z, p["kv_norm"].reshape(1, -1), p["w_ukv"].astype(BF16), z, jnp.asarray(tab_k))
    return q, k, v


ATTN_SUB = 128


def _dot_nt(a, b):
    return lax.dot_general(a, b, (((1,), (1,)), ((), ())), preferred_element_type=F32)


def _attention_body(q_ref, key_refs, val_refs, o_ref):
    n_sub = q_ref.shape[0] // ATTN_SUB
    rows = lambda i: slice(i * ATTN_SUB, (i + 1) * ATTN_SUB)
    scores = lambda i: [_dot_nt(q_ref[rows(i), :], k_ref[...]) for k_ref in key_refs]
    pending = scores(0)
    for i in range(n_sub):
        s = pending
        if i + 1 < n_sub:
            pending = scores(i + 1)
        m = functools.reduce(jnp.maximum, [jnp.max(x, axis=-1, keepdims=True) for x in s])
        p = [jnp.exp(x - m) for x in s]
        l = sum(jnp.sum(x, axis=-1, keepdims=True) for x in p)
        o = sum(jnp.dot(x.astype(BF16), v_ref[...], preferred_element_type=F32) for x, v_ref in zip(p, val_refs))
        o_ref[rows(i), :] = (o / l).astype(o_ref.dtype)


def _attn_lat_kernel(q_ref, kl_ref, vl_ref, kc_ref, vc_ref, o_ref):
    _attention_body(q_ref, (kl_ref, kc_ref), (vl_ref, vc_ref), o_ref)


def _attn_ctx_kernel(q_ref, kc_ref, vc_ref, into_ref, o_ref):
    del into_ref
    _attention_body(q_ref, (kc_ref,), (vc_ref,), o_ref)


def mla_attention(q, k, v, batch, seq, ctx_len, with_ctx_out):
    tq = 512
    nq = seq // tq
    ctx_blk = batch * seq // ctx_len
    out_rows = batch * (seq + ctx_len) if with_ctx_out else batch * seq
    o_lat = pl.pallas_call(
        _attn_lat_kernel,
        grid=(batch, MLA_HEADS, nq),
        in_specs=[pl.BlockSpec((tq, MLA_QK), lambda b, h, i: (b * nq + i, h)),
                  pl.BlockSpec((seq, MLA_QK), lambda b, h, i: (b, h)),
                  pl.BlockSpec((seq, MLA_V), lambda b, h, i: (b, h)),
                  pl.BlockSpec((ctx_len, MLA_QK), lambda b, h, i: (ctx_blk + b, h)),
                  pl.BlockSpec((ctx_len, MLA_V), lambda b, h, i: (ctx_blk + b, h))],
        out_specs=pl.BlockSpec((tq, MLA_V), lambda b, h, i: (b * nq + i, h)),
        out_shape=jax.ShapeDtypeStruct((out_rows, MLA_HEADS * MLA_V), BF16),
        compiler_params=_params(3), name="mla_attention_latent",
    )(q, k, v, k, v)
    if not with_ctx_out:
        return o_lat
    into_specs, into_args, aliases = _into_buffer(o_lat, 3)
    return pl.pallas_call(
        _attn_ctx_kernel,
        grid=(batch, MLA_HEADS),
        in_specs=[pl.BlockSpec((ctx_len, MLA_QK), lambda b, h: (ctx_blk + b, h)),
                  pl.BlockSpec((ctx_len, MLA_QK), lambda b, h: (ctx_blk + b, h)),
                  pl.BlockSpec((ctx_len, MLA_V), lambda b, h: (ctx_blk + b, h))] + into_specs,
        out_specs=pl.BlockSpec((ctx_len, MLA_V), lambda b, h: (ctx_blk + b, h)),
        out_shape=jax.ShapeDtypeStruct((out_rows, MLA_HEADS * MLA_V), BF16),
        input_output_aliases=aliases,
        compiler_params=_params(2), name="mla_attention_context",
    )(q, k, v, *into_args)


GDN_QKV = 2 * GDN_HEADS * GDN_DK + GDN_HEADS * GDN_DV
GDN_SCALARS = 2 * GDN_HEADS
GDN_COLS = 16
LOG2_CHUNK = int(math.log2(GDN_CHUNK))
NEG_BIG = -1e30
GDN_GROUP = 128
GDN_GROUPS_PER_STEP = 8


def _gdn_in_kernel(x_ref, w_ref, cw_ref, o_ref):
    z = jnp.dot(x_ref[...], w_ref[...], preferred_element_type=F32)
    z = _centred_conv3(z, None, cw_ref[...], None)
    z = z * jax.nn.sigmoid(z)
    j = pl.program_id(1)
    blocks_per_kind = GDN_HEADS * GDN_DK // z.shape[1]
    q_scale = jnp.where(j < blocks_per_kind, GDN_DK ** -0.5, 1.0)
    for hh in range(z.shape[1] // GDN_DK):
        zh = z[:, hh * GDN_DK:(hh + 1) * GDN_DK]
        zn = zh * lax.rsqrt(jnp.sum(zh * zh, axis=-1, keepdims=True) + RMS_EPS) * q_scale
        o_ref[:, hh * GDN_DK:(hh + 1) * GDN_DK] = jnp.where(j < 2 * blocks_per_kind, zn, zh).astype(o_ref.dtype)


def gdn_in(h, w_qkv, conv_w, length, n_seq, row_offset):
    k = h.shape[1]
    n = w_qkv.shape[1]
    tn = 256
    off = row_offset // length
    return pl.pallas_call(
        _gdn_in_kernel,
        grid=(n_seq, n // tn),
        in_specs=[pl.BlockSpec((length, k), lambda b, j: (b + off, 0)),
                  pl.BlockSpec((k, tn), lambda b, j: (0, j)),
                  pl.BlockSpec((3, tn), lambda b, j: (0, j))],
        out_specs=pl.BlockSpec((length, tn), lambda b, j: (b, j)),
        out_shape=jax.ShapeDtypeStruct((n_seq * length, n), BF16),
        compiler_params=_params(2), name="gdn_in_proj_conv3",
    )(h, w_qkv, conv_w)


def _gdn_gates_kernel(ab_ref, alog_ref, dtb_ref, beta_ref, ecum_ref, edec_ref, cum_ref, egl_ref):
    a = ab_ref[:, :V7X_LANES]
    b = ab_ref[:, V7X_LANES:]
    x = a + dtb_ref[...]
    softplus = jnp.maximum(x, 0.0) + jnp.log(1.0 + jnp.exp(-jnp.abs(x)))
    g = -jnp.exp(alog_ref[...]) * softplus
    tm = g.shape[0]
    r = lax.broadcasted_iota(jnp.int32, (tm, tm), 0)
    c = lax.broadcasted_iota(jnp.int32, (tm, tm), 1)
    same = (r >> LOG2_CHUNK) == (c >> LOG2_CHUNK)
    prefix = jnp.where(same & (c <= r), 1.0, 0.0)
    suffix = jnp.where(same & (c >= r), 1.0, 0.0)
    total = jnp.where(same, 1.0, 0.0)
    dot = lambda m: jnp.dot(m, g, precision=HIGHEST, preferred_element_type=F32)
    lane = lax.broadcasted_iota(jnp.int32, g.shape, 1)
    cum = jnp.where(lane < GDN_HEADS, dot(prefix), dot(suffix))
    g_last = dot(total)
    beta_ref[...] = jax.nn.sigmoid(b)
    ecum_ref[...] = jnp.exp(cum)
    edec_ref[...] = jnp.exp(g_last - cum)
    cum_ref[...] = cum
    egl_ref[...] = jnp.exp(g_last)


def gdn_gates(ab, a_log, dt_bias):
    rows = ab.shape[0]
    tm = 256
    pad = lambda v: jnp.pad(v.reshape(1, -1), ((0, 0), (0, V7X_LANES - GDN_SCALARS)))
    spec = pl.BlockSpec((tm, V7X_LANES), lambda i: (i, 0))
    vec = pl.BlockSpec((1, V7X_LANES), lambda i: (0, 0))
    return pl.pallas_call(
        _gdn_gates_kernel,
        grid=(rows // tm,),
        in_specs=[pl.BlockSpec((tm, 2 * V7X_LANES), lambda i: (i, 0)), vec, vec],
        out_specs=[spec] * 5,
        out_shape=[jax.ShapeDtypeStruct((rows, V7X_LANES), F32)] * 5,
        compiler_params=_params(1), name="gdn_gates",
    )(ab, pad(a_log), pad(dt_bias))


def _bdot(a, b):
    return jnp.dot(a.astype(BF16), b.astype(BF16), preferred_element_type=F32)


def _bdot_nt(a, b):
    return lax.dot_general(a.astype(BF16), b.astype(BF16), (((1,), (1,)), ((), ())), preferred_element_type=F32)


def _bdot_tn(a, b):
    return lax.dot_general(a.astype(BF16), b.astype(BF16), (((0,), (0,)), ((), ())), preferred_element_type=F32)


def _gdn_group_terms(qkv_ref, col_ref, row_ref, problems, want_out):
    ri = lax.broadcasted_iota(jnp.int32, (GDN_GROUP, GDN_GROUP), 0)
    ci = lax.broadcasted_iota(jnp.int32, (GDN_GROUP, GDN_GROUP), 1)
    same = (ri >> LOG2_CHUNK) == (ci >> LOG2_CHUNK)
    incl = [same & (ri >= ci), same & (ri <= ci)]
    strict = [same & (ri > ci), same & (ri < ci)]
    eye = jnp.where(ri == ci, 1.0, 0.0)
    pair = (ri >> 1) == (ci >> 1)
    joins = [((ri >> (lg + 1)) == (ci >> (lg + 1))) & ((ri >> lg) != (ci >> lg)) for lg in range(1, LOG2_CHUNK)]

    def load(gi, d):
        rows = pl.ds(pl.multiple_of(gi * GDN_GROUP, GDN_GROUP), GDN_GROUP)
        base = 5 * d
        return dict(
            q=qkv_ref[0][rows, :].astype(F32), k=qkv_ref[1][rows, :].astype(F32), v=qkv_ref[2][rows, :].astype(F32),
            beta=col_ref[0, rows, base + 0:base + 1], ecum=col_ref[0, rows, base + 1:base + 2],
            edec=col_ref[0, rows, base + 2:base + 3], cum_c=col_ref[0, rows, base + 3:base + 4],
            cum_r=row_ref[0, gi, d:d + 1, :], d=d)

    ps = [load(gi, d) for gi, d in problems]
    for p in ps:
        p["gamma"] = jnp.exp(jnp.where(incl[p["d"]], p["cum_c"] - p["cum_r"], NEG_BIG))
        p["kb"] = p["k"] * p["beta"]
    for p in ps:
        p["a"] = jnp.where(strict[p["d"]], _bdot_nt(p["kb"], p["k"]) * p["gamma"], 0.0)
    for p in ps:
        p["inv"] = eye - jnp.where(pair, p["a"], 0.0)
    for join in joins:
        for p in ps:
            p["t"] = _bdot(jnp.where(join, p["a"], 0.0), p["inv"])
        for p in ps:
            p["inv"] = p["inv"] - _bdot(p["inv"], p["t"])
    for p in ps:
        p["uw"] = _bdot(p["inv"], jnp.concatenate([p["v"] * p["beta"], p["kb"] * p["ecum"]], axis=-1))
        p["kdec"] = p["k"] * p["edec"]
    if want_out:
        for p in ps:
            p["qk"] = jnp.where(incl[p["d"]], _bdot_nt(p["q"], p["k"]) * p["gamma"], 0.0)
    out = []
    for p in ps:
        kw_t, n_t = [], []
        for c in range(GDN_GROUP // GDN_CHUNK):
            sl = slice(c * GDN_CHUNK, (c + 1) * GDN_CHUNK)
            both = _bdot_tn(p["uw"][sl], p["kdec"][sl])
            n_t.append(both[:GDN_DV])
            kw_t.append(both[GDN_DV:])
        q_eff = o_local = None
        if want_out:
            corr = _bdot(p["qk"], p["uw"])
            q_eff, o_local = p["q"] * p["ecum"] - corr[:, GDN_DV:], corr[:, :GDN_DV]
        out.append((kw_t, n_t, q_eff, o_local))
    return out


def _gdn_scan_kernel(ql_ref, kl_ref, vl_ref, qc_ref, kc_ref, vc_ref, coll_ref, colc_ref, rowl_ref, rowc_ref,
                     gate_ref, onorm_ref, y_ref, kw_ref, nt_ref, qeff_ref, oloc_ref, out_ref):
    n_ctx = qc_ref.shape[0] // GDN_CHUNK
    n_lat = ql_ref.shape[0] // GDN_CHUNK
    per_group = GDN_GROUP // GDN_CHUNK

    def precompute(qkv, col_ref, row_ref, slot0, want_out):
        groups = min(GDN_GROUPS_PER_STEP, qkv[0].shape[0] // GDN_GROUP)

        def body(it, carry):
            problems = [(it * groups + gg, d) for gg in range(groups) for d in range(2)]
            terms = _gdn_group_terms(qkv, col_ref, row_ref, problems, want_out)
            for (gi, d), (kw_t, n_t, q_eff, o_local) in zip(problems, terms):
                for c in range(per_group):
                    kw_ref[d, slot0 + gi * per_group + c] = kw_t[c].astype(kw_ref.dtype)
                    nt_ref[d, slot0 + gi * per_group + c] = n_t[c]
                if want_out:
                    rows = pl.ds(pl.multiple_of(gi * GDN_GROUP, GDN_GROUP), GDN_GROUP)
                    qeff_ref[d, rows, :] = q_eff.astype(qeff_ref.dtype)
                    oloc_ref[d, rows, :] = o_local
            return carry
        lax.fori_loop(0, qkv[0].shape[0] // (GDN_GROUP * groups), body, 0)

    precompute((qc_ref, kc_ref, vc_ref), colc_ref, rowc_ref, 0, False)
    precompute((ql_ref, kl_ref, vl_ref), coll_ref, rowl_ref, n_ctx, True)

    def advance(st, d, slot, egl):
        return egl * st - jnp.dot(st.astype(BF16), kw_ref[d, slot], preferred_element_type=F32) + nt_ref[d, slot]

    def chunk_rows(c):
        return pl.ds(pl.multiple_of(c * GDN_CHUNK, GDN_CHUNK), GDN_CHUNK)

    def scan_ctx(step, carry):
        st_f, st_b = carry
        cf, cb = step, n_ctx - 1 - step
        egl_f = colc_ref[0, pl.ds(cf * GDN_CHUNK, 1), 4:5]
        egl_b = colc_ref[0, pl.ds(cb * GDN_CHUNK, 1), 9:10]
        return advance(st_f, 0, cf, egl_f), advance(st_b, 1, cb, egl_b)

    def scan_lat(step, carry):
        st_f, st_b = carry
        cf, cb = step, n_lat - 1 - step
        rf, rb = chunk_rows(cf), chunk_rows(cb)
        out_ref[0, rf, :] = _bdot_nt(qeff_ref[0, rf, :], st_f) + oloc_ref[0, rf, :]
        out_ref[1, rb, :] = _bdot_nt(qeff_ref[1, rb, :], st_b) + oloc_ref[1, rb, :]
        egl_f = coll_ref[0, pl.ds(cf * GDN_CHUNK, 1), 4:5]
        egl_b = coll_ref[0, pl.ds(cb * GDN_CHUNK, 1), 9:10]
        return advance(st_f, 0, n_ctx + cf, egl_f), advance(st_b, 1, n_ctx + cb, egl_b)

    zero = jnp.zeros((GDN_DV, GDN_DK), F32)
    carry = lax.fori_loop(0, n_ctx, scan_ctx, (zero, zero))
    lax.fori_loop(0, n_lat, scan_lat, carry)

    onorm = onorm_ref[...]

    def finish(gi, carry):
        rows = pl.ds(pl.multiple_of(gi * GDN_GROUP, GDN_GROUP), GDN_GROUP)
        o = out_ref[0, rows, :] + out_ref[1, rows, :]
        o = o * lax.rsqrt(jnp.mean(o * o, axis=-1, keepdims=True) + RMS_EPS) * onorm
        gte = gate_ref[rows, :].astype(F32)
        y_ref[rows, :] = (o * gte * jax.nn.sigmoid(gte)).astype(y_ref.dtype)
        return carry

    lax.fori_loop(0, ql_ref.shape[0] // GDN_GROUP, finish, 0)


def gdn_scan(qkv_l, qkv_c, col, row, gate, o_norm, batch, seq, ctx_len):
    n_lat, n_ctx = seq // GDN_CHUNK, ctx_len // GDN_CHUNK
    ctx_blk = batch * seq // ctx_len
    nh = GDN_HEADS
    lat = lambda q: pl.BlockSpec((seq, GDN_DK), lambda b, h, q=q: (b, q * nh + h))
    cx = lambda q: pl.BlockSpec((ctx_len, GDN_DK), lambda b, h, q=q: (b, q * nh + h))
    return pl.pallas_call(
        _gdn_scan_kernel,
        grid=(batch, nh),
        in_specs=[lat(0), lat(1), lat(2), cx(0), cx(1), cx(2),
                  pl.BlockSpec((1, seq, GDN_COLS), lambda b, h: (h, b, 0)),
                  pl.BlockSpec((1, ctx_len, GDN_COLS), lambda b, h: (h, ctx_blk + b, 0)),
                  pl.BlockSpec((1, seq // GDN_GROUP, 2, GDN_GROUP), lambda b, h: (h, b, 0, 0)),
                  pl.BlockSpec((1, ctx_len // GDN_GROUP, 2, GDN_GROUP), lambda b, h: (h, ctx_blk + b, 0, 0)),
                  pl.BlockSpec((seq, GDN_DV), lambda b, h: (b, h)),
                  pl.BlockSpec((1, GDN_DV), lambda b, h: (0, 0))],
        out_specs=pl.BlockSpec((seq, GDN_DV), lambda b, h: (b, h)),
        out_shape=jax.ShapeDtypeStruct((batch * seq, nh * GDN_DV), BF16),
        scratch_shapes=[pltpu.VMEM((2, n_ctx + n_lat, GDN_DK, GDN_DK), BF16),
                        pltpu.VMEM((2, n_ctx + n_lat, GDN_DV, GDN_DK), F32),
                        pltpu.VMEM((2, seq, GDN_DK), BF16),
                        pltpu.VMEM((2, seq, GDN_DV), F32),
                        pltpu.VMEM((2, seq, GDN_DV), F32)],
        compiler_params=_params(2), name="gdn_chunk_scan",
    )(qkv_l, qkv_l, qkv_l, qkv_c, qkv_c, qkv_c, col, col, row, row, gate, o_norm.reshape(1, GDN_DV))


def gdn_mixer_core(h, p, batch, seq, ctx_len):
    t_lat = batch * seq
    w_in = p["w_in"]
    hv = GDN_HEADS * GDN_DV
    w_qkv = w_in[:, :GDN_QKV].astype(BF16)
    qkv_l = gdn_in(h, w_qkv, p["conv_w"], seq, batch, 0)
    qkv_c = gdn_in(h, w_qkv, p["conv_w"], ctx_len, batch, t_lat)
    gate = project(h[:t_lat], w_in[:, GDN_QKV:GDN_QKV + hv].astype(BF16), out_dtype=BF16)
    lane_pad = ((0, 0), (0, V7X_LANES - GDN_SCALARS))
    w_a = jnp.pad(w_in[:, GDN_QKV + hv:GDN_QKV + hv + GDN_SCALARS], lane_pad)
    w_b = jnp.pad(w_in[:, GDN_QKV + hv + GDN_SCALARS:], lane_pad)
    ab = project(h, jnp.concatenate([w_a, w_b], axis=1).astype(BF16))
    parts = gdn_gates(ab, p["a_log"], p["dt_bias"])
    rows = ab.shape[0]
    stacked = jnp.stack([a[:, :GDN_SCALARS] for a in parts], axis=0).reshape(5, rows, 2, GDN_HEADS)
    col = stacked.transpose(3, 1, 2, 0).reshape(GDN_HEADS, rows, 10)
    col = jnp.pad(col, ((0, 0), (0, 0), (0, GDN_COLS - 10)))
    row = stacked[3].reshape(rows // GDN_GROUP, GDN_GROUP, 2, GDN_HEADS).transpose(3, 0, 2, 1)
    return gdn_scan(qkv_l, qkv_c, col, row, gate, p["o_norm"], batch, seq, ctx_len)


def kernel(x, c, ctx, c_ctx, mod_w, mod_b, ln_g, ln_b, ffn_w_in, ffn_w_out, hy_w_in, hy_b_in, hy_conv_w, hy_conv_b, hy_fw1, hy_fb1, hy_fw2, hy_fb2, hy_fw3, hy_fb3, hy_fw4, hy_freq, hy_skip, hy_w_out, hy_b_out, mla_w_in, mla_q_norm, mla_kv_norm, mla_w_uq, mla_w_ukv, mla_w_out, gdn_w_in, gdn_conv_w, gdn_a_log, gdn_dt_bias, gdn_o_norm, gdn_w_out):
    batch, seq, d = x.shape
    ctx_len = ctx.shape[1]
    t_lat, t_ctx = batch * seq, batch * ctx_len
    t_all = t_lat + t_ctx
    assert t_lat % ROW_TILE == 0 and t_ctx % ROW_TILE == 0 and seq % ROW_TILE == 0

    n_mod = -(-(batch + 1) // 8) * 8
    cc = jnp.concatenate([c, c_ctx[None], jnp.zeros((n_mod - batch - 1, d), F32)], axis=0)
    mods = adaln_all(cc, mod_w, mod_b).reshape(DEPTH, n_mod, 6, 1, d).transpose(0, 2, 1, 3, 4)
    zero_bias = jnp.zeros((d,), F32)

    s, h = assemble_stream(x.reshape(t_lat, d), ctx.reshape(t_ctx, d), mods[0, 1], mods[0, 0], seq, batch)
    rows = t_all
    for i in range(DEPTH):
        kind, j = i % N_MIXERS, i // N_MIXERS
        ctx_out = any(l % N_MIXERS != MIXER_HYENA for l in range(i + 1, DEPTH))
        rows_out = t_all if ctx_out else t_lat
        if kind == MIXER_HYENA:
            p = {"w_in": hy_w_in[j].astype(BF16), "b_in": hy_b_in[j], "conv_w": hy_conv_w[j], "conv_b": hy_conv_b[j],
                 "fw1": hy_fw1[j], "fb1": hy_fb1[j], "fw2": hy_fw2[j], "fb2": hy_fb2[j],
                 "fw3": hy_fw3[j], "fb3": hy_fb3[j], "fw4": hy_fw4[j], "freq": hy_freq[j], "skip": hy_skip[j]}
            y = hyena_mixer_core(h, p, seq, batch, 0, rows_out)
            if ctx_out:
                y = hyena_mixer_core(h, p, ctx_len, batch, t_lat, rows_out, into=y)
            w_out, b_out = hy_w_out[j], hy_b_out[j]
        elif kind == MIXER_MLA:
            p = {"w_in": mla_w_in[j], "q_norm": mla_q_norm[j], "kv_norm": mla_kv_norm[j],
                 "w_uq": mla_w_uq[j], "w_ukv": mla_w_ukv[j]}
            q, k, v = mla_qkv(h, p, t_all, seq, batch)
            y = mla_attention(q, k, v, batch, seq, ctx_len, ctx_out)
            w_out, b_out = mla_w_out[j], zero_bias
        else:
            assert not ctx_out
            p = {"w_in": gdn_w_in[j], "conv_w": gdn_conv_w[j], "a_log": gdn_a_log[j],
                 "dt_bias": gdn_dt_bias[j], "o_norm": gdn_o_norm[j]}
            y = gdn_mixer_core(h, p, batch, seq, ctx_len)
            w_out, b_out = gdn_w_out[j], zero_bias
        s, h = out_ln(y, w_out.astype(BF16), b_out, s, mods[i, 2], ln_g[i, 0], ln_b[i, 0], mods[i, 4], mods[i, 3],
                      rows_out, seq, batch, ROW_TILE)
        a = ffn_in(h, ffn_w_in[i].astype(BF16), rows_out)
        nxt = min(i + 1, DEPTH - 1)
        s, h = out_ln(a, ffn_w_out[i].astype(BF16), zero_bias, s, mods[i, 5], ln_g[i, 1], ln_b[i, 1],
                      mods[nxt, 1], mods[nxt, 0], rows_out, seq, batch, ROW_TILE // 2)
        rows = rows_out
    del rows
    return s[:t_lat].reshape(batch, seq, d)
```

```python
import functools
import math

import numpy as np

import jax
import jax.numpy as jnp
from jax import lax
from jax.experimental import pallas as pl
from jax.experimental.pallas import tpu as pltpu

F32 = jnp.float32
BF16 = jnp.bfloat16
HIGHEST = lax.Precision.HIGHEST

D_MODEL = 1024
DEPTH = 4
GRID_W = 64
N_MIXERS = 3
MIXER_HYENA = 0
MIXER_MLA = 1
DEEPNORM_ALPHA = (2 * DEPTH) ** 0.25
LN_EPS = 1e-5
RMS_EPS = 1e-6
D_FF = -(-8 * D_MODEL // (3 * 256)) * 256
HYENA_EMB = 33
HYENA_FILTER_HIDDEN = 64
HYENA_DECAY_TARGET = 1e-2
HYENA_FAST_DECAY_PCT = 0.3
HYENA_SLOW_DECAY_PCT = 1.5
MLA_HEADS = 8
MLA_Q_LORA = 384
MLA_KV_LORA = 256
MLA_NOPE = 128
MLA_ROPE = 64
MLA_V = 128
MLA_SCALE = (MLA_NOPE + MLA_ROPE) ** -0.5
MLA_QK = 256
ROPE_THETA = 10000.0
GDN_HEADS = 8
GDN_DK = 128
GDN_DV = 128
GDN_CHUNK = 128

V7X_LANES = 128
V7X_MXU_DIM = 256
V7X_VMEM_LIMIT_BYTES = 56 * 1024 * 1024
ROW_TILE = 1024


def _params(n_axes):
    return pltpu.CompilerParams(dimension_semantics=("arbitrary",) * n_axes,
                                vmem_limit_bytes=V7X_VMEM_LIMIT_BYTES)


def _mod_row_map(tile, seq, batch):
    return lambda i: (jnp.minimum(i * tile // seq, batch), 0, 0)


def _adaln_kernel(c_ref, w_ref, b_ref, o_ref):
    x = c_ref[...]
    x = (x * jax.nn.sigmoid(x)).astype(BF16)
    o_ref[0] = jnp.dot(x, w_ref[0].astype(BF16), preferred_element_type=F32) + b_ref[0]


def adaln_all(cc, mod_w, mod_b):
    r, d = cc.shape
    depth, _, n = mod_w.shape
    tn = 1536
    return pl.pallas_call(
        _adaln_kernel,
        grid=(depth, n // tn),
        in_specs=[pl.BlockSpec((r, d), lambda l, j: (0, 0)),
                  pl.BlockSpec((1, d, tn), lambda l, j: (l, 0, j)),
                  pl.BlockSpec((1, 1, tn), lambda l, j: (l, 0, j))],
        out_specs=pl.BlockSpec((1, r, tn), lambda l, j: (l, 0, j)),
        out_shape=jax.ShapeDtypeStruct((depth, r, n), F32),
        compiler_params=_params(2), name="adaln_table",
    )(cc, mod_w, mod_b.reshape(depth, 1, n))


def _assemble_kernel(n_lat_tiles, x_ref, c_ref, scale_ref, shift_ref, s_ref, h_ref):
    s = jnp.where(pl.program_id(0) < n_lat_tiles, x_ref[...], c_ref[...])
    s_ref[...] = s
    h_ref[...] = (s * (1.0 + scale_ref[0]) + shift_ref[0]).astype(h_ref.dtype)


def assemble_stream(x2d, ctx2d, scale, shift, seq, batch):
    t_lat, d = x2d.shape
    tm = ROW_TILE
    n_lat, n_ctx = t_lat // tm, ctx2d.shape[0] // tm
    rows = t_lat + ctx2d.shape[0]
    mod_spec = pl.BlockSpec((1, 1, d), _mod_row_map(tm, seq, batch))
    row_spec = pl.BlockSpec((tm, d), lambda i: (i, 0))
    return pl.pallas_call(
        functools.partial(_assemble_kernel, n_lat),
        grid=(n_lat + n_ctx,),
        in_specs=[pl.BlockSpec((tm, d), lambda i: (jnp.minimum(i, n_lat - 1), 0)),
                  pl.BlockSpec((tm, d), lambda i: (jnp.maximum(i - n_lat, 0), 0)), mod_spec, mod_spec],
        out_specs=[row_spec, row_spec],
        out_shape=[jax.ShapeDtypeStruct((rows, d), F32), jax.ShapeDtypeStruct((rows, d), BF16)],
        compiler_params=_params(1), name="assemble_modulate",
    )(x2d, ctx2d, scale, shift)


def _proj_kernel(x_ref, w_ref, o_ref):
    o_ref[...] = jnp.dot(x_ref[...], w_ref[...], preferred_element_type=F32).astype(o_ref.dtype)


def project(x, w, out_dtype=F32, tm=ROW_TILE, tn=None, rows=None):
    m, k = x.shape
    m = m if rows is None else rows
    n = w.shape[1]
    tn = n if tn is None else tn
    return pl.pallas_call(
        _proj_kernel,
        grid=(m // tm, n // tn),
        in_specs=[pl.BlockSpec((tm, k), lambda i, j: (i, 0)),
                  pl.BlockSpec((k, tn), lambda i, j: (0, j))],
        out_specs=pl.BlockSpec((tm, tn), lambda i, j: (i, j)),
        out_shape=jax.ShapeDtypeStruct((m, n), out_dtype),
        compiler_params=_params(2), name="projection",
    )(x, w)


def _out_ln_kernel(x_ref, w_ref, b_ref, s_ref, gate_ref, lng_ref, lnb_ref, scale_ref, shift_ref,
                   s_out_ref, h_out_ref):
    y = jnp.dot(x_ref[...].astype(BF16), w_ref[0], preferred_element_type=F32) + b_ref[...]
    z = DEEPNORM_ALPHA * s_ref[...] + gate_ref[0] * y
    mu = jnp.mean(z, axis=-1, keepdims=True)
    zc = z - mu
    var = jnp.mean(zc * zc, axis=-1, keepdims=True)
    sn = zc * lax.rsqrt(var + LN_EPS) * lng_ref[...] + lnb_ref[...]
    s_out_ref[...] = sn
    h_out_ref[...] = (sn * (1.0 + scale_ref[0]) + shift_ref[0]).astype(h_out_ref.dtype)


def out_ln(x, w, layer, b, s, gate, ln_g, ln_b, scale, shift, rows, seq, batch, tm):
    k = x.shape[1]
    d = w.shape[2]
    mod_spec = pl.BlockSpec((1, 1, d), _mod_row_map(tm, seq, batch))
    vec_spec = pl.BlockSpec((1, d), lambda i: (0, 0))
    row_spec = pl.BlockSpec((tm, d), lambda i: (i, 0))
    return pl.pallas_call(
        _out_ln_kernel,
        grid=(rows // tm,),
        in_specs=[pl.BlockSpec((tm, k), lambda i: (i, 0)),
                  pl.BlockSpec((1, k, d), lambda i: (layer, 0, 0)),
                  vec_spec, row_spec, mod_spec, vec_spec, vec_spec, mod_spec, mod_spec],
        out_specs=[row_spec, row_spec],
        out_shape=[jax.ShapeDtypeStruct((rows, d), F32), jax.ShapeDtypeStruct((rows, d), BF16)],
        compiler_params=_params(1), name="out_proj_ln",
    )(x, w, b.reshape(1, d), s, gate, ln_g.reshape(1, d), ln_b.reshape(1, d), scale, shift)


def _ffn_in_kernel(x_ref, w_ref, a_ref):
    x = x_ref[...]
    dff = a_ref.shape[1]
    for c in range(dff // V7X_MXU_DIM):
        lo = c * V7X_MXU_DIM
        g = jnp.dot(x, w_ref[0, :, lo:lo + V7X_MXU_DIM], preferred_element_type=F32)
        u = jnp.dot(x, w_ref[0, :, dff + lo:dff + lo + V7X_MXU_DIM], preferred_element_type=F32)
        a_ref[:, lo:lo + V7X_MXU_DIM] = (g * jax.nn.sigmoid(g) * u).astype(a_ref.dtype)


def ffn_in(h, w_in, layer, rows):
    k = h.shape[1]
    dff = w_in.shape[2] // 2
    tm = ROW_TILE
    return pl.pallas_call(
        _ffn_in_kernel,
        grid=(rows // tm,),
        in_specs=[pl.BlockSpec((tm, k), lambda i: (i, 0)),
                  pl.BlockSpec((1, k, 2 * dff), lambda i: (layer, 0, 0))],
        out_specs=pl.BlockSpec((tm, dff), lambda i: (i, 0)),
        out_shape=jax.ShapeDtypeStruct((rows, dff), BF16),
        compiler_params=_params(1), name="ffn_swiglu_in",
    )(h, w_in)


V7X_SUBLANES = 8


def _centred_conv3(z, bias, cw, const):
    length = z.shape[0]
    c0, c1, c2 = cw[0:1], cw[1:2], cw[2:3]
    out = pltpu.roll(z, 1, 0) * c0 + z * c1 + pltpu.roll(z, length - 1, 0) * c2
    wrapped_last, wrapped_first = z[length - 1:length], z[0:1]
    if bias is not None:
        out = out + (const + bias * (c0 + c1 + c2))
        wrapped_last, wrapped_first = wrapped_last + bias, wrapped_first + bias
    e = V7X_SUBLANES
    r = lax.broadcasted_iota(jnp.int32, (e, 1), 0)
    top = jnp.where(r == 0, out[0:e] - wrapped_last * c0, out[0:e])
    bot = jnp.where(r == e - 1, out[length - e:] - wrapped_first * c2, out[length - e:])
    return jnp.concatenate([top, out[e:length - e], bot], axis=0)


def _hyena_in_kernel(x_ref, w0_ref, w1_ref, w2_ref, b0_ref, b1_ref, b2_ref, cw0_ref, cw1_ref, cw2_ref,
                     cb0_ref, cb1_ref, cb2_ref, x0_ref, u_ref):
    x = x_ref[...]

    def branch(w_ref, b_ref, cw_ref, cb_ref):
        z = jnp.dot(x, w_ref[...], preferred_element_type=F32)
        return _centred_conv3(z, b_ref[...], cw_ref[...], cb_ref[...])

    x0 = branch(w0_ref, b0_ref, cw0_ref, cb0_ref)
    x1 = branch(w1_ref, b1_ref, cw1_ref, cb1_ref)
    v = branch(w2_ref, b2_ref, cw2_ref, cb2_ref)
    x0_ref[...] = x0.astype(x0_ref.dtype)
    u_ref[...] = (x1 * v).astype(u_ref.dtype)


def hyena_in(h, w_in, b_in, conv_w, conv_b, length, n_seq, row_offset):
    k = h.shape[1]
    d = w_in.shape[1] // 3
    tn = 256
    nb = d // tn
    off = row_offset // length
    w_specs = [pl.BlockSpec((k, tn), lambda b, j, q=q: (0, j + q * nb)) for q in range(3)]
    v_specs = [pl.BlockSpec((1, tn), lambda b, j, q=q: (0, j + q * nb)) for q in range(3)]
    cw_specs = [pl.BlockSpec((3, tn), lambda b, j, q=q: (0, j + q * nb)) for q in range(3)]
    out_spec = pl.BlockSpec((length, tn), lambda b, j: (b, j))
    b2, cb2 = b_in.reshape(1, 3 * d), conv_b.reshape(1, 3 * d)
    return pl.pallas_call(
        _hyena_in_kernel,
        grid=(n_seq, nb),
        in_specs=[pl.BlockSpec((length, k), lambda b, j: (b + off, 0))] + w_specs + v_specs + cw_specs + v_specs,
        out_specs=[out_spec, out_spec],
        out_shape=[jax.ShapeDtypeStruct((n_seq * length, d), BF16)] * 2,
        compiler_params=_params(2), name="hyena_in_proj_conv3",
    )(h, w_in, w_in, w_in, b2, b2, b2, conv_w, conv_w, conv_w, cb2, cb2, cb2)


def _hyena_filter_kernel(feat_ref, w1_ref, b1_ref, w2_ref, b2_ref, w3_ref, b3_ref, w4_ref, freq_ref, win_ref, k_ref):
    freq = freq_ref[...]
    z = jnp.sin(freq * (jnp.dot(feat_ref[...], w1_ref[...], precision=HIGHEST, preferred_element_type=F32) + b1_ref[...]))
    z = jnp.sin(freq * (jnp.dot(z, w2_ref[...], precision=HIGHEST, preferred_element_type=F32) + b2_ref[...]))
    z = jnp.sin(freq * (jnp.dot(z, w3_ref[...], precision=HIGHEST, preferred_element_type=F32) + b3_ref[...]))
    taps = jnp.dot(z, w4_ref[...], precision=HIGHEST, preferred_element_type=F32)
    d = win_ref.shape[1]
    win = win_ref[...]
    k_fwd = taps[:, :d] * win
    k_bwd = taps[:, d:] * win
    tile = k_fwd.shape[0]
    first = (lax.broadcasted_iota(jnp.int32, (tile, 1), 0) + pl.program_id(0) * tile) == 0
    k_ref[:, :d] = jnp.where(first, k_fwd + k_bwd, k_fwd).astype(k_ref.dtype)
    k_ref[:, d:] = jnp.where(first, 0.0, k_bwd).astype(k_ref.dtype)


def _hyena_features(length):
    t01 = np.linspace(0.0, 1.0, length, dtype=np.float32)[:, None]
    bands = (HYENA_EMB - 1) // 2
    w = (np.float32(2.0 * math.pi / length) * np.arange(length, dtype=np.float32)).astype(np.float32)
    f = np.linspace(1e-4, bands - 1, bands, dtype=np.float32)
    ang = (w[:, None] * f[None, :]).astype(np.float32)
    feat = np.concatenate([t01, np.cos(ang), -np.sin(ang)], axis=-1).astype(np.float32)
    return np.pad(feat, ((0, 0), (0, V7X_LANES - HYENA_EMB)))


def _hyena_window(length):
    max_decay = math.log(HYENA_DECAY_TARGET) / HYENA_FAST_DECAY_PCT
    min_decay = math.log(HYENA_DECAY_TARGET) / HYENA_SLOW_DECAY_PCT
    deltas = np.abs(np.linspace(min_decay, max_decay, D_MODEL, dtype=np.float32))
    t = np.linspace(0.0, 1.0, length, dtype=np.float32)
    return np.exp(-t[:, None] * deltas[None, :]).astype(np.float32)


def hyena_filter_taps(length, p):
    hid = V7X_LANES
    pad_h = hid - HYENA_FILTER_HIDDEN
    padc = lambda a: jnp.pad(a, ((0, 0), (0, pad_h)))
    w1 = jnp.pad(p["fw1"], ((0, V7X_LANES - HYENA_EMB), (0, pad_h)))
    w2 = jnp.pad(p["fw2"], ((0, pad_h), (0, pad_h)))
    w3 = jnp.pad(p["fw3"], ((0, pad_h), (0, pad_h)))
    w4 = jnp.pad(p["fw4"], ((0, pad_h), (0, 0)))
    vec = lambda a: padc(a.reshape(1, -1))
    args = (jnp.asarray(_hyena_features(length)), w1, vec(p["fb1"]), w2, vec(p["fb2"]), w3, vec(p["fb3"]), w4,
            vec(p["freq"]), jnp.asarray(_hyena_window(length)))
    tl = min(256, length)
    full = lambda a: pl.BlockSpec(a.shape, lambda i: (0, 0))
    return pl.pallas_call(
        _hyena_filter_kernel,
        grid=(length // tl,),
        in_specs=[pl.BlockSpec((tl, hid), lambda i: (i, 0))] + [full(a) for a in args[1:9]]
                 + [pl.BlockSpec((tl, D_MODEL), lambda i: (i, 0))],
        out_specs=pl.BlockSpec((tl, 2 * D_MODEL), lambda i: (i, 0)),
        out_shape=jax.ShapeDtypeStruct((length, 2 * D_MODEL), BF16),
        compiler_params=_params(1), name="hyena_filter",
    )(*args)


@functools.lru_cache(maxsize=None)
def _negacyclic_dft(length):
    n = 2 * length
    k = np.arange(length, dtype=np.int64)
    turns = ((2 * k[:, None] + 1) * k[None, :]) % (2 * n)
    ang = turns.astype(np.float64) * (2.0 * math.pi / (2 * n))
    c, s = np.cos(ang), np.sin(ang)
    inv = 2.0 / n
    return c, s, (c.T * inv), (s.T * inv)


def _dft_spec_kernel(c_ref, s_ref, u_ref, p_ref):
    u = u_ref[...]
    p_ref[0] = jnp.dot(c_ref[...], u, preferred_element_type=F32)
    p_ref[1] = jnp.dot(s_ref[...], u, preferred_element_type=F32)


def _dft_fwd_kernel(c_ref, s_ref, u_ref, cf_ref, cb_ref, sf_ref, sb_ref, y_ref):
    u = u_ref[...]
    ur = jnp.dot(c_ref[...], u, preferred_element_type=F32)
    us = jnp.dot(s_ref[...], u, preferred_element_type=F32)
    hr = cf_ref[0] + cb_ref[0]
    hi = sb_ref[0] - sf_ref[0]
    y_ref[0, 0] = (ur * hr + us * hi).astype(y_ref.dtype)
    y_ref[0, 1] = (us * hr - ur * hi).astype(y_ref.dtype)


def _dft_inv_kernel(ct_ref, st_ref, y_ref, u_ref, x0_ref, skip_ref, *rest):
    o_ref = rest[-1]
    y = (jnp.dot(ct_ref[...], y_ref[0, 0], preferred_element_type=F32)
         + jnp.dot(st_ref[...], y_ref[0, 1], preferred_element_type=F32))
    u = u_ref[...].astype(F32)
    o_ref[...] = ((y + u * skip_ref[...]) * x0_ref[...].astype(F32)).astype(o_ref.dtype)


def _into_buffer(into, n_inputs):
    if into is None:
        return [], (), {}
    return [pl.BlockSpec(memory_space=pl.ANY)], (into,), {n_inputs: 0}


def hyena_long_conv(x0, u, taps, skip, length, n_seq, out_rows, row_offset, into):
    d = u.shape[1]
    c, s, ct, st = (jnp.asarray(a, dtype=BF16) for a in _negacyclic_dft(length))
    tf = min(512, length)
    nf = length // tf
    spec = pl.pallas_call(
        _dft_spec_kernel,
        grid=(nf,),
        in_specs=[pl.BlockSpec((tf, length), lambda i: (i, 0)),
                  pl.BlockSpec((tf, length), lambda i: (i, 0)),
                  pl.BlockSpec((length, 2 * d), lambda i: (0, 0))],
        out_specs=pl.BlockSpec((2, tf, 2 * d), lambda i: (0, i, 0)),
        out_shape=jax.ShapeDtypeStruct((2, length, 2 * d), F32),
        compiler_params=_params(1), name="hyena_filter_spectrum",
    )(c, s, taps)
    spec_block = lambda part, half: pl.BlockSpec((1, tf, d), lambda b, i: (part, i, half))
    y = pl.pallas_call(
        _dft_fwd_kernel,
        grid=(n_seq, nf),
        in_specs=[pl.BlockSpec((tf, length), lambda b, i: (i, 0)),
                  pl.BlockSpec((tf, length), lambda b, i: (i, 0)),
                  pl.BlockSpec((length, d), lambda b, i: (b, 0)),
                  spec_block(0, 0), spec_block(0, 1), spec_block(1, 0), spec_block(1, 1)],
        out_specs=pl.BlockSpec((1, 2, tf, d), lambda b, i: (b, 0, i, 0)),
        out_shape=jax.ShapeDtypeStruct((n_seq, 2, length, d), BF16),
        compiler_params=_params(2), name="hyena_dft_forward",
    )(c, s, u, spec, spec, spec, spec)
    into_specs, into_args, aliases = _into_buffer(into, 6)
    off = row_offset // tf
    return pl.pallas_call(
        _dft_inv_kernel,
        grid=(n_seq, nf),
        in_specs=[pl.BlockSpec((tf, length), lambda b, i: (i, 0)),
                  pl.BlockSpec((tf, length), lambda b, i: (i, 0)),
                  pl.BlockSpec((1, 2, length, d), lambda b, i: (b, 0, 0, 0)),
                  pl.BlockSpec((tf, d), lambda b, i: (b * nf + i, 0)),
                  pl.BlockSpec((tf, d), lambda b, i: (b * nf + i, 0)),
                  pl.BlockSpec((1, d), lambda b, i: (0, 0))] + into_specs,
        out_specs=pl.BlockSpec((tf, d), lambda b, i: (off + b * nf + i, 0)),
        out_shape=jax.ShapeDtypeStruct((out_rows, d), BF16),
        input_output_aliases=aliases,
        compiler_params=_params(2), name="hyena_dft_inverse",
    )(ct, st, y, u, x0, skip.reshape(1, d), *into_args)


def hyena_mixer_core(h, p, length, n_seq, row_offset, out_rows, into=None):
    x0, u = hyena_in(h, p["w_in"], p["b_in"], p["conv_w"], p["conv_b"], length, n_seq, row_offset)
    taps = hyena_filter_taps(length, p)
    return hyena_long_conv(x0, u, taps, p["skip"], length, n_seq, out_rows, row_offset, into)


def _rope_swap_columns():
    half, quarter = MLA_ROPE // 2, MLA_ROPE // 4
    r = np.arange(MLA_ROPE)
    within = r % half
    src = np.where(within < quarter, r + quarter, r - quarter)
    sign = np.where(within < quarter, -1.0, 1.0).astype(np.float32)
    return src, sign


@functools.lru_cache(maxsize=None)
def _rope_tables(seq, ctx_rows):
    half, quarter = MLA_ROPE // 2, MLA_ROPE // 4
    inv_freq = (np.float32(ROPE_THETA) ** (-np.arange(quarter, dtype=np.float32) / quarter)).astype(np.float32)
    t = np.arange(seq)
    row, col = (t // GRID_W).astype(np.float32), (t % GRID_W).astype(np.float32)
    r = np.arange(MLA_ROPE)
    pos = np.where((r // half)[None, :] == 0, row[:, None], col[:, None]).astype(np.float32)
    ang = (pos * inv_freq[r % quarter][None, :]).astype(np.float32).astype(np.float64)
    cos = np.concatenate([np.cos(ang), np.ones((ctx_rows, MLA_ROPE))], axis=0)
    sin = np.concatenate([np.sin(ang), np.zeros((ctx_rows, MLA_ROPE))], axis=0)
    ones = np.ones((seq + ctx_rows, MLA_NOPE))
    tab_q = (np.concatenate([ones, cos, sin], axis=1) * MLA_SCALE).astype(np.float32)
    tab_k = np.concatenate([cos, sin], axis=1).astype(np.float32)
    return tab_q, tab_k


def _rms_in_kernel(x_ref, g_ref):
    x = x_ref[...]
    return (x * lax.rsqrt(jnp.mean(x * x, axis=-1, keepdims=True) + RMS_EPS) * g_ref[...]).astype(BF16)


def _mla_q_kernel(x_ref, g_ref, w_ref, tab_ref, q_ref):
    acc = jnp.dot(_rms_in_kernel(x_ref, g_ref), w_ref[...], preferred_element_type=F32)
    tab = tab_ref[...]
    for h in range(MLA_HEADS):
        q_ref[:, h * MLA_QK:(h + 1) * MLA_QK] = (acc[:, h * MLA_QK:(h + 1) * MLA_QK] * tab).astype(q_ref.dtype)


def _mla_kv_kernel(x_ref, g_ref, w_ref, rope_ref, tab_ref, k_ref, v_ref):
    acc = jnp.dot(_rms_in_kernel(x_ref, g_ref), w_ref[...], preferred_element_type=F32)
    prod = rope_ref[...] * tab_ref[...]
    k_rot = (prod + pltpu.roll(prod, MLA_ROPE, 1)).astype(k_ref.dtype)
    hw = MLA_NOPE + MLA_V
    for h in range(MLA_HEADS):
        k_ref[:, h * MLA_QK:h * MLA_QK + MLA_NOPE] = acc[:, h * hw:h * hw + MLA_NOPE].astype(k_ref.dtype)
        k_ref[:, h * MLA_QK + MLA_NOPE:(h + 1) * MLA_QK] = k_rot
        v_ref[:, h * MLA_V:(h + 1) * MLA_V] = acc[:, h * hw + MLA_NOPE:(h + 1) * hw].astype(v_ref.dtype)


def mla_qkv(h, p, rows, seq, batch):
    tm = ROW_TILE
    src, sign = _rope_swap_columns()
    w_in = p["w_in"]
    o_kv, o_rope = MLA_Q_LORA, MLA_Q_LORA + MLA_KV_LORA
    w_rope = w_in[:, o_rope:]
    w_cat = jnp.concatenate([w_in[:, o_kv:o_rope], w_rope, w_rope[:, src] * sign, w_in[:, :o_kv]], axis=1).astype(BF16)
    z = project(h[:rows], w_cat)
    w_uq = p["w_uq"].reshape(MLA_Q_LORA, MLA_HEADS, MLA_NOPE + MLA_ROPE)
    w_uq_rope = w_uq[:, :, MLA_NOPE:]
    w_uq = jnp.concatenate([w_uq, w_uq_rope[:, :, src] * sign], axis=2).reshape(MLA_Q_LORA, MLA_HEADS * MLA_QK).astype(BF16)
    tab_q, tab_k = _rope_tables(seq, tm)
    lat_tiles = batch * seq // tm
    tab_map = lambda i: (jnp.where(i < lat_tiles, i % (seq // tm), seq // tm), 0)
    n_tiles = rows // tm
    q = pl.pallas_call(
        _mla_q_kernel,
        grid=(n_tiles,),
        in_specs=[pl.BlockSpec((tm, MLA_Q_LORA), lambda i: (i, 1)),
                  pl.BlockSpec((1, MLA_Q_LORA), lambda i: (0, 0)),
                  pl.BlockSpec((MLA_Q_LORA, MLA_HEADS * MLA_QK), lambda i: (0, 0)),
                  pl.BlockSpec((tm, MLA_QK), tab_map)],
        out_specs=pl.BlockSpec((tm, MLA_HEADS * MLA_QK), lambda i: (i, 0)),
        out_shape=jax.ShapeDtypeStruct((rows, MLA_HEADS * MLA_QK), BF16),
        compiler_params=_params(1), name="mla_q_proj",
    )(z, p["q_norm"].reshape(1, -1), w_uq, jnp.asarray(tab_q))
    k, v = pl.pallas_call(
        _mla_kv_kernel,
        grid=(n_tiles,),
        in_specs=[pl.BlockSpec((tm, MLA_KV_LORA), lambda i: (i, 0)),
                  pl.BlockSpec((1, MLA_KV_LORA), lambda i: (0, 0)),
                  pl.BlockSpec((MLA_KV_LORA, MLA_HEADS * (MLA_NOPE + MLA_V)), lambda i: (0, 0)),
                  pl.BlockSpec((tm, 2 * MLA_ROPE), lambda i: (i, 2)),
                  pl.BlockSpec((tm, 2 * MLA_ROPE), tab_map)],
        out_specs=[pl.BlockSpec((tm, MLA_HEADS * MLA_QK), lambda i: (i, 0)),
                   pl.BlockSpec((tm, MLA_HEADS * MLA_V), lambda i: (i, 0))],
        out_shape=[jax.ShapeDtypeStruct((rows, MLA_HEADS * MLA_QK), BF16),
                   jax.ShapeDtypeStruct((rows, MLA_HEADS * MLA_V), BF16)],
        compiler_params=_params(1), name="mla_kv_proj",
    )(z, p["kv_norm"].reshape(1, -1), p["w_ukv"].astype(BF16), z, jnp.asarray(tab_k))
    return q, k, v


ATTN_SUB = 256


def _dot_nt(a, b):
    return lax.dot_general(a, b, (((1,), (1,)), ((), ())), preferred_element_type=F32)


def _attention_body(q_ref, key_refs, val_refs, o_ref):
    n_sub = q_ref.shape[0] // ATTN_SUB
    rows = lambda i: slice(i * ATTN_SUB, (i + 1) * ATTN_SUB)
    scores = lambda i: [_dot_nt(q_ref[rows(i), :], k_ref[...]) for k_ref in key_refs]
    pending = scores(0)
    for i in range(n_sub):
        s = pending
        if i + 1 < n_sub:
            pending = scores(i + 1)
        m = functools.reduce(jnp.maximum, [jnp.max(x, axis=-1, keepdims=True) for x in s])
        p = [jnp.exp(x - m) for x in s]
        l = sum(jnp.sum(x, axis=-1, keepdims=True) for x in p)
        o = sum(jnp.dot(x.astype(BF16), v_ref[...], preferred_element_type=F32) for x, v_ref in zip(p, val_refs))
        o_ref[rows(i), :] = (o / l).astype(o_ref.dtype)


def _attn_lat_kernel(q_ref, kl_ref, vl_ref, kc_ref, vc_ref, o_ref):
    _attention_body(q_ref, (kl_ref, kc_ref), (vl_ref, vc_ref), o_ref)


def _attn_ctx_kernel(q_ref, kc_ref, vc_ref, into_ref, o_ref):
    del into_ref
    _attention_body(q_ref, (kc_ref,), (vc_ref,), o_ref)


def mla_attention(q, k, v, batch, seq, ctx_len, with_ctx_out):
    tq = seq
    nq = seq // tq
    ctx_blk = batch * seq // ctx_len
    out_rows = batch * (seq + ctx_len) if with_ctx_out else batch * seq
    o_lat = pl.pallas_call(
        _attn_lat_kernel,
        grid=(batch, MLA_HEADS, nq),
        in_specs=[pl.BlockSpec((tq, MLA_QK), lambda b, h, i: (b * nq + i, h)),
                  pl.BlockSpec((seq, MLA_QK), lambda b, h, i: (b, h)),
                  pl.BlockSpec((seq, MLA_V), lambda b, h, i: (b, h)),
                  pl.BlockSpec((ctx_len, MLA_QK), lambda b, h, i: (ctx_blk + b, h)),
                  pl.BlockSpec((ctx_len, MLA_V), lambda b, h, i: (ctx_blk + b, h))],
        out_specs=pl.BlockSpec((tq, MLA_V), lambda b, h, i: (b * nq + i, h)),
        out_shape=jax.ShapeDtypeStruct((out_rows, MLA_HEADS * MLA_V), BF16),
        compiler_params=_params(3), name="mla_attention_latent",
    )(q, k, v, k, v)
    if not with_ctx_out:
        return o_lat
    into_specs, into_args, aliases = _into_buffer(o_lat, 3)
    return pl.pallas_call(
        _attn_ctx_kernel,
        grid=(batch, MLA_HEADS),
        in_specs=[pl.BlockSpec((ctx_len, MLA_QK), lambda b, h: (ctx_blk + b, h)),
                  pl.BlockSpec((ctx_len, MLA_QK), lambda b, h: (ctx_blk + b, h)),
                  pl.BlockSpec((ctx_len, MLA_V), lambda b, h: (ctx_blk + b, h))] + into_specs,
        out_specs=pl.BlockSpec((ctx_len, MLA_V), lambda b, h: (ctx_blk + b, h)),
        out_shape=jax.ShapeDtypeStruct((out_rows, MLA_HEADS * MLA_V), BF16),
        input_output_aliases=aliases,
        compiler_params=_params(2), name="mla_attention_context",
    )(q, k, v, *into_args)


GDN_QKV = 2 * GDN_HEADS * GDN_DK + GDN_HEADS * GDN_DV
GDN_SCALARS = 2 * GDN_HEADS
GDN_ROWS = 16
LOG2_CHUNK = int(math.log2(GDN_CHUNK))
NEG_BIG = -1e30
GDN_GROUP = 128
GDN_HEADS_PER_STEP = 2
GDN_GROUPS_PER_STEP = 4


def _gdn_in_kernel(x_ref, w_ref, cw_ref, o_ref):
    z = jnp.dot(x_ref[...], w_ref[...], preferred_element_type=F32)
    z = _centred_conv3(z, None, cw_ref[...], None)
    z = z * jax.nn.sigmoid(z)
    j = pl.program_id(1)
    blocks_per_kind = GDN_HEADS * GDN_DK // z.shape[1]
    q_scale = jnp.where(j < blocks_per_kind, GDN_DK ** -0.5, 1.0)
    for hh in range(z.shape[1] // GDN_DK):
        zh = z[:, hh * GDN_DK:(hh + 1) * GDN_DK]
        zn = zh * lax.rsqrt(jnp.sum(zh * zh, axis=-1, keepdims=True) + RMS_EPS) * q_scale
        o_ref[:, hh * GDN_DK:(hh + 1) * GDN_DK] = jnp.where(j < 2 * blocks_per_kind, zn, zh).astype(o_ref.dtype)


def gdn_in(h, w_qkv, conv_w, length, n_seq, row_offset):
    k = h.shape[1]
    n = w_qkv.shape[1]
    tn = 256
    off = row_offset // length
    return pl.pallas_call(
        _gdn_in_kernel,
        grid=(n_seq, n // tn),
        in_specs=[pl.BlockSpec((length, k), lambda b, j: (b + off, 0)),
                  pl.BlockSpec((k, tn), lambda b, j: (0, j)),
                  pl.BlockSpec((3, tn), lambda b, j: (0, j))],
        out_specs=pl.BlockSpec((length, tn), lambda b, j: (b, j)),
        out_shape=jax.ShapeDtypeStruct((n_seq * length, n), BF16),
        compiler_params=_params(2), name="gdn_in_proj_conv3",
    )(h, w_qkv, conv_w)


def _gdn_gates_kernel(ab_ref, alog_ref, dtb_ref, beta_ref, ecum_ref, edec_ref, cum_ref, egl_ref):
    a = ab_ref[:, :V7X_LANES]
    b = ab_ref[:, V7X_LANES:]
    x = a + dtb_ref[...]
    softplus = jnp.maximum(x, 0.0) + jnp.log(1.0 + jnp.exp(-jnp.abs(x)))
    g = -jnp.exp(alog_ref[...]) * softplus
    tm = g.shape[0]
    r = lax.broadcasted_iota(jnp.int32, (tm, tm), 0)
    c = lax.broadcasted_iota(jnp.int32, (tm, tm), 1)
    same = (r >> LOG2_CHUNK) == (c >> LOG2_CHUNK)
    prefix = jnp.where(same & (c <= r), 1.0, 0.0)
    suffix = jnp.where(same & (c >= r), 1.0, 0.0)
    total = jnp.where(same, 1.0, 0.0)
    dot = lambda m: jnp.dot(m, g, precision=HIGHEST, preferred_element_type=F32)
    lane = lax.broadcasted_iota(jnp.int32, g.shape, 1)
    cum = jnp.where(lane < GDN_HEADS, dot(prefix), dot(suffix))
    g_last = dot(total)
    beta_ref[...] = jax.nn.sigmoid(b)
    ecum_ref[...] = jnp.exp(cum)
    edec_ref[...] = jnp.exp(g_last - cum)
    cum_ref[...] = cum
    egl_ref[...] = jnp.exp(g_last)


def gdn_gates(ab, a_log, dt_bias):
    rows = ab.shape[0]
    tm = 256
    pad = lambda v: jnp.pad(v.reshape(1, -1), ((0, 0), (0, V7X_LANES - GDN_SCALARS)))
    spec = pl.BlockSpec((tm, V7X_LANES), lambda i: (i, 0))
    vec = pl.BlockSpec((1, V7X_LANES), lambda i: (0, 0))
    return pl.pallas_call(
        _gdn_gates_kernel,
        grid=(rows // tm,),
        in_specs=[pl.BlockSpec((tm, 2 * V7X_LANES), lambda i: (i, 0)), vec, vec],
        out_specs=[spec] * 5,
        out_shape=[jax.ShapeDtypeStruct((rows, V7X_LANES), F32)] * 5,
        compiler_params=_params(1), name="gdn_gates",
    )(ab, pad(a_log), pad(dt_bias))


def _bdot(a, b):
    return jnp.dot(a.astype(BF16), b.astype(BF16), preferred_element_type=F32)


def _bdot_nt(a, b):
    return lax.dot_general(a.astype(BF16), b.astype(BF16), (((1,), (1,)), ((), ())), preferred_element_type=F32)


def _bdot_tn(a, b):
    return lax.dot_general(a.astype(BF16), b.astype(BF16), (((0,), (0,)), ((), ())), preferred_element_type=F32)


def _gdn_group_terms(qkv_ref, sc_ref, problems, want_out):
    ri = lax.broadcasted_iota(jnp.int32, (GDN_GROUP, GDN_GROUP), 0)
    ci = lax.broadcasted_iota(jnp.int32, (GDN_GROUP, GDN_GROUP), 1)
    same = (ri >> LOG2_CHUNK) == (ci >> LOG2_CHUNK)
    incl = [same & (ri >= ci), same & (ri <= ci)]
    strict = [same & (ri > ci), same & (ri < ci)]
    eye = jnp.where(ri == ci, 1.0, 0.0)
    pair = (ri >> 1) == (ci >> 1)
    joins = [((ri >> (lg + 1)) == (ci >> (lg + 1))) & ((ri >> lg) != (ci >> lg)) for lg in range(1, LOG2_CHUNK)]

    scalars = {}

    def load(hh, gi, d):
        rows = pl.ds(pl.multiple_of(gi * GDN_GROUP, GDN_GROUP), GDN_GROUP)
        lanes = slice(hh * GDN_DK, (hh + 1) * GDN_DK)
        if (hh, id(gi)) not in scalars:
            sc = sc_ref[hh, :, rows]
            square = jnp.concatenate([sc, jnp.zeros((GDN_GROUP - GDN_ROWS, GDN_GROUP), F32)], axis=0)
            scalars[(hh, id(gi))] = (sc, square.T)
        sc, sc_t = scalars[(hh, id(gi))]
        base = 5 * d
        return dict(
            q=qkv_ref[0][rows, lanes].astype(F32), k=qkv_ref[1][rows, lanes].astype(F32),
            v=qkv_ref[2][rows, lanes].astype(F32),
            beta=sc_t[:, base + 0:base + 1], ecum=sc_t[:, base + 1:base + 2],
            edec=sc_t[:, base + 2:base + 3], cum_c=sc_t[:, base + 3:base + 4],
            cum_r=sc[base + 3:base + 4, :], d=d)

    ps = [load(hh, gi, d) for hh, gi, d in problems]
    for p in ps:
        p["gamma"] = jnp.exp(jnp.where(incl[p["d"]], p["cum_c"] - p["cum_r"], NEG_BIG))
        p["kb"] = p["k"] * p["beta"]
    for p in ps:
        p["a"] = jnp.where(strict[p["d"]], _bdot_nt(p["kb"], p["k"]) * p["gamma"], 0.0)
    for p in ps:
        p["inv"] = eye - jnp.where(pair, p["a"], 0.0)
    for join in joins:
        for p in ps:
            p["t"] = _bdot(jnp.where(join, p["a"], 0.0), p["inv"])
        for p in ps:
            p["inv"] = p["inv"] - _bdot(p["inv"], p["t"])
    for p in ps:
        p["uw"] = _bdot(p["inv"], jnp.concatenate([p["v"] * p["beta"], p["kb"] * p["ecum"]], axis=-1))
        p["kdec"] = p["k"] * p["edec"]
    if want_out:
        for p in ps:
            p["qk"] = jnp.where(incl[p["d"]], _bdot_nt(p["q"], p["k"]) * p["gamma"], 0.0)
    out = []
    for p in ps:
        kw_t, n_t = [], []
        for c in range(GDN_GROUP // GDN_CHUNK):
            sl = slice(c * GDN_CHUNK, (c + 1) * GDN_CHUNK)
            both = _bdot_tn(p["uw"][sl], p["kdec"][sl])
            n_t.append(both[:GDN_DV])
            kw_t.append(both[GDN_DV:])
        q_eff = o_local = None
        if want_out:
            corr = _bdot(p["qk"], p["uw"])
            q_eff, o_local = p["q"] * p["ecum"] - corr[:, GDN_DV:], corr[:, :GDN_DV]
        out.append((kw_t, n_t, q_eff, o_local))
    return out


def _gdn_scan_kernel(ql_ref, kl_ref, vl_ref, qc_ref, kc_ref, vc_ref, scl_ref, scc_ref,
                     gate_ref, onorm_ref, y_ref, kw_ref, nt_ref, qeff_ref, oloc_ref, out_ref):
    n_ctx = qc_ref.shape[0] // GDN_CHUNK
    n_lat = ql_ref.shape[0] // GDN_CHUNK
    per_group = GDN_GROUP // GDN_CHUNK
    heads = range(GDN_HEADS_PER_STEP)
    chains = [(hh, d) for hh in heads for d in range(2)]

    def precompute(qkv, sc_ref, slot0, want_out):
        groups = min(GDN_GROUPS_PER_STEP, qkv[0].shape[0] // GDN_GROUP)

        def body(it, carry):
            gis = [it * groups + gg for gg in range(groups)]
            problems = [(hh, gi, d) for hh in heads for gi in gis for d in range(2)]
            terms = _gdn_group_terms(qkv, sc_ref, problems, want_out)
            for (hh, gi, d), (kw_t, n_t, q_eff, o_local) in zip(problems, terms):
                for c in range(per_group):
                    kw_ref[hh, d, slot0 + gi * per_group + c] = kw_t[c].astype(kw_ref.dtype)
                    nt_ref[hh, d, slot0 + gi * per_group + c] = n_t[c]
                if want_out:
                    rows = pl.ds(pl.multiple_of(gi * GDN_GROUP, GDN_GROUP), GDN_GROUP)
                    qeff_ref[hh, d, rows, :] = q_eff.astype(qeff_ref.dtype)
                    oloc_ref[hh, d, rows, :] = o_local
            return carry
        lax.fori_loop(0, qkv[0].shape[0] // (GDN_GROUP * groups), body, 0)

    precompute((qc_ref, kc_ref, vc_ref), scc_ref, 0, False)
    precompute((ql_ref, kl_ref, vl_ref), scl_ref, n_ctx, True)

    def chunk_rows(c):
        return pl.ds(pl.multiple_of(c * GDN_CHUNK, GDN_CHUNK), GDN_CHUNK)

    def advance(states, sc_ref, slot0, chunk_of):
        prods = [jnp.dot(st.astype(BF16), kw_ref[hh, d, slot0 + chunk_of[d]], preferred_element_type=F32)
                 for st, (hh, d) in zip(states, chains)]
        new = []
        for st, pr, (hh, d) in zip(states, prods, chains):
            c = chunk_of[d]
            egl = sc_ref[hh, 5 * d + 4:5 * d + 5, chunk_rows(c)][:, 0:1]
            new.append(egl * st - pr + nt_ref[hh, d, slot0 + c])
        return tuple(new)

    def scan_ctx(step, states):
        return advance(states, scc_ref, 0, (step, n_ctx - 1 - step))

    def scan_lat(step, states):
        chunk_of = (step, n_lat - 1 - step)
        outs = [_bdot_nt(qeff_ref[hh, d, chunk_rows(chunk_of[d]), :], st) for st, (hh, d) in zip(states, chains)]
        for o, (hh, d) in zip(outs, chains):
            rows = chunk_rows(chunk_of[d])
            out_ref[hh, d, rows, :] = o + oloc_ref[hh, d, rows, :]
        return advance(states, scl_ref, n_ctx, chunk_of)

    zero = jnp.zeros((GDN_DV, GDN_DK), F32)
    states = lax.fori_loop(0, n_ctx, scan_ctx, tuple(zero for _ in chains))
    lax.fori_loop(0, n_lat, scan_lat, states)

    onorm = onorm_ref[...]

    def finish(gi, carry):
        rows = pl.ds(pl.multiple_of(gi * GDN_GROUP, GDN_GROUP), GDN_GROUP)
        for hh in heads:
            lanes = slice(hh * GDN_DV, (hh + 1) * GDN_DV)
            o = out_ref[hh, 0, rows, :] + out_ref[hh, 1, rows, :]
            o = o * lax.rsqrt(jnp.mean(o * o, axis=-1, keepdims=True) + RMS_EPS) * onorm
            gte = gate_ref[rows, lanes].astype(F32)
            y_ref[rows, lanes] = (o * gte * jax.nn.sigmoid(gte)).astype(y_ref.dtype)
        return carry

    lax.fori_loop(0, ql_ref.shape[0] // GDN_GROUP, finish, 0)


def gdn_scan(qkv_l, qkv_c, scalars, gate, o_norm, batch, seq, ctx_len):
    n_lat, n_ctx = seq // GDN_CHUNK, ctx_len // GDN_CHUNK
    ctx_blk = batch * seq // ctx_len
    hp = GDN_HEADS_PER_STEP
    n_hp = GDN_HEADS // hp
    lat = lambda q: pl.BlockSpec((seq, hp * GDN_DK), lambda b, h, q=q: (b, q * n_hp + h))
    cx = lambda q: pl.BlockSpec((ctx_len, hp * GDN_DK), lambda b, h, q=q: (b, q * n_hp + h))
    return pl.pallas_call(
        _gdn_scan_kernel,
        grid=(batch, n_hp),
        in_specs=[lat(0), lat(1), lat(2), cx(0), cx(1), cx(2),
                  pl.BlockSpec((hp, GDN_ROWS, seq), lambda b, h: (h, 0, b)),
                  pl.BlockSpec((hp, GDN_ROWS, ctx_len), lambda b, h: (h, 0, ctx_blk + b)),
                  pl.BlockSpec((seq, hp * GDN_DV), lambda b, h: (b, h)),
                  pl.BlockSpec((1, GDN_DV), lambda b, h: (0, 0))],
        out_specs=pl.BlockSpec((seq, hp * GDN_DV), lambda b, h: (b, h)),
        out_shape=jax.ShapeDtypeStruct((batch * seq, GDN_HEADS * GDN_DV), BF16),
        scratch_shapes=[pltpu.VMEM((hp, 2, n_ctx + n_lat, GDN_DK, GDN_DK), BF16),
                        pltpu.VMEM((hp, 2, n_ctx + n_lat, GDN_DV, GDN_DK), F32),
                        pltpu.VMEM((hp, 2, seq, GDN_DK), BF16),
                        pltpu.VMEM((hp, 2, seq, GDN_DV), F32),
                        pltpu.VMEM((hp, 2, seq, GDN_DV), F32)],
        compiler_params=_params(2), name="gdn_chunk_scan",
    )(qkv_l, qkv_l, qkv_l, qkv_c, qkv_c, qkv_c, scalars, scalars, gate, o_norm.reshape(1, GDN_DV))


def gdn_mixer_core(h, p, batch, seq, ctx_len):
    t_lat = batch * seq
    w_in = p["w_in"]
    hv = GDN_HEADS * GDN_DV
    w_qkv = w_in[:, :GDN_QKV].astype(BF16)
    qkv_l = gdn_in(h, w_qkv, p["conv_w"], seq, batch, 0)
    qkv_c = gdn_in(h, w_qkv, p["conv_w"], ctx_len, batch, t_lat)
    gate = project(h, w_in[:, GDN_QKV:GDN_QKV + hv].astype(BF16), out_dtype=BF16, rows=t_lat)
    lane_pad = ((0, 0), (0, V7X_LANES - GDN_SCALARS))
    w_a = jnp.pad(w_in[:, GDN_QKV + hv:GDN_QKV + hv + GDN_SCALARS], lane_pad)
    w_b = jnp.pad(w_in[:, GDN_QKV + hv + GDN_SCALARS:], lane_pad)
    ab = project(h, jnp.concatenate([w_a, w_b], axis=1).astype(BF16))
    parts = gdn_gates(ab, p["a_log"], p["dt_bias"])
    rows = ab.shape[0]
    stacked = jnp.stack([a[:, :GDN_SCALARS] for a in parts], axis=0).reshape(5, rows, 2, GDN_HEADS)
    scalars = stacked.transpose(3, 2, 0, 1).reshape(GDN_HEADS, 10, rows)
    scalars = jnp.pad(scalars, ((0, 0), (0, GDN_ROWS - 10), (0, 0)))
    return gdn_scan(qkv_l, qkv_c, scalars, gate, p["o_norm"], batch, seq, ctx_len)


def kernel(x, c, ctx, c_ctx, mod_w, mod_b, ln_g, ln_b, ffn_w_in, ffn_w_out, hy_w_in, hy_b_in, hy_conv_w, hy_conv_b, hy_fw1, hy_fb1, hy_fw2, hy_fb2, hy_fw3, hy_fb3, hy_fw4, hy_freq, hy_skip, hy_w_out, hy_b_out, mla_w_in, mla_q_norm, mla_kv_norm, mla_w_uq, mla_w_ukv, mla_w_out, gdn_w_in, gdn_conv_w, gdn_a_log, gdn_dt_bias, gdn_o_norm, gdn_w_out):
    batch, seq, d = x.shape
    ctx_len = ctx.shape[1]
    t_lat, t_ctx = batch * seq, batch * ctx_len
    t_all = t_lat + t_ctx
    assert t_lat % ROW_TILE == 0 and t_ctx % ROW_TILE == 0 and seq % ROW_TILE == 0

    n_mod = -(-(batch + 1) // 8) * 8
    cc = jnp.concatenate([c, c_ctx[None], jnp.zeros((n_mod - batch - 1, d), F32)], axis=0)
    mods = adaln_all(cc, mod_w, mod_b).reshape(DEPTH, n_mod, 6, 1, d).transpose(0, 2, 1, 3, 4)
    zero_bias = jnp.zeros((d,), F32)

    ffn_w_in_bf16, ffn_w_out_bf16 = ffn_w_in.astype(BF16), ffn_w_out.astype(BF16)
    s, h = assemble_stream(x.reshape(t_lat, d), ctx.reshape(t_ctx, d), mods[0, 1], mods[0, 0], seq, batch)
    for i in range(DEPTH):
        kind, j = i % N_MIXERS, i // N_MIXERS
        ctx_out = any(l % N_MIXERS != MIXER_HYENA for l in range(i + 1, DEPTH))
        rows_out = t_all if ctx_out else t_lat
        if kind == MIXER_HYENA:
            p = {"w_in": hy_w_in[j].astype(BF16), "b_in": hy_b_in[j], "conv_w": hy_conv_w[j], "conv_b": hy_conv_b[j],
                 "fw1": hy_fw1[j], "fb1": hy_fb1[j], "fw2": hy_fw2[j], "fb2": hy_fb2[j],
                 "fw3": hy_fw3[j], "fb3": hy_fb3[j], "fw4": hy_fw4[j], "freq": hy_freq[j], "skip": hy_skip[j]}
            y = hyena_mixer_core(h, p, seq, batch, 0, rows_out)
            if ctx_out:
                y = hyena_mixer_core(h, p, ctx_len, batch, t_lat, rows_out, into=y)
            w_out, b_out = hy_w_out, hy_b_out[j]
        elif kind == MIXER_MLA:
            p = {"w_in": mla_w_in[j], "q_norm": mla_q_norm[j], "kv_norm": mla_kv_norm[j],
                 "w_uq": mla_w_uq[j], "w_ukv": mla_w_ukv[j]}
            q, k, v = mla_qkv(h, p, t_all, seq, batch)
            y = mla_attention(q, k, v, batch, seq, ctx_len, ctx_out)
            w_out, b_out = mla_w_out, zero_bias
        else:
            assert not ctx_out
            p = {"w_in": gdn_w_in[j], "conv_w": gdn_conv_w[j], "a_log": gdn_a_log[j],
                 "dt_bias": gdn_dt_bias[j], "o_norm": gdn_o_norm[j]}
            y = gdn_mixer_core(h, p, batch, seq, ctx_len)
            w_out, b_out = gdn_w_out, zero_bias
        s, h = out_ln(y, w_out.astype(BF16), j, b_out, s, mods[i, 2], ln_g[i, 0], ln_b[i, 0], mods[i, 4], mods[i, 3],
                      rows_out, seq, batch, ROW_TILE)
        a = ffn_in(h, ffn_w_in_bf16, i, rows_out)
        nxt = min(i + 1, DEPTH - 1)
        s, h = out_ln(a, ffn_w_out_bf16, i, zero_bias, s, mods[i, 5], ln_g[i, 1], ln_b[i, 1],
                      mods[nxt, 1], mods[nxt, 0], rows_out, seq, batch, ROW_TILE // 2)
    return s[:t_lat].reshape(batch, seq, d)
```

```python
import functools
import math

import numpy as np

import jax
import jax.numpy as jnp
from jax import lax
from jax.experimental import pallas as pl
from jax.experimental.pallas import tpu as pltpu

F32 = jnp.float32
BF16 = jnp.bfloat16
HIGHEST = lax.Precision.HIGHEST

D_MODEL = 1024
DEPTH = 4
GRID_W = 64
N_MIXERS = 3
MIXER_HYENA = 0
MIXER_MLA = 1
DEEPNORM_ALPHA = (2 * DEPTH) ** 0.25
LN_EPS = 1e-5
RMS_EPS = 1e-6
D_FF = -(-8 * D_MODEL // (3 * 256)) * 256
HYENA_EMB = 33
HYENA_FILTER_HIDDEN = 64
HYENA_DECAY_TARGET = 1e-2
HYENA_FAST_DECAY_PCT = 0.3
HYENA_SLOW_DECAY_PCT = 1.5
MLA_HEADS = 8
MLA_Q_LORA = 384
MLA_KV_LORA = 256
MLA_NOPE = 128
MLA_ROPE = 64
MLA_V = 128
MLA_SCALE = (MLA_NOPE + MLA_ROPE) ** -0.5
MLA_QK = 256
ROPE_THETA = 10000.0
GDN_HEADS = 8
GDN_DK = 128
GDN_DV = 128
GDN_CHUNK = 128

V7X_LANES = 128
V7X_MXU_DIM = 256
V7X_VMEM_LIMIT_BYTES = 56 * 1024 * 1024
ROW_TILE = 1024


def _params(n_axes):
    return pltpu.CompilerParams(dimension_semantics=("arbitrary",) * n_axes,
                                vmem_limit_bytes=V7X_VMEM_LIMIT_BYTES)


def _mod_row_map(tile, seq, batch):
    return lambda i: (jnp.minimum(i * tile // seq, batch), 0, 0)


def _adaln_kernel(c_ref, w_ref, b_ref, o_ref):
    x = c_ref[...]
    x = (x * jax.nn.sigmoid(x)).astype(BF16)
    o_ref[0] = jnp.dot(x, w_ref[0].astype(BF16), preferred_element_type=F32) + b_ref[0]


def adaln_all(cc, mod_w, mod_b):
    r, d = cc.shape
    depth, _, n = mod_w.shape
    tn = 1536
    return pl.pallas_call(
        _adaln_kernel,
        grid=(depth, n // tn),
        in_specs=[pl.BlockSpec((r, d), lambda l, j: (0, 0)),
                  pl.BlockSpec((1, d, tn), lambda l, j: (l, 0, j)),
                  pl.BlockSpec((1, 1, tn), lambda l, j: (l, 0, j))],
        out_specs=pl.BlockSpec((1, r, tn), lambda l, j: (l, 0, j)),
        out_shape=jax.ShapeDtypeStruct((depth, r, n), F32),
        compiler_params=_params(2), name="adaln_table",
    )(cc, mod_w, mod_b.reshape(depth, 1, n))


def _assemble_kernel(n_lat_tiles, x_ref, c_ref, scale_ref, shift_ref, s_ref, h_ref):
    s = jnp.where(pl.program_id(0) < n_lat_tiles, x_ref[...], c_ref[...])
    s_ref[...] = s
    h_ref[...] = (s * (1.0 + scale_ref[0]) + shift_ref[0]).astype(h_ref.dtype)


def assemble_stream(x2d, ctx2d, scale, shift, seq, batch):
    t_lat, d = x2d.shape
    tm = ROW_TILE
    n_lat, n_ctx = t_lat // tm, ctx2d.shape[0] // tm
    rows = t_lat + ctx2d.shape[0]
    mod_spec = pl.BlockSpec((1, 1, d), _mod_row_map(tm, seq, batch))
    row_spec = pl.BlockSpec((tm, d), lambda i: (i, 0))
    return pl.pallas_call(
        functools.partial(_assemble_kernel, n_lat),
        grid=(n_lat + n_ctx,),
        in_specs=[pl.BlockSpec((tm, d), lambda i: (jnp.minimum(i, n_lat - 1), 0)),
                  pl.BlockSpec((tm, d), lambda i: (jnp.maximum(i - n_lat, 0), 0)), mod_spec, mod_spec],
        out_specs=[row_spec, row_spec],
        out_shape=[jax.ShapeDtypeStruct((rows, d), F32), jax.ShapeDtypeStruct((rows, d), BF16)],
        compiler_params=_params(1), name="assemble_modulate",
    )(x2d, ctx2d, scale, shift)


def _proj_kernel(x_ref, w_ref, o_ref):
    o_ref[...] = jnp.dot(x_ref[...], w_ref[...], preferred_element_type=F32).astype(o_ref.dtype)


def project(x, w, out_dtype=F32, tm=ROW_TILE, tn=None, rows=None):
    m, k = x.shape
    m = m if rows is None else rows
    n = w.shape[1]
    tn = n if tn is None else tn
    return pl.pallas_call(
        _proj_kernel,
        grid=(m // tm, n // tn),
        in_specs=[pl.BlockSpec((tm, k), lambda i, j: (i, 0)),
                  pl.BlockSpec((k, tn), lambda i, j: (0, j))],
        out_specs=pl.BlockSpec((tm, tn), lambda i, j: (i, j)),
        out_shape=jax.ShapeDtypeStruct((m, n), out_dtype),
        compiler_params=_params(2), name="projection",
    )(x, w)


def _out_ln_kernel(x_ref, w_ref, b_ref, s_ref, gate_ref, lng_ref, lnb_ref, scale_ref, shift_ref,
                   s_out_ref, h_out_ref):
    y = jnp.dot(x_ref[...].astype(BF16), w_ref[0], preferred_element_type=F32) + b_ref[...]
    z = DEEPNORM_ALPHA * s_ref[...] + gate_ref[0] * y
    mu = jnp.mean(z, axis=-1, keepdims=True)
    zc = z - mu
    var = jnp.mean(zc * zc, axis=-1, keepdims=True)
    sn = zc * lax.rsqrt(var + LN_EPS) * lng_ref[...] + lnb_ref[...]
    s_out_ref[...] = sn
    h_out_ref[...] = (sn * (1.0 + scale_ref[0]) + shift_ref[0]).astype(h_out_ref.dtype)


def out_ln(x, w, layer, b, s, gate, ln_g, ln_b, scale, shift, rows, seq, batch, tm):
    k = x.shape[1]
    d = w.shape[2]
    mod_spec = pl.BlockSpec((1, 1, d), _mod_row_map(tm, seq, batch))
    vec_spec = pl.BlockSpec((1, d), lambda i: (0, 0))
    row_spec = pl.BlockSpec((tm, d), lambda i: (i, 0))
    return pl.pallas_call(
        _out_ln_kernel,
        grid=(rows // tm,),
        in_specs=[pl.BlockSpec((tm, k), lambda i: (i, 0)),
                  pl.BlockSpec((1, k, d), lambda i: (layer, 0, 0)),
                  vec_spec, row_spec, mod_spec, vec_spec, vec_spec, mod_spec, mod_spec],
        out_specs=[row_spec, row_spec],
        out_shape=[jax.ShapeDtypeStruct((rows, d), F32), jax.ShapeDtypeStruct((rows, d), BF16)],
        compiler_params=_params(1), name="out_proj_ln",
    )(x, w, b.reshape(1, d), s, gate, ln_g.reshape(1, d), ln_b.reshape(1, d), scale, shift)


def _ffn_in_kernel(x_ref, w_ref, a_ref):
    x = x_ref[...]
    dff = a_ref.shape[1]
    for c in range(dff // V7X_MXU_DIM):
        lo = c * V7X_MXU_DIM
        g = jnp.dot(x, w_ref[0, :, lo:lo + V7X_MXU_DIM], preferred_element_type=F32)
        u = jnp.dot(x, w_ref[0, :, dff + lo:dff + lo + V7X_MXU_DIM], preferred_element_type=F32)
        a_ref[:, lo:lo + V7X_MXU_DIM] = (g * jax.nn.sigmoid(g) * u).astype(a_ref.dtype)


def ffn_in(h, w_in, layer, rows):
    k = h.shape[1]
    dff = w_in.shape[2] // 2
    tm = ROW_TILE
    return pl.pallas_call(
        _ffn_in_kernel,
        grid=(rows // tm,),
        in_specs=[pl.BlockSpec((tm, k), lambda i: (i, 0)),
                  pl.BlockSpec((1, k, 2 * dff), lambda i: (layer, 0, 0))],
        out_specs=pl.BlockSpec((tm, dff), lambda i: (i, 0)),
        out_shape=jax.ShapeDtypeStruct((rows, dff), BF16),
        compiler_params=_params(1), name="ffn_swiglu_in",
    )(h, w_in)


V7X_SUBLANES = 8


def _centred_conv3(z, bias, cw, const):
    length = z.shape[0]
    c0, c1, c2 = cw[0:1], cw[1:2], cw[2:3]
    out = pltpu.roll(z, 1, 0) * c0 + z * c1 + pltpu.roll(z, length - 1, 0) * c2
    wrapped_last, wrapped_first = z[length - 1:length], z[0:1]
    if bias is not None:
        out = out + (const + bias * (c0 + c1 + c2))
        wrapped_last, wrapped_first = wrapped_last + bias, wrapped_first + bias
    e = V7X_SUBLANES
    r = lax.broadcasted_iota(jnp.int32, (e, 1), 0)
    top = jnp.where(r == 0, out[0:e] - wrapped_last * c0, out[0:e])
    bot = jnp.where(r == e - 1, out[length - e:] - wrapped_first * c2, out[length - e:])
    return jnp.concatenate([top, out[e:length - e], bot], axis=0)


def _hyena_in_kernel(x_ref, w0_ref, w1_ref, w2_ref, b0_ref, b1_ref, b2_ref, cw0_ref, cw1_ref, cw2_ref,
                     cb0_ref, cb1_ref, cb2_ref, x0_ref, u_ref):
    x = x_ref[...]

    def branch(w_ref, b_ref, cw_ref, cb_ref):
        z = jnp.dot(x, w_ref[...], preferred_element_type=F32)
        return _centred_conv3(z, b_ref[...], cw_ref[...], cb_ref[...])

    x0 = branch(w0_ref, b0_ref, cw0_ref, cb0_ref)
    x1 = branch(w1_ref, b1_ref, cw1_ref, cb1_ref)
    v = branch(w2_ref, b2_ref, cw2_ref, cb2_ref)
    x0_ref[...] = x0.astype(x0_ref.dtype)
    u_ref[...] = (x1 * v).astype(u_ref.dtype)


def hyena_in(h, w_in, b_in, conv_w, conv_b, length, n_seq, row_offset):
    k = h.shape[1]
    d = w_in.shape[1] // 3
    tn = 256
    nb = d // tn
    off = row_offset // length
    w_specs = [pl.BlockSpec((k, tn), lambda b, j, q=q: (0, j + q * nb)) for q in range(3)]
    v_specs = [pl.BlockSpec((1, tn), lambda b, j, q=q: (0, j + q * nb)) for q in range(3)]
    cw_specs = [pl.BlockSpec((3, tn), lambda b, j, q=q: (0, j + q * nb)) for q in range(3)]
    out_spec = pl.BlockSpec((length, tn), lambda b, j: (b, j))
    b2, cb2 = b_in.reshape(1, 3 * d), conv_b.reshape(1, 3 * d)
    return pl.pallas_call(
        _hyena_in_kernel,
        grid=(n_seq, nb),
        in_specs=[pl.BlockSpec((length, k), lambda b, j: (b + off, 0))] + w_specs + v_specs + cw_specs + v_specs,
        out_specs=[out_spec, out_spec],
        out_shape=[jax.ShapeDtypeStruct((n_seq * length, d), BF16)] * 2,
        compiler_params=_params(2), name="hyena_in_proj_conv3",
    )(h, w_in, w_in, w_in, b2, b2, b2, conv_w, conv_w, conv_w, cb2, cb2, cb2)


def _hyena_filter_kernel(feat_ref, w1_ref, b1_ref, w2_ref, b2_ref, w3_ref, b3_ref, w4_ref, freq_ref, win_ref, k_ref):
    freq = freq_ref[...]
    z = jnp.sin(freq * (jnp.dot(feat_ref[...], w1_ref[...], precision=HIGHEST, preferred_element_type=F32) + b1_ref[...]))
    z = jnp.sin(freq * (jnp.dot(z, w2_ref[...], precision=HIGHEST, preferred_element_type=F32) + b2_ref[...]))
    z = jnp.sin(freq * (jnp.dot(z, w3_ref[...], precision=HIGHEST, preferred_element_type=F32) + b3_ref[...]))
    taps = jnp.dot(z, w4_ref[...], precision=HIGHEST, preferred_element_type=F32)
    d = win_ref.shape[1]
    win = win_ref[...]
    k_fwd = taps[:, :d] * win
    k_bwd = taps[:, d:] * win
    tile = k_fwd.shape[0]
    first = (lax.broadcasted_iota(jnp.int32, (tile, 1), 0) + pl.program_id(0) * tile) == 0
    k_ref[:, :d] = jnp.where(first, k_fwd + k_bwd, k_fwd).astype(k_ref.dtype)
    k_ref[:, d:] = jnp.where(first, 0.0, k_bwd).astype(k_ref.dtype)


def _hyena_features(length):
    t01 = np.linspace(0.0, 1.0, length, dtype=np.float32)[:, None]
    bands = (HYENA_EMB - 1) // 2
    w = (np.float32(2.0 * math.pi / length) * np.arange(length, dtype=np.float32)).astype(np.float32)
    f = np.linspace(1e-4, bands - 1, bands, dtype=np.float32)
    ang = (w[:, None] * f[None, :]).astype(np.float32)
    feat = np.concatenate([t01, np.cos(ang), -np.sin(ang)], axis=-1).astype(np.float32)
    return np.pad(feat, ((0, 0), (0, V7X_LANES - HYENA_EMB)))


def _hyena_window(length):
    max_decay = math.log(HYENA_DECAY_TARGET) / HYENA_FAST_DECAY_PCT
    min_decay = math.log(HYENA_DECAY_TARGET) / HYENA_SLOW_DECAY_PCT
    deltas = np.abs(np.linspace(min_decay, max_decay, D_MODEL, dtype=np.float32))
    t = np.linspace(0.0, 1.0, length, dtype=np.float32)
    return np.exp(-t[:, None] * deltas[None, :]).astype(np.float32)


def hyena_filter_taps(length, p):
    hid = V7X_LANES
    pad_h = hid - HYENA_FILTER_HIDDEN
    padc = lambda a: jnp.pad(a, ((0, 0), (0, pad_h)))
    w1 = jnp.pad(p["fw1"], ((0, V7X_LANES - HYENA_EMB), (0, pad_h)))
    w2 = jnp.pad(p["fw2"], ((0, pad_h), (0, pad_h)))
    w3 = jnp.pad(p["fw3"], ((0, pad_h), (0, pad_h)))
    w4 = jnp.pad(p["fw4"], ((0, pad_h), (0, 0)))
    vec = lambda a: padc(a.reshape(1, -1))
    args = (jnp.asarray(_hyena_features(length)), w1, vec(p["fb1"]), w2, vec(p["fb2"]), w3, vec(p["fb3"]), w4,
            vec(p["freq"]), jnp.asarray(_hyena_window(length)))
    tl = min(256, length)
    full = lambda a: pl.BlockSpec(a.shape, lambda i: (0, 0))
    return pl.pallas_call(
        _hyena_filter_kernel,
        grid=(length // tl,),
        in_specs=[pl.BlockSpec((tl, hid), lambda i: (i, 0))] + [full(a) for a in args[1:9]]
                 + [pl.BlockSpec((tl, D_MODEL), lambda i: (i, 0))],
        out_specs=pl.BlockSpec((tl, 2 * D_MODEL), lambda i: (i, 0)),
        out_shape=jax.ShapeDtypeStruct((length, 2 * D_MODEL), BF16),
        compiler_params=_params(1), name="hyena_filter",
    )(*args)


@functools.lru_cache(maxsize=None)
def _negacyclic_dft(length):
    n = 2 * length
    k = np.arange(length, dtype=np.int64)
    turns = ((2 * k[:, None] + 1) * k[None, :]) % (2 * n)
    ang = turns.astype(np.float64) * (2.0 * math.pi / (2 * n))
    c, s = np.cos(ang), np.sin(ang)
    inv = 2.0 / n
    return c, s, (c.T * inv), (s.T * inv)


def _dft_spec_kernel(c_ref, s_ref, taps_ref, h_ref):
    taps = taps_ref[...]
    d = h_ref.shape[2]
    pc = jnp.dot(c_ref[...], taps, preferred_element_type=F32)
    ps = jnp.dot(s_ref[...], taps, preferred_element_type=F32)
    h_ref[0] = pc[:, :d] + pc[:, d:]
    h_ref[1] = ps[:, d:] - ps[:, :d]


def _dft_fwd_kernel(c_ref, s_ref, u_ref, hr_ref, hi_ref, y_ref):
    u = u_ref[...]
    ur = jnp.dot(c_ref[...], u, preferred_element_type=F32)
    us = jnp.dot(s_ref[...], u, preferred_element_type=F32)
    hr, hi = hr_ref[0], hi_ref[0]
    y_ref[0, 0] = (ur * hr + us * hi).astype(y_ref.dtype)
    y_ref[0, 1] = (us * hr - ur * hi).astype(y_ref.dtype)


def _dft_inv_kernel(ct_ref, st_ref, y_ref, u_ref, x0_ref, skip_ref, *rest):
    o_ref = rest[-1]
    y = (jnp.dot(ct_ref[...], y_ref[0, 0], preferred_element_type=F32)
         + jnp.dot(st_ref[...], y_ref[0, 1], preferred_element_type=F32))
    u = u_ref[...].astype(F32)
    o_ref[...] = ((y + u * skip_ref[...]) * x0_ref[...].astype(F32)).astype(o_ref.dtype)


def _into_buffer(into, n_inputs):
    if into is None:
        return [], (), {}
    return [pl.BlockSpec(memory_space=pl.ANY)], (into,), {n_inputs: 0}


def hyena_long_conv(x0, u, taps, skip, length, n_seq, out_rows, row_offset, into):
    d = u.shape[1]
    c, s, ct, st = (jnp.asarray(a, dtype=BF16) for a in _negacyclic_dft(length))
    tf = min(512, length)
    nf = length // tf
    spec = pl.pallas_call(
        _dft_spec_kernel,
        grid=(nf,),
        in_specs=[pl.BlockSpec((tf, length), lambda i: (i, 0)),
                  pl.BlockSpec((tf, length), lambda i: (i, 0)),
                  pl.BlockSpec((length, 2 * d), lambda i: (0, 0))],
        out_specs=pl.BlockSpec((2, tf, d), lambda i: (0, i, 0)),
        out_shape=jax.ShapeDtypeStruct((2, length, d), F32),
        compiler_params=_params(1), name="hyena_filter_spectrum",
    )(c, s, taps)
    y = pl.pallas_call(
        _dft_fwd_kernel,
        grid=(n_seq, nf),
        in_specs=[pl.BlockSpec((tf, length), lambda b, i: (i, 0)),
                  pl.BlockSpec((tf, length), lambda b, i: (i, 0)),
                  pl.BlockSpec((length, d), lambda b, i: (b, 0)),
                  pl.BlockSpec((1, tf, d), lambda b, i: (0, i, 0)),
                  pl.BlockSpec((1, tf, d), lambda b, i: (1, i, 0))],
        out_specs=pl.BlockSpec((1, 2, tf, d), lambda b, i: (b, 0, i, 0)),
        out_shape=jax.ShapeDtypeStruct((n_seq, 2, length, d), BF16),
        compiler_params=_params(2), name="hyena_dft_forward",
    )(c, s, u, spec, spec)
    if into is None and out_rows > n_seq * length:
        into = jnp.zeros((out_rows, d), BF16)
    into_specs, into_args, aliases = _into_buffer(into, 6)
    off = row_offset // tf
    return pl.pallas_call(
        _dft_inv_kernel,
        grid=(n_seq, nf),
        in_specs=[pl.BlockSpec((tf, length), lambda b, i: (i, 0)),
                  pl.BlockSpec((tf, length), lambda b, i: (i, 0)),
                  pl.BlockSpec((1, 2, length, d), lambda b, i: (b, 0, 0, 0)),
                  pl.BlockSpec((tf, d), lambda b, i: (b * nf + i, 0)),
                  pl.BlockSpec((tf, d), lambda b, i: (b * nf + i, 0)),
                  pl.BlockSpec((1, d), lambda b, i: (0, 0))] + into_specs,
        out_specs=pl.BlockSpec((tf, d), lambda b, i: (off + b * nf + i, 0)),
        out_shape=jax.ShapeDtypeStruct((out_rows, d), BF16),
        input_output_aliases=aliases,
        compiler_params=_params(2), name="hyena_dft_inverse",
    )(ct, st, y, u, x0, skip.reshape(1, d), *into_args)


def hyena_mixer_core(h, p, length, n_seq, row_offset, out_rows, into=None):
    x0, u = hyena_in(h, p["w_in"], p["b_in"], p["conv_w"], p["conv_b"], length, n_seq, row_offset)
    taps = hyena_filter_taps(length, p)
    return hyena_long_conv(x0, u, taps, p["skip"], length, n_seq, out_rows, row_offset, into)


def _rope_swap_columns():
    half, quarter = MLA_ROPE // 2, MLA_ROPE // 4
    r = np.arange(MLA_ROPE)
    within = r % half
    src = np.where(within < quarter, r + quarter, r - quarter)
    sign = np.where(within < quarter, -1.0, 1.0).astype(np.float32)
    return src, sign


@functools.lru_cache(maxsize=None)
def _rope_tables(seq, ctx_rows):
    half, quarter = MLA_ROPE // 2, MLA_ROPE // 4
    inv_freq = (np.float32(ROPE_THETA) ** (-np.arange(quarter, dtype=np.float32) / quarter)).astype(np.float32)
    t = np.arange(seq)
    row, col = (t // GRID_W).astype(np.float32), (t % GRID_W).astype(np.float32)
    r = np.arange(MLA_ROPE)
    pos = np.where((r // half)[None, :] == 0, row[:, None], col[:, None]).astype(np.float32)
    ang = (pos * inv_freq[r % quarter][None, :]).astype(np.float32).astype(np.float64)
    cos = np.concatenate([np.cos(ang), np.ones((ctx_rows, MLA_ROPE))], axis=0)
    sin = np.concatenate([np.sin(ang), np.zeros((ctx_rows, MLA_ROPE))], axis=0)
    ones = np.ones((seq + ctx_rows, MLA_NOPE))
    tab_q = (np.concatenate([ones, cos, sin], axis=1) * MLA_SCALE).astype(np.float32)
    tab_k = np.concatenate([cos, sin], axis=1).astype(np.float32)
    return tab_q, tab_k


def _rms_in_kernel(x_ref, g_ref):
    x = x_ref[...]
    return (x * lax.rsqrt(jnp.mean(x * x, axis=-1, keepdims=True) + RMS_EPS) * g_ref[...]).astype(BF16)


def _mla_q_kernel(x_ref, g_ref, w_ref, tab_ref, q_ref):
    acc = jnp.dot(_rms_in_kernel(x_ref, g_ref), w_ref[...], preferred_element_type=F32)
    tab = tab_ref[...]
    for h in range(MLA_HEADS):
        q_ref[:, h * MLA_QK:(h + 1) * MLA_QK] = (acc[:, h * MLA_QK:(h + 1) * MLA_QK] * tab).astype(q_ref.dtype)


def _mla_kv_kernel(x_ref, g_ref, w_ref, rope_ref, tab_ref, k_ref, v_ref):
    acc = jnp.dot(_rms_in_kernel(x_ref, g_ref), w_ref[...], preferred_element_type=F32)
    prod = rope_ref[...] * tab_ref[...]
    k_rot = (prod + pltpu.roll(prod, MLA_ROPE, 1)).astype(k_ref.dtype)
    hw = MLA_NOPE + MLA_V
    for h in range(MLA_HEADS):
        k_ref[:, h * MLA_QK:h * MLA_QK + MLA_NOPE] = acc[:, h * hw:h * hw + MLA_NOPE].astype(k_ref.dtype)
        k_ref[:, h * MLA_QK + MLA_NOPE:(h + 1) * MLA_QK] = k_rot
        v_ref[:, h * MLA_V:(h + 1) * MLA_V] = acc[:, h * hw + MLA_NOPE:(h + 1) * hw].astype(v_ref.dtype)


def mla_qkv(h, p, rows, seq, batch):
    tm = ROW_TILE
    src, sign = _rope_swap_columns()
    w_in = p["w_in"]
    o_kv, o_rope = MLA_Q_LORA, MLA_Q_LORA + MLA_KV_LORA
    w_rope = w_in[:, o_rope:]
    w_cat = jnp.concatenate([w_in[:, o_kv:o_rope], w_rope, w_rope[:, src] * sign, w_in[:, :o_kv]], axis=1).astype(BF16)
    z = project(h[:rows], w_cat)
    w_uq = p["w_uq"].reshape(MLA_Q_LORA, MLA_HEADS, MLA_NOPE + MLA_ROPE)
    w_uq_rope = w_uq[:, :, MLA_NOPE:]
    w_uq = jnp.concatenate([w_uq, w_uq_rope[:, :, src] * sign], axis=2).reshape(MLA_Q_LORA, MLA_HEADS * MLA_QK).astype(BF16)
    tab_q, tab_k = _rope_tables(seq, tm)
    lat_tiles = batch * seq // tm
    tab_map = lambda i: (jnp.where(i < lat_tiles, i % (seq // tm), seq // tm), 0)
    n_tiles = rows // tm
    q = pl.pallas_call(
        _mla_q_kernel,
        grid=(n_tiles,),
        in_specs=[pl.BlockSpec((tm, MLA_Q_LORA), lambda i: (i, 1)),
                  pl.BlockSpec((1, MLA_Q_LORA), lambda i: (0, 0)),
                  pl.BlockSpec((MLA_Q_LORA, MLA_HEADS * MLA_QK), lambda i: (0, 0)),
                  pl.BlockSpec((tm, MLA_QK), tab_map)],
        out_specs=pl.BlockSpec((tm, MLA_HEADS * MLA_QK), lambda i: (i, 0)),
        out_shape=jax.ShapeDtypeStruct((rows, MLA_HEADS * MLA_QK), BF16),
        compiler_params=_params(1), name="mla_q_proj",
    )(z, p["q_norm"].reshape(1, -1), w_uq, jnp.asarray(tab_q))
    k, v = pl.pallas_call(
        _mla_kv_kernel,
        grid=(n_tiles,),
        in_specs=[pl.BlockSpec((tm, MLA_KV_LORA), lambda i: (i, 0)),
                  pl.BlockSpec((1, MLA_KV_LORA), lambda i: (0, 0)),
                  pl.BlockSpec((MLA_KV_LORA, MLA_HEADS * (MLA_NOPE + MLA_V)), lambda i: (0, 0)),
                  pl.BlockSpec((tm, 2 * MLA_ROPE), lambda i: (i, 2)),
                  pl.BlockSpec((tm, 2 * MLA_ROPE), tab_map)],
        out_specs=[pl.BlockSpec((tm, MLA_HEADS * MLA_QK), lambda i: (i, 0)),
                   pl.BlockSpec((tm, MLA_HEADS * MLA_V), lambda i: (i, 0))],
        out_shape=[jax.ShapeDtypeStruct((rows, MLA_HEADS * MLA_QK), BF16),
                   jax.ShapeDtypeStruct((rows, MLA_HEADS * MLA_V), BF16)],
        compiler_params=_params(1), name="mla_kv_proj",
    )(z, p["kv_norm"].reshape(1, -1), p["w_ukv"].astype(BF16), z, jnp.asarray(tab_k))
    return q, k, v


ATTN_SUB = 256


def _dot_nt(a, b):
    return lax.dot_general(a, b, (((1,), (1,)), ((), ())), preferred_element_type=F32)


def _attention_body(q_ref, key_refs, val_refs, o_ref):
    n_sub = q_ref.shape[0] // ATTN_SUB
    rows = lambda i: slice(i * ATTN_SUB, (i + 1) * ATTN_SUB)
    scores = lambda i: [_dot_nt(q_ref[rows(i), :], k_ref[...]) for k_ref in key_refs]
    pending = scores(0)
    for i in range(n_sub):
        s = pending
        if i + 1 < n_sub:
            pending = scores(i + 1)
        m = functools.reduce(jnp.maximum, [jnp.max(x, axis=-1, keepdims=True) for x in s])
        p = [jnp.exp(x - m) for x in s]
        l = sum(jnp.sum(x, axis=-1, keepdims=True) for x in p)
        o = sum(jnp.dot(x.astype(BF16), v_ref[...], preferred_element_type=F32) for x, v_ref in zip(p, val_refs))
        o_ref[rows(i), :] = (o / l).astype(o_ref.dtype)


def _attn_lat_kernel(q_ref, kl_ref, vl_ref, kc_ref, vc_ref, *rest):
    _attention_body(q_ref, (kl_ref, kc_ref), (vl_ref, vc_ref), rest[-1])


def _attn_ctx_kernel(q_ref, kc_ref, vc_ref, into_ref, o_ref):
    del into_ref
    _attention_body(q_ref, (kc_ref,), (vc_ref,), o_ref)


def mla_attention(q, k, v, batch, seq, ctx_len, with_ctx_out):
    tq = seq
    nq = seq // tq
    ctx_blk = batch * seq // ctx_len
    out_rows = batch * (seq + ctx_len) if with_ctx_out else batch * seq
    zeros = jnp.zeros((out_rows, MLA_HEADS * MLA_V), BF16) if with_ctx_out else None
    into_specs, into_args, aliases = _into_buffer(zeros, 5)
    o_lat = pl.pallas_call(
        _attn_lat_kernel,
        grid=(batch, MLA_HEADS, nq),
        in_specs=[pl.BlockSpec((tq, MLA_QK), lambda b, h, i: (b * nq + i, h)),
                  pl.BlockSpec((seq, MLA_QK), lambda b, h, i: (b, h)),
                  pl.BlockSpec((seq, MLA_V), lambda b, h, i: (b, h)),
                  pl.BlockSpec((ctx_len, MLA_QK), lambda b, h, i: (ctx_blk + b, h)),
                  pl.BlockSpec((ctx_len, MLA_V), lambda b, h, i: (ctx_blk + b, h))] + into_specs,
        out_specs=pl.BlockSpec((tq, MLA_V), lambda b, h, i: (b * nq + i, h)),
        out_shape=jax.ShapeDtypeStruct((out_rows, MLA_HEADS * MLA_V), BF16),
        input_output_aliases=aliases,
        compiler_params=_params(3), name="mla_attention_latent",
    )(q, k, v, k, v, *into_args)
    if not with_ctx_out:
        return o_lat
    into_specs, into_args, aliases = _into_buffer(o_lat, 3)
    return pl.pallas_call(
        _attn_ctx_kernel,
        grid=(batch, MLA_HEADS),
        in_specs=[pl.BlockSpec((ctx_len, MLA_QK), lambda b, h: (ctx_blk + b, h)),
                  pl.BlockSpec((ctx_len, MLA_QK), lambda b, h: (ctx_blk + b, h)),
                  pl.BlockSpec((ctx_len, MLA_V), lambda b, h: (ctx_blk + b, h))] + into_specs,
        out_specs=pl.BlockSpec((ctx_len, MLA_V), lambda b, h: (ctx_blk + b, h)),
        out_shape=jax.ShapeDtypeStruct((out_rows, MLA_HEADS * MLA_V), BF16),
        input_output_aliases=aliases,
        compiler_params=_params(2), name="mla_attention_context",
    )(q, k, v, *into_args)


GDN_QKV = 2 * GDN_HEADS * GDN_DK + GDN_HEADS * GDN_DV
GDN_SCALARS = 2 * GDN_HEADS
GDN_ROWS = 16
LOG2_CHUNK = int(math.log2(GDN_CHUNK))
NEG_BIG = -1e30
GDN_GROUP = 128
GDN_HEADS_PER_STEP = 2
GDN_GROUPS_PER_STEP = 4


def _gdn_in_kernel(x_ref, w_ref, cw_ref, o_ref):
    z = jnp.dot(x_ref[...], w_ref[...], preferred_element_type=F32)
    z = _centred_conv3(z, None, cw_ref[...], None)
    z = z * jax.nn.sigmoid(z)
    j = pl.program_id(1)
    blocks_per_kind = GDN_HEADS * GDN_DK // z.shape[1]
    q_scale = jnp.where(j < blocks_per_kind, GDN_DK ** -0.5, 1.0)
    for hh in range(z.shape[1] // GDN_DK):
        zh = z[:, hh * GDN_DK:(hh + 1) * GDN_DK]
        zn = zh * lax.rsqrt(jnp.sum(zh * zh, axis=-1, keepdims=True) + RMS_EPS) * q_scale
        o_ref[:, hh * GDN_DK:(hh + 1) * GDN_DK] = jnp.where(j < 2 * blocks_per_kind, zn, zh).astype(o_ref.dtype)


def gdn_in(h, w_qkv, conv_w, length, n_seq, row_offset):
    k = h.shape[1]
    n = w_qkv.shape[1]
    tn = 256
    off = row_offset // length
    return pl.pallas_call(
        _gdn_in_kernel,
        grid=(n_seq, n // tn),
        in_specs=[pl.BlockSpec((length, k), lambda b, j: (b + off, 0)),
                  pl.BlockSpec((k, tn), lambda b, j: (0, j)),
                  pl.BlockSpec((3, tn), lambda b, j: (0, j))],
        out_specs=pl.BlockSpec((length, tn), lambda b, j: (b, j)),
        out_shape=jax.ShapeDtypeStruct((n_seq * length, n), BF16),
        compiler_params=_params(2), name="gdn_in_proj_conv3",
    )(h, w_qkv, conv_w)


def _gdn_gates_kernel(ab_ref, alog_ref, dtb_ref, beta_ref, ecum_ref, edec_ref, cum_ref, egl_ref):
    a = ab_ref[:, :V7X_LANES]
    b = ab_ref[:, V7X_LANES:]
    x = a + dtb_ref[...]
    softplus = jnp.maximum(x, 0.0) + jnp.log(1.0 + jnp.exp(-jnp.abs(x)))
    g = -jnp.exp(alog_ref[...]) * softplus
    tm = g.shape[0]
    r = lax.broadcasted_iota(jnp.int32, (tm, tm), 0)
    c = lax.broadcasted_iota(jnp.int32, (tm, tm), 1)
    same = (r >> LOG2_CHUNK) == (c >> LOG2_CHUNK)
    prefix = jnp.where(same & (c <= r), 1.0, 0.0)
    suffix = jnp.where(same & (c >= r), 1.0, 0.0)
    total = jnp.where(same, 1.0, 0.0)
    dot = lambda m: jnp.dot(m, g, precision=HIGHEST, preferred_element_type=F32)
    lane = lax.broadcasted_iota(jnp.int32, g.shape, 1)
    cum = jnp.where(lane < GDN_HEADS, dot(prefix), dot(suffix))
    g_last = dot(total)
    beta_ref[...] = jax.nn.sigmoid(b)
    ecum_ref[...] = jnp.exp(cum)
    edec_ref[...] = jnp.exp(g_last - cum)
    cum_ref[...] = cum
    egl_ref[...] = jnp.exp(g_last)


def gdn_gates(ab, a_log, dt_bias):
    rows = ab.shape[0]
    tm = 256
    pad = lambda v: jnp.pad(v.reshape(1, -1), ((0, 0), (0, V7X_LANES - GDN_SCALARS)))
    spec = pl.BlockSpec((tm, V7X_LANES), lambda i: (i, 0))
    vec = pl.BlockSpec((1, V7X_LANES), lambda i: (0, 0))
    return pl.pallas_call(
        _gdn_gates_kernel,
        grid=(rows // tm,),
        in_specs=[pl.BlockSpec((tm, 2 * V7X_LANES), lambda i: (i, 0)), vec, vec],
        out_specs=[spec] * 5,
        out_shape=[jax.ShapeDtypeStruct((rows, V7X_LANES), F32)] * 5,
        compiler_params=_params(1), name="gdn_gates",
    )(ab, pad(a_log), pad(dt_bias))


def _bdot(a, b):
    return jnp.dot(a.astype(BF16), b.astype(BF16), preferred_element_type=F32)


def _bdot_nt(a, b):
    return lax.dot_general(a.astype(BF16), b.astype(BF16), (((1,), (1,)), ((), ())), preferred_element_type=F32)


def _bdot_tn(a, b):
    return lax.dot_general(a.astype(BF16), b.astype(BF16), (((0,), (0,)), ((), ())), preferred_element_type=F32)


def _gdn_group_terms(qkv_ref, sc_ref, problems, want_out):
    ri = lax.broadcasted_iota(jnp.int32, (GDN_GROUP, GDN_GROUP), 0)
    ci = lax.broadcasted_iota(jnp.int32, (GDN_GROUP, GDN_GROUP), 1)
    same = (ri >> LOG2_CHUNK) == (ci >> LOG2_CHUNK)
    incl = [same & (ri >= ci), same & (ri <= ci)]
    strict = [same & (ri > ci), same & (ri < ci)]
    eye = jnp.where(ri == ci, 1.0, 0.0)
    pair = (ri >> 1) == (ci >> 1)
    joins = [((ri >> (lg + 1)) == (ci >> (lg + 1))) & ((ri >> lg) != (ci >> lg)) for lg in range(1, LOG2_CHUNK)]

    scalars = {}

    def load(hh, gi, d):
        rows = pl.ds(pl.multiple_of(gi * GDN_GROUP, GDN_GROUP), GDN_GROUP)
        lanes = slice(hh * GDN_DK, (hh + 1) * GDN_DK)
        if (hh, id(gi)) not in scalars:
            sc = sc_ref[hh, :, rows]
            square = jnp.concatenate([sc, jnp.zeros((GDN_GROUP - GDN_ROWS, GDN_GROUP), F32)], axis=0)
            scalars[(hh, id(gi))] = (sc, square.T)
        sc, sc_t = scalars[(hh, id(gi))]
        base = 5 * d
        return dict(
            q=qkv_ref[0][rows, lanes].astype(F32), k=qkv_ref[1][rows, lanes].astype(F32),
            v=qkv_ref[2][rows, lanes].astype(F32),
            beta=sc_t[:, base + 0:base + 1], ecum=sc_t[:, base + 1:base + 2],
            edec=sc_t[:, base + 2:base + 3], cum_c=sc_t[:, base + 3:base + 4],
            cum_r=sc[base + 3:base + 4, :], d=d)

    ps = [load(hh, gi, d) for hh, gi, d in problems]
    for p in ps:
        p["gamma"] = jnp.exp(jnp.where(incl[p["d"]], p["cum_c"] - p["cum_r"], NEG_BIG))
        p["kb"] = p["k"] * p["beta"]
    for p in ps:
        p["a"] = jnp.where(strict[p["d"]], _bdot_nt(p["kb"], p["k"]) * p["gamma"], 0.0)
    for p in ps:
        p["inv"] = eye - jnp.where(pair, p["a"], 0.0)
    for join in joins:
        for p in ps:
            p["t"] = _bdot(jnp.where(join, p["a"], 0.0), p["inv"])
        for p in ps:
            p["inv"] = p["inv"] - _bdot(p["inv"], p["t"])
    for p in ps:
        p["uw"] = _bdot(p["inv"], jnp.concatenate([p["v"] * p["beta"], p["kb"] * p["ecum"]], axis=-1))
        p["kdec"] = p["k"] * p["edec"]
    if want_out:
        for p in ps:
            p["qk"] = jnp.where(incl[p["d"]], _bdot_nt(p["q"], p["k"]) * p["gamma"], 0.0)
    out = []
    for p in ps:
        kw_t, n_t = [], []
        for c in range(GDN_GROUP // GDN_CHUNK):
            sl = slice(c * GDN_CHUNK, (c + 1) * GDN_CHUNK)
            both = _bdot_tn(p["uw"][sl], p["kdec"][sl])
            n_t.append(both[:GDN_DV])
            kw_t.append(both[GDN_DV:])
        q_eff = o_local = None
        if want_out:
            corr = _bdot(p["qk"], p["uw"])
            q_eff, o_local = p["q"] * p["ecum"] - corr[:, GDN_DV:], corr[:, :GDN_DV]
        out.append((kw_t, n_t, q_eff, o_local))
    return out


def _gdn_scan_kernel(ql_ref, kl_ref, vl_ref, qc_ref, kc_ref, vc_ref, scl_ref, scc_ref,
                     gate_ref, onorm_ref, y_ref, kw_ref, nt_ref, qeff_ref, oloc_ref, out_ref):
    n_ctx = qc_ref.shape[0] // GDN_CHUNK
    n_lat = ql_ref.shape[0] // GDN_CHUNK
    per_group = GDN_GROUP // GDN_CHUNK
    heads = range(GDN_HEADS_PER_STEP)
    chains = [(hh, d) for hh in heads for d in range(2)]

    def precompute(qkv, sc_ref, slot0, want_out):
        groups = min(GDN_GROUPS_PER_STEP, qkv[0].shape[0] // GDN_GROUP)

        def body(it, carry):
            gis = [it * groups + gg for gg in range(groups)]
            problems = [(hh, gi, d) for hh in heads for gi in gis for d in range(2)]
            terms = _gdn_group_terms(qkv, sc_ref, problems, want_out)
            for (hh, gi, d), (kw_t, n_t, q_eff, o_local) in zip(problems, terms):
                for c in range(per_group):
                    kw_ref[hh, d, slot0 + gi * per_group + c] = kw_t[c].astype(kw_ref.dtype)
                    nt_ref[hh, d, slot0 + gi * per_group + c] = n_t[c]
                if want_out:
                    rows = pl.ds(pl.multiple_of(gi * GDN_GROUP, GDN_GROUP), GDN_GROUP)
                    qeff_ref[hh, d, rows, :] = q_eff.astype(qeff_ref.dtype)
                    oloc_ref[hh, d, rows, :] = o_local
            return carry
        lax.fori_loop(0, qkv[0].shape[0] // (GDN_GROUP * groups), body, 0)

    precompute((qc_ref, kc_ref, vc_ref), scc_ref, 0, False)
    precompute((ql_ref, kl_ref, vl_ref), scl_ref, n_ctx, True)

    def chunk_rows(c):
        return pl.ds(pl.multiple_of(c * GDN_CHUNK, GDN_CHUNK), GDN_CHUNK)

    def advance(states, sc_ref, slot0, chunk_of):
        prods = [jnp.dot(st.astype(BF16), kw_ref[hh, d, slot0 + chunk_of[d]], preferred_element_type=F32)
                 for st, (hh, d) in zip(states, chains)]
        new = []
        for st, pr, (hh, d) in zip(states, prods, chains):
            c = chunk_of[d]
            egl = sc_ref[hh, 5 * d + 4:5 * d + 5, chunk_rows(c)][:, 0:1]
            new.append(egl * st - pr + nt_ref[hh, d, slot0 + c])
        return tuple(new)

    def scan_ctx(step, states):
        return advance(states, scc_ref, 0, (step, n_ctx - 1 - step))

    def scan_lat(step, states):
        chunk_of = (step, n_lat - 1 - step)
        outs = [_bdot_nt(qeff_ref[hh, d, chunk_rows(chunk_of[d]), :], st) for st, (hh, d) in zip(states, chains)]
        for o, (hh, d) in zip(outs, chains):
            rows = chunk_rows(chunk_of[d])
            out_ref[hh, d, rows, :] = o + oloc_ref[hh, d, rows, :]
        return advance(states, scl_ref, n_ctx, chunk_of)

    zero = jnp.zeros((GDN_DV, GDN_DK), F32)
    states = lax.fori_loop(0, n_ctx, scan_ctx, tuple(zero for _ in chains))
    lax.fori_loop(0, n_lat, scan_lat, states)

    onorm = onorm_ref[...]

    def finish(gi, carry):
        rows = pl.ds(pl.multiple_of(gi * GDN_GROUP, GDN_GROUP), GDN_GROUP)
        for hh in heads:
            lanes = slice(hh * GDN_DV, (hh + 1) * GDN_DV)
            o = out_ref[hh, 0, rows, :] + out_ref[hh, 1, rows, :]
            o = o * lax.rsqrt(jnp.mean(o * o, axis=-1, keepdims=True) + RMS_EPS) * onorm
            gte = gate_ref[rows, lanes].astype(F32)
            y_ref[rows, lanes] = (o * gte * jax.nn.sigmoid(gte)).astype(y_ref.dtype)
        return carry

    lax.fori_loop(0, ql_ref.shape[0] // GDN_GROUP, finish, 0)


def gdn_scan(qkv_l, qkv_c, scalars, gate, o_norm, batch, seq, ctx_len):
    n_lat, n_ctx = seq // GDN_CHUNK, ctx_len // GDN_CHUNK
    ctx_blk = batch * seq // ctx_len
    hp = GDN_HEADS_PER_STEP
    n_hp = GDN_HEADS // hp
    lat = lambda q: pl.BlockSpec((seq, hp * GDN_DK), lambda b, h, q=q: (b, q * n_hp + h))
    cx = lambda q: pl.BlockSpec((ctx_len, hp * GDN_DK), lambda b, h, q=q: (b, q * n_hp + h))
    return pl.pallas_call(
        _gdn_scan_kernel,
        grid=(batch, n_hp),
        in_specs=[lat(0), lat(1), lat(2), cx(0), cx(1), cx(2),
                  pl.BlockSpec((hp, GDN_ROWS, seq), lambda b, h: (h, 0, b)),
                  pl.BlockSpec((hp, GDN_ROWS, ctx_len), lambda b, h: (h, 0, ctx_blk + b)),
                  pl.BlockSpec((seq, hp * GDN_DV), lambda b, h: (b, h)),
                  pl.BlockSpec((1, GDN_DV), lambda b, h: (0, 0))],
        out_specs=pl.BlockSpec((seq, hp * GDN_DV), lambda b, h: (b, h)),
        out_shape=jax.ShapeDtypeStruct((batch * seq, GDN_HEADS * GDN_DV), BF16),
        scratch_shapes=[pltpu.VMEM((hp, 2, n_ctx + n_lat, GDN_DK, GDN_DK), BF16),
                        pltpu.VMEM((hp, 2, n_ctx + n_lat, GDN_DV, GDN_DK), F32),
                        pltpu.VMEM((hp, 2, seq, GDN_DK), BF16),
                        pltpu.VMEM((hp, 2, seq, GDN_DV), F32),
                        pltpu.VMEM((hp, 2, seq, GDN_DV), F32)],
        compiler_params=_params(2), name="gdn_chunk_scan",
    )(qkv_l, qkv_l, qkv_l, qkv_c, qkv_c, qkv_c, scalars, scalars, gate, o_norm.reshape(1, GDN_DV))


def gdn_mixer_core(h, p, batch, seq, ctx_len):
    t_lat = batch * seq
    w_in = p["w_in"]
    hv = GDN_HEADS * GDN_DV
    w_qkv = w_in[:, :GDN_QKV].astype(BF16)
    qkv_l = gdn_in(h, w_qkv, p["conv_w"], seq, batch, 0)
    qkv_c = gdn_in(h, w_qkv, p["conv_w"], ctx_len, batch, t_lat)
    gate = project(h, w_in[:, GDN_QKV:GDN_QKV + hv].astype(BF16), out_dtype=BF16, rows=t_lat)
    lane_pad = ((0, 0), (0, V7X_LANES - GDN_SCALARS))
    w_a = jnp.pad(w_in[:, GDN_QKV + hv:GDN_QKV + hv + GDN_SCALARS], lane_pad)
    w_b = jnp.pad(w_in[:, GDN_QKV + hv + GDN_SCALARS:], lane_pad)
    ab = project(h, jnp.concatenate([w_a, w_b], axis=1).astype(BF16))
    parts = gdn_gates(ab, p["a_log"], p["dt_bias"])
    rows = ab.shape[0]
    stacked = jnp.stack([a[:, :GDN_SCALARS] for a in parts], axis=0).reshape(5, rows, 2, GDN_HEADS)
    scalars = stacked.transpose(3, 2, 0, 1).reshape(GDN_HEADS, 10, rows)
    scalars = jnp.pad(scalars, ((0, 0), (0, GDN_ROWS - 10), (0, 0)))
    return gdn_scan(qkv_l, qkv_c, scalars, gate, p["o_norm"], batch, seq, ctx_len)


def kernel(x, c, ctx, c_ctx, mod_w, mod_b, ln_g, ln_b, ffn_w_in, ffn_w_out, hy_w_in, hy_b_in, hy_conv_w, hy_conv_b, hy_fw1, hy_fb1, hy_fw2, hy_fb2, hy_fw3, hy_fb3, hy_fw4, hy_freq, hy_skip, hy_w_out, hy_b_out, mla_w_in, mla_q_norm, mla_kv_norm, mla_w_uq, mla_w_ukv, mla_w_out, gdn_w_in, gdn_conv_w, gdn_a_log, gdn_dt_bias, gdn_o_norm, gdn_w_out):
    batch, seq, d = x.shape
    ctx_len = ctx.shape[1]
    t_lat, t_ctx = batch * seq, batch * ctx_len
    t_all = t_lat + t_ctx
    assert t_lat % ROW_TILE == 0 and t_ctx % ROW_TILE == 0 and seq % ROW_TILE == 0

    n_mod = -(-(batch + 1) // 8) * 8
    cc = jnp.concatenate([c, c_ctx[None], jnp.zeros((n_mod - batch - 1, d), F32)], axis=0)
    mods = adaln_all(cc, mod_w, mod_b).reshape(DEPTH, n_mod, 6, 1, d).transpose(0, 2, 1, 3, 4)
    zero_bias = jnp.zeros((d,), F32)

    ffn_w_in_bf16, ffn_w_out_bf16 = ffn_w_in.astype(BF16), ffn_w_out.astype(BF16)
    s, h = assemble_stream(x.reshape(t_lat, d), ctx.reshape(t_ctx, d), mods[0, 1], mods[0, 0], seq, batch)
    for i in range(DEPTH):
        kind, j = i % N_MIXERS, i // N_MIXERS
        ctx_out = any(l % N_MIXERS != MIXER_HYENA for l in range(i + 1, DEPTH))
        rows_out = t_all if ctx_out else t_lat
        if kind == MIXER_HYENA:
            p = {"w_in": hy_w_in[j].astype(BF16), "b_in": hy_b_in[j], "conv_w": hy_conv_w[j], "conv_b": hy_conv_b[j],
                 "fw1": hy_fw1[j], "fb1": hy_fb1[j], "fw2": hy_fw2[j], "fb2": hy_fb2[j],
                 "fw3": hy_fw3[j], "fb3": hy_fb3[j], "fw4": hy_fw4[j], "freq": hy_freq[j], "skip": hy_skip[j]}
            y = hyena_mixer_core(h, p, seq, batch, 0, rows_out)
            if ctx_out:
                y = hyena_mixer_core(h, p, ctx_len, batch, t_lat, rows_out, into=y)
            w_out, b_out = hy_w_out, hy_b_out[j]
        elif kind == MIXER_MLA:
            p = {"w_in": mla_w_in[j], "q_norm": mla_q_norm[j], "kv_norm": mla_kv_norm[j],
                 "w_uq": mla_w_uq[j], "w_ukv": mla_w_ukv[j]}
            q, k, v = mla_qkv(h, p, t_all, seq, batch)
            y = mla_attention(q, k, v, batch, seq, ctx_len, ctx_out)
            w_out, b_out = mla_w_out, zero_bias
        else:
            assert not ctx_out
            p = {"w_in": gdn_w_in[j], "conv_w": gdn_conv_w[j], "a_log": gdn_a_log[j],
                 "dt_bias": gdn_dt_bias[j], "o_norm": gdn_o_norm[j]}
            y = gdn_mixer_core(h, p, batch, seq, ctx_len)
            w_out, b_out = gdn_w_out, zero_bias
        s, h = out_ln(y, w_out.astype(BF16), j, b_out, s, mods[i, 2], ln_g[i, 0], ln_b[i, 0], mods[i, 4], mods[i, 3],
                      rows_out, seq, batch, ROW_TILE)
        a = ffn_in(h, ffn_w_in_bf16, i, rows_out)
        nxt = min(i + 1, DEPTH - 1)
        s, h = out_ln(a, ffn_w_out_bf16, i, zero_bias, s, mods[i, 5], ln_g[i, 1], ln_b[i, 1],
                      mods[nxt, 1], mods[nxt, 0], rows_out, seq, batch, ROW_TILE // 2)
    return s[:t_lat].reshape(batch, seq, d)
```

```python
import functools
import math

import numpy as np

import jax
import jax.numpy as jnp
from jax import lax
from jax.experimental import pallas as pl
from jax.experimental.pallas import tpu as pltpu

F32 = jnp.float32
BF16 = jnp.bfloat16
HIGHEST = lax.Precision.HIGHEST

D_MODEL = 1024
DEPTH = 4
GRID_W = 64
N_MIXERS = 3
MIXER_HYENA = 0
MIXER_MLA = 1
DEEPNORM_ALPHA = (2 * DEPTH) ** 0.25
LN_EPS = 1e-5
RMS_EPS = 1e-6
D_FF = -(-8 * D_MODEL // (3 * 256)) * 256
HYENA_EMB = 33
HYENA_FILTER_HIDDEN = 64
HYENA_DECAY_TARGET = 1e-2
HYENA_FAST_DECAY_PCT = 0.3
HYENA_SLOW_DECAY_PCT = 1.5
MLA_HEADS = 8
MLA_Q_LORA = 384
MLA_KV_LORA = 256
MLA_NOPE = 128
MLA_ROPE = 64
MLA_V = 128
MLA_SCALE = (MLA_NOPE + MLA_ROPE) ** -0.5
MLA_QK = 256
ROPE_THETA = 10000.0
GDN_HEADS = 8
GDN_DK = 128
GDN_DV = 128
GDN_CHUNK = 128

V7X_LANES = 128
V7X_MXU_DIM = 256
V7X_VMEM_LIMIT_BYTES = 56 * 1024 * 1024
ROW_TILE = 1024


def _params(n_axes):
    return pltpu.CompilerParams(dimension_semantics=("arbitrary",) * n_axes,
                                vmem_limit_bytes=V7X_VMEM_LIMIT_BYTES)


def _mod_row_map(tile, seq, batch):
    return lambda i: (jnp.minimum(i * tile // seq, batch), 0, 0)


def _adaln_kernel(c_ref, w_ref, b_ref, o_ref):
    x = c_ref[...]
    x = (x * jax.nn.sigmoid(x)).astype(BF16)
    o_ref[0] = jnp.dot(x, w_ref[0].astype(BF16), preferred_element_type=F32) + b_ref[0]


def adaln_all(cc, mod_w, mod_b):
    r, d = cc.shape
    depth, _, n = mod_w.shape
    tn = 1536
    return pl.pallas_call(
        _adaln_kernel,
        grid=(depth, n // tn),
        in_specs=[pl.BlockSpec((r, d), lambda l, j: (0, 0)),
                  pl.BlockSpec((1, d, tn), lambda l, j: (l, 0, j)),
                  pl.BlockSpec((1, 1, tn), lambda l, j: (l, 0, j))],
        out_specs=pl.BlockSpec((1, r, tn), lambda l, j: (l, 0, j)),
        out_shape=jax.ShapeDtypeStruct((depth, r, n), F32),
        compiler_params=_params(2), name="adaln_table",
    )(cc, mod_w, mod_b.reshape(depth, 1, n))


def _assemble_kernel(n_lat_tiles, x_ref, c_ref, scale_ref, shift_ref, s_ref, h_ref):
    s = jnp.where(pl.program_id(0) < n_lat_tiles, x_ref[...], c_ref[...])
    s_ref[...] = s
    h_ref[...] = (s * (1.0 + scale_ref[0]) + shift_ref[0]).astype(h_ref.dtype)


def assemble_stream(x2d, ctx2d, scale, shift, seq, batch):
    t_lat, d = x2d.shape
    tm = ROW_TILE
    n_lat, n_ctx = t_lat // tm, ctx2d.shape[0] // tm
    rows = t_lat + ctx2d.shape[0]
    mod_spec = pl.BlockSpec((1, 1, d), _mod_row_map(tm, seq, batch))
    row_spec = pl.BlockSpec((tm, d), lambda i: (i, 0))
    return pl.pallas_call(
        functools.partial(_assemble_kernel, n_lat),
        grid=(n_lat + n_ctx,),
        in_specs=[pl.BlockSpec((tm, d), lambda i: (jnp.minimum(i, n_lat - 1), 0)),
                  pl.BlockSpec((tm, d), lambda i: (jnp.maximum(i - n_lat, 0), 0)), mod_spec, mod_spec],
        out_specs=[row_spec, row_spec],
        out_shape=[jax.ShapeDtypeStruct((rows, d), F32), jax.ShapeDtypeStruct((rows, d), BF16)],
        compiler_params=_params(1), name="assemble_modulate",
    )(x2d, ctx2d, scale, shift)


def _proj_kernel(x_ref, w_ref, o_ref):
    o_ref[...] = jnp.dot(x_ref[...], w_ref[...], preferred_element_type=F32).astype(o_ref.dtype)


def project(x, w, out_dtype=F32, tm=ROW_TILE, tn=None, rows=None):
    m, k = x.shape
    m = m if rows is None else rows
    n = w.shape[1]
    tn = n if tn is None else tn
    return pl.pallas_call(
        _proj_kernel,
        grid=(m // tm, n // tn),
        in_specs=[pl.BlockSpec((tm, k), lambda i, j: (i, 0)),
                  pl.BlockSpec((k, tn), lambda i, j: (0, j))],
        out_specs=pl.BlockSpec((tm, tn), lambda i, j: (i, j)),
        out_shape=jax.ShapeDtypeStruct((m, n), out_dtype),
        compiler_params=_params(2), name="projection",
    )(x, w)


def _out_ln_kernel(x_ref, *rest):
    _out_ln_body(x_ref[...], *rest)


def _out_ln_split_kernel(n_lat_tiles, x_ref, xc_ref, *rest):
    _out_ln_body(jnp.where(pl.program_id(0) < n_lat_tiles, x_ref[...], xc_ref[...]), *rest)


def _out_ln_body(x, w_ref, b_ref, s_ref, gate_ref, lng_ref, lnb_ref, scale_ref, shift_ref, s_out_ref, h_out_ref):
    y = jnp.dot(x.astype(BF16), w_ref[0], preferred_element_type=F32) + b_ref[...]
    z = DEEPNORM_ALPHA * s_ref[...] + gate_ref[0] * y
    mu = jnp.mean(z, axis=-1, keepdims=True)
    zc = z - mu
    var = jnp.mean(zc * zc, axis=-1, keepdims=True)
    sn = zc * lax.rsqrt(var + LN_EPS) * lng_ref[...] + lnb_ref[...]
    s_out_ref[...] = sn
    h_out_ref[...] = (sn * (1.0 + scale_ref[0]) + shift_ref[0]).astype(h_out_ref.dtype)


def out_ln(x, w, layer, b, s, gate, ln_g, ln_b, scale, shift, rows, seq, batch, tm, x_ctx=None):
    k = x.shape[1]
    d = w.shape[2]
    mod_spec = pl.BlockSpec((1, 1, d), _mod_row_map(tm, seq, batch))
    vec_spec = pl.BlockSpec((1, d), lambda i: (0, 0))
    row_spec = pl.BlockSpec((tm, d), lambda i: (i, 0))
    if x_ctx is None:
        body, x_specs, x_args = _out_ln_kernel, [pl.BlockSpec((tm, k), lambda i: (i, 0))], (x,)
    else:
        n_lat = x.shape[0] // tm
        body = functools.partial(_out_ln_split_kernel, n_lat)
        x_specs = [pl.BlockSpec((tm, k), lambda i: (jnp.minimum(i, n_lat - 1), 0)),
                   pl.BlockSpec((tm, k), lambda i: (jnp.maximum(i - n_lat, 0), 0))]
        x_args = (x, x_ctx)
    return pl.pallas_call(
        body,
        grid=(rows // tm,),
        in_specs=x_specs + [pl.BlockSpec((1, k, d), lambda i: (layer, 0, 0)),
                            vec_spec, row_spec, mod_spec, vec_spec, vec_spec, mod_spec, mod_spec],
        out_specs=[row_spec, row_spec],
        out_shape=[jax.ShapeDtypeStruct((rows, d), F32), jax.ShapeDtypeStruct((rows, d), BF16)],
        compiler_params=_params(1), name="out_proj_ln",
    )(*x_args, w, b.reshape(1, d), s, gate, ln_g.reshape(1, d), ln_b.reshape(1, d), scale, shift)


def _ffn_in_kernel(x_ref, w_ref, a_ref):
    x = x_ref[...]
    dff = a_ref.shape[1]
    for c in range(dff // V7X_MXU_DIM):
        lo = c * V7X_MXU_DIM
        g = jnp.dot(x, w_ref[0, :, lo:lo + V7X_MXU_DIM], preferred_element_type=F32)
        u = jnp.dot(x, w_ref[0, :, dff + lo:dff + lo + V7X_MXU_DIM], preferred_element_type=F32)
        a_ref[:, lo:lo + V7X_MXU_DIM] = (g * jax.nn.sigmoid(g) * u).astype(a_ref.dtype)


def ffn_in(h, w_in, layer, rows):
    k = h.shape[1]
    dff = w_in.shape[2] // 2
    tm = ROW_TILE
    return pl.pallas_call(
        _ffn_in_kernel,
        grid=(rows // tm,),
        in_specs=[pl.BlockSpec((tm, k), lambda i: (i, 0)),
                  pl.BlockSpec((1, k, 2 * dff), lambda i: (layer, 0, 0))],
        out_specs=pl.BlockSpec((tm, dff), lambda i: (i, 0)),
        out_shape=jax.ShapeDtypeStruct((rows, dff), BF16),
        compiler_params=_params(1), name="ffn_swiglu_in",
    )(h, w_in)


V7X_SUBLANES = 8


def _centred_conv3(z, bias, cw, const, seq_len):
    length = z.shape[0]
    c0, c1, c2 = cw[0:1], cw[1:2], cw[2:3]
    if seq_len != length:
        if bias is not None:
            z = z + bias
        pos = lax.broadcasted_iota(jnp.int32, (length, 1), 0) & (seq_len - 1)
        prev = jnp.where(pos == 0, 0.0, pltpu.roll(z, 1, 0))
        nxt = jnp.where(pos == seq_len - 1, 0.0, pltpu.roll(z, length - 1, 0))
        out = prev * c0 + z * c1 + nxt * c2
        return out if const is None else out + const
    out = pltpu.roll(z, 1, 0) * c0 + z * c1 + pltpu.roll(z, length - 1, 0) * c2
    wrapped_last, wrapped_first = z[length - 1:length], z[0:1]
    if bias is not None:
        out = out + (const + bias * (c0 + c1 + c2))
        wrapped_last, wrapped_first = wrapped_last + bias, wrapped_first + bias
    e = V7X_SUBLANES
    r = lax.broadcasted_iota(jnp.int32, (e, 1), 0)
    top = jnp.where(r == 0, out[0:e] - wrapped_last * c0, out[0:e])
    bot = jnp.where(r == e - 1, out[length - e:] - wrapped_first * c2, out[length - e:])
    return jnp.concatenate([top, out[e:length - e], bot], axis=0)


def _hyena_in_kernel(seq_len, x_ref, w0_ref, w1_ref, w2_ref, b0_ref, b1_ref, b2_ref, cw0_ref, cw1_ref, cw2_ref,
                     cb0_ref, cb1_ref, cb2_ref, x0_ref, u_ref):
    x = x_ref[...]

    def branch(w_ref, b_ref, cw_ref, cb_ref):
        z = jnp.dot(x, w_ref[...], preferred_element_type=F32)
        return _centred_conv3(z, b_ref[...], cw_ref[...], cb_ref[...], seq_len)

    x0 = branch(w0_ref, b0_ref, cw0_ref, cb0_ref)
    x1 = branch(w1_ref, b1_ref, cw1_ref, cb1_ref)
    v = branch(w2_ref, b2_ref, cw2_ref, cb2_ref)
    x0_ref[...] = x0.astype(x0_ref.dtype)
    u_ref[...] = (x1 * v).astype(u_ref.dtype)


def hyena_in(h, w_in, b_in, conv_w, conv_b, length, n_seq, row_offset, seq_len):
    k = h.shape[1]
    d = w_in.shape[1] // 3
    tn = 256
    nb = d // tn
    off = row_offset // length
    w_specs = [pl.BlockSpec((k, tn), lambda b, j, q=q: (0, j + q * nb)) for q in range(3)]
    v_specs = [pl.BlockSpec((1, tn), lambda b, j, q=q: (0, j + q * nb)) for q in range(3)]
    cw_specs = [pl.BlockSpec((3, tn), lambda b, j, q=q: (0, j + q * nb)) for q in range(3)]
    out_spec = pl.BlockSpec((length, tn), lambda b, j: (b, j))
    b2, cb2 = b_in.reshape(1, 3 * d), conv_b.reshape(1, 3 * d)
    return pl.pallas_call(
        functools.partial(_hyena_in_kernel, seq_len),
        grid=(n_seq, nb),
        in_specs=[pl.BlockSpec((length, k), lambda b, j: (b + off, 0))] + w_specs + v_specs + cw_specs + v_specs,
        out_specs=[out_spec, out_spec],
        out_shape=[jax.ShapeDtypeStruct((n_seq * length, d), BF16)] * 2,
        compiler_params=_params(2), name="hyena_in_proj_conv3",
    )(h, w_in, w_in, w_in, b2, b2, b2, conv_w, conv_w, conv_w, cb2, cb2, cb2)


def _hyena_filter_kernel(feat_ref, w1_ref, b1_ref, w2_ref, b2_ref, w3_ref, b3_ref, w4_ref, freq_ref, win_ref, k_ref):
    freq = freq_ref[...]
    z = jnp.sin(freq * (jnp.dot(feat_ref[...], w1_ref[...], precision=HIGHEST, preferred_element_type=F32) + b1_ref[...]))
    z = jnp.sin(freq * (jnp.dot(z, w2_ref[...], precision=HIGHEST, preferred_element_type=F32) + b2_ref[...]))
    z = jnp.sin(freq * (jnp.dot(z, w3_ref[...], precision=HIGHEST, preferred_element_type=F32) + b3_ref[...]))
    taps = jnp.dot(z, w4_ref[...], precision=HIGHEST, preferred_element_type=F32)
    d = win_ref.shape[1]
    win = win_ref[...]
    k_fwd = taps[:, :d] * win
    k_bwd = taps[:, d:] * win
    tile = k_fwd.shape[0]
    first = (lax.broadcasted_iota(jnp.int32, (tile, 1), 0) + pl.program_id(0) * tile) == 0
    k_ref[:, :d] = jnp.where(first, k_fwd + k_bwd, k_fwd).astype(k_ref.dtype)
    k_ref[:, d:] = jnp.where(first, 0.0, k_bwd).astype(k_ref.dtype)


def _hyena_features(length):
    t01 = np.linspace(0.0, 1.0, length, dtype=np.float32)[:, None]
    bands = (HYENA_EMB - 1) // 2
    w = (np.float32(2.0 * math.pi / length) * np.arange(length, dtype=np.float32)).astype(np.float32)
    f = np.linspace(1e-4, bands - 1, bands, dtype=np.float32)
    ang = (w[:, None] * f[None, :]).astype(np.float32)
    feat = np.concatenate([t01, np.cos(ang), -np.sin(ang)], axis=-1).astype(np.float32)
    return np.pad(feat, ((0, 0), (0, V7X_LANES - HYENA_EMB)))


def _hyena_window(length):
    max_decay = math.log(HYENA_DECAY_TARGET) / HYENA_FAST_DECAY_PCT
    min_decay = math.log(HYENA_DECAY_TARGET) / HYENA_SLOW_DECAY_PCT
    deltas = np.abs(np.linspace(min_decay, max_decay, D_MODEL, dtype=np.float32))
    t = np.linspace(0.0, 1.0, length, dtype=np.float32)
    return np.exp(-t[:, None] * deltas[None, :]).astype(np.float32)


def hyena_filter_taps(length, p):
    hid = V7X_LANES
    pad_h = hid - HYENA_FILTER_HIDDEN
    padc = lambda a: jnp.pad(a, ((0, 0), (0, pad_h)))
    w1 = jnp.pad(p["fw1"], ((0, V7X_LANES - HYENA_EMB), (0, pad_h)))
    w2 = jnp.pad(p["fw2"], ((0, pad_h), (0, pad_h)))
    w3 = jnp.pad(p["fw3"], ((0, pad_h), (0, pad_h)))
    w4 = jnp.pad(p["fw4"], ((0, pad_h), (0, 0)))
    vec = lambda a: padc(a.reshape(1, -1))
    args = (jnp.asarray(_hyena_features(length)), w1, vec(p["fb1"]), w2, vec(p["fb2"]), w3, vec(p["fb3"]), w4,
            vec(p["freq"]), jnp.asarray(_hyena_window(length)))
    tl = min(256, length)
    full = lambda a: pl.BlockSpec(a.shape, lambda i: (0, 0))
    return pl.pallas_call(
        _hyena_filter_kernel,
        grid=(length // tl,),
        in_specs=[pl.BlockSpec((tl, hid), lambda i: (i, 0))] + [full(a) for a in args[1:9]]
                 + [pl.BlockSpec((tl, D_MODEL), lambda i: (i, 0))],
        out_specs=pl.BlockSpec((tl, 2 * D_MODEL), lambda i: (i, 0)),
        out_shape=jax.ShapeDtypeStruct((length, 2 * D_MODEL), BF16),
        compiler_params=_params(1), name="hyena_filter",
    )(*args)


@functools.lru_cache(maxsize=None)
def _negacyclic_dft(length):
    n = 2 * length
    k = np.arange(length, dtype=np.int64)
    turns = ((2 * k[:, None] + 1) * k[None, :]) % (2 * n)
    ang = turns.astype(np.float64) * (2.0 * math.pi / (2 * n))
    c, s = np.cos(ang), np.sin(ang)
    inv = 2.0 / n
    return c, s, (c.T * inv), (s.T * inv)


def _dft_spec_kernel(c_ref, s_ref, taps_ref, h_ref):
    taps = taps_ref[...]
    d = h_ref.shape[2]
    pc = jnp.dot(c_ref[...], taps, preferred_element_type=F32)
    ps = jnp.dot(s_ref[...], taps, preferred_element_type=F32)
    h_ref[0] = pc[:, :d] + pc[:, d:]
    h_ref[1] = ps[:, d:] - ps[:, :d]


def _dft_fwd_kernel(c_ref, s_ref, u_ref, hr_ref, hi_ref, y_ref):
    u = u_ref[...]
    ur = jnp.dot(c_ref[...], u, preferred_element_type=F32)
    us = jnp.dot(s_ref[...], u, preferred_element_type=F32)
    hr, hi = hr_ref[0], hi_ref[0]
    y_ref[0, 0] = (ur * hr + us * hi).astype(y_ref.dtype)
    y_ref[0, 1] = (us * hr - ur * hi).astype(y_ref.dtype)


def _dft_inv_kernel(ct_ref, st_ref, y_ref, u_ref, x0_ref, skip_ref, o_ref):
    y = (jnp.dot(ct_ref[...], y_ref[0, 0], preferred_element_type=F32)
         + jnp.dot(st_ref[...], y_ref[0, 1], preferred_element_type=F32))
    u = u_ref[...].astype(F32)
    o_ref[...] = ((y + u * skip_ref[...]) * x0_ref[...].astype(F32)).astype(o_ref.dtype)


def hyena_long_conv(x0, u, taps, skip, length, n_seq):
    d = u.shape[1]
    c, s, ct, st = (jnp.asarray(a, dtype=BF16) for a in _negacyclic_dft(length))
    tf = min(512, length)
    nf = length // tf
    spec = pl.pallas_call(
        _dft_spec_kernel,
        grid=(nf,),
        in_specs=[pl.BlockSpec((tf, length), lambda i: (i, 0)),
                  pl.BlockSpec((tf, length), lambda i: (i, 0)),
                  pl.BlockSpec((length, 2 * d), lambda i: (0, 0))],
        out_specs=pl.BlockSpec((2, tf, d), lambda i: (0, i, 0)),
        out_shape=jax.ShapeDtypeStruct((2, length, d), F32),
        compiler_params=_params(1), name="hyena_filter_spectrum",
    )(c, s, taps)
    y = pl.pallas_call(
        _dft_fwd_kernel,
        grid=(n_seq, nf),
        in_specs=[pl.BlockSpec((tf, length), lambda b, i: (i, 0)),
                  pl.BlockSpec((tf, length), lambda b, i: (i, 0)),
                  pl.BlockSpec((length, d), lambda b, i: (b, 0)),
                  pl.BlockSpec((1, tf, d), lambda b, i: (0, i, 0)),
                  pl.BlockSpec((1, tf, d), lambda b, i: (1, i, 0))],
        out_specs=pl.BlockSpec((1, 2, tf, d), lambda b, i: (b, 0, i, 0)),
        out_shape=jax.ShapeDtypeStruct((n_seq, 2, length, d), BF16),
        compiler_params=_params(2), name="hyena_dft_forward",
    )(c, s, u, spec, spec)
    return pl.pallas_call(
        _dft_inv_kernel,
        grid=(n_seq, nf),
        in_specs=[pl.BlockSpec((tf, length), lambda b, i: (i, 0)),
                  pl.BlockSpec((tf, length), lambda b, i: (i, 0)),
                  pl.BlockSpec((1, 2, length, d), lambda b, i: (b, 0, 0, 0)),
                  pl.BlockSpec((tf, d), lambda b, i: (b * nf + i, 0)),
                  pl.BlockSpec((tf, d), lambda b, i: (b * nf + i, 0)),
                  pl.BlockSpec((1, d), lambda b, i: (0, 0))],
        out_specs=pl.BlockSpec((tf, d), lambda b, i: (b * nf + i, 0)),
        out_shape=jax.ShapeDtypeStruct((n_seq * length, d), BF16),
        compiler_params=_params(2), name="hyena_dft_inverse",
    )(ct, st, y, u, x0, skip.reshape(1, d))


def hyena_mixer_core(h, p, length, n_seq, row_offset, tile_rows):
    x0, u = hyena_in(h, p["w_in"], p["b_in"], p["conv_w"], p["conv_b"], tile_rows, n_seq * length // tile_rows,
                     row_offset, length)
    taps = hyena_filter_taps(length, p)
    return hyena_long_conv(x0, u, taps, p["skip"], length, n_seq)


def _rope_swap_columns():
    half, quarter = MLA_ROPE // 2, MLA_ROPE // 4
    r = np.arange(MLA_ROPE)
    within = r % half
    src = np.where(within < quarter, r + quarter, r - quarter)
    sign = np.where(within < quarter, -1.0, 1.0).astype(np.float32)
    return src, sign


@functools.lru_cache(maxsize=None)
def _rope_tables(seq, ctx_rows):
    half, quarter = MLA_ROPE // 2, MLA_ROPE // 4
    inv_freq = (np.float32(ROPE_THETA) ** (-np.arange(quarter, dtype=np.float32) / quarter)).astype(np.float32)
    t = np.arange(seq)
    row, col = (t // GRID_W).astype(np.float32), (t % GRID_W).astype(np.float32)
    r = np.arange(MLA_ROPE)
    pos = np.where((r // half)[None, :] == 0, row[:, None], col[:, None]).astype(np.float32)
    ang = (pos * inv_freq[r % quarter][None, :]).astype(np.float32).astype(np.float64)
    cos = np.concatenate([np.cos(ang), np.ones((ctx_rows, MLA_ROPE))], axis=0)
    sin = np.concatenate([np.sin(ang), np.zeros((ctx_rows, MLA_ROPE))], axis=0)
    ones = np.ones((seq + ctx_rows, MLA_NOPE))
    tab_q = (np.concatenate([ones, cos, sin], axis=1) * MLA_SCALE).astype(np.float32)
    tab_k = np.concatenate([cos, sin], axis=1).astype(np.float32)
    return tab_q, tab_k


def _rms_in_kernel(x_ref, g_ref):
    x = x_ref[...]
    return (x * lax.rsqrt(jnp.mean(x * x, axis=-1, keepdims=True) + RMS_EPS) * g_ref[...]).astype(BF16)


def _mla_q_kernel(x_ref, g_ref, w_ref, tab_ref, q_ref):
    acc = jnp.dot(_rms_in_kernel(x_ref, g_ref), w_ref[...], preferred_element_type=F32)
    tab = tab_ref[...]
    for h in range(MLA_HEADS):
        q_ref[:, h * MLA_QK:(h + 1) * MLA_QK] = (acc[:, h * MLA_QK:(h + 1) * MLA_QK] * tab).astype(q_ref.dtype)


def _mla_kv_kernel(x_ref, g_ref, w_ref, rope_ref, tab_ref, k_ref, v_ref):
    acc = jnp.dot(_rms_in_kernel(x_ref, g_ref), w_ref[...], preferred_element_type=F32)
    prod = rope_ref[...] * tab_ref[...]
    k_rot = (prod + pltpu.roll(prod, MLA_ROPE, 1)).astype(k_ref.dtype)
    hw = MLA_NOPE + MLA_V
    for h in range(MLA_HEADS):
        k_ref[:, h * MLA_QK:h * MLA_QK + MLA_NOPE] = acc[:, h * hw:h * hw + MLA_NOPE].astype(k_ref.dtype)
        k_ref[:, h * MLA_QK + MLA_NOPE:(h + 1) * MLA_QK] = k_rot
        v_ref[:, h * MLA_V:(h + 1) * MLA_V] = acc[:, h * hw + MLA_NOPE:(h + 1) * hw].astype(v_ref.dtype)


def mla_qkv(h, p, rows, seq, batch):
    tm = ROW_TILE
    src, sign = _rope_swap_columns()
    w_in = p["w_in"]
    o_kv, o_rope = MLA_Q_LORA, MLA_Q_LORA + MLA_KV_LORA
    w_rope = w_in[:, o_rope:]
    w_cat = jnp.concatenate([w_in[:, o_kv:o_rope], w_rope, w_rope[:, src] * sign, w_in[:, :o_kv]], axis=1).astype(BF16)
    z = project(h[:rows], w_cat)
    w_uq = p["w_uq"].reshape(MLA_Q_LORA, MLA_HEADS, MLA_NOPE + MLA_ROPE)
    w_uq_rope = w_uq[:, :, MLA_NOPE:]
    w_uq = jnp.concatenate([w_uq, w_uq_rope[:, :, src] * sign], axis=2).reshape(MLA_Q_LORA, MLA_HEADS * MLA_QK).astype(BF16)
    tab_q, tab_k = _rope_tables(seq, tm)
    lat_tiles = batch * seq // tm
    tab_map = lambda i: (jnp.where(i < lat_tiles, i % (seq // tm), seq // tm), 0)
    n_tiles = rows // tm
    q = pl.pallas_call(
        _mla_q_kernel,
        grid=(n_tiles,),
        in_specs=[pl.BlockSpec((tm, MLA_Q_LORA), lambda i: (i, 1)),
                  pl.BlockSpec((1, MLA_Q_LORA), lambda i: (0, 0)),
                  pl.BlockSpec((MLA_Q_LORA, MLA_HEADS * MLA_QK), lambda i: (0, 0)),
                  pl.BlockSpec((tm, MLA_QK), tab_map)],
        out_specs=pl.BlockSpec((tm, MLA_HEADS * MLA_QK), lambda i: (i, 0)),
        out_shape=jax.ShapeDtypeStruct((rows, MLA_HEADS * MLA_QK), BF16),
        compiler_params=_params(1), name="mla_q_proj",
    )(z, p["q_norm"].reshape(1, -1), w_uq, jnp.asarray(tab_q))
    k, v = pl.pallas_call(
        _mla_kv_kernel,
        grid=(n_tiles,),
        in_specs=[pl.BlockSpec((tm, MLA_KV_LORA), lambda i: (i, 0)),
                  pl.BlockSpec((1, MLA_KV_LORA), lambda i: (0, 0)),
                  pl.BlockSpec((MLA_KV_LORA, MLA_HEADS * (MLA_NOPE + MLA_V)), lambda i: (0, 0)),
                  pl.BlockSpec((tm, 2 * MLA_ROPE), lambda i: (i, 2)),
                  pl.BlockSpec((tm, 2 * MLA_ROPE), tab_map)],
        out_specs=[pl.BlockSpec((tm, MLA_HEADS * MLA_QK), lambda i: (i, 0)),
                   pl.BlockSpec((tm, MLA_HEADS * MLA_V), lambda i: (i, 0))],
        out_shape=[jax.ShapeDtypeStruct((rows, MLA_HEADS * MLA_QK), BF16),
                   jax.ShapeDtypeStruct((rows, MLA_HEADS * MLA_V), BF16)],
        compiler_params=_params(1), name="mla_kv_proj",
    )(z, p["kv_norm"].reshape(1, -1), p["w_ukv"].astype(BF16), z, jnp.asarray(tab_k))
    return q, k, v


ATTN_SUB = 256


def _dot_nt(a, b):
    return lax.dot_general(a, b, (((1,), (1,)), ((), ())), preferred_element_type=F32)


def _attention_body(q_ref, key_refs, val_refs, o_ref):
    n_sub = q_ref.shape[0] // ATTN_SUB
    rows = lambda i: slice(i * ATTN_SUB, (i + 1) * ATTN_SUB)
    scores = lambda i: [_dot_nt(q_ref[rows(i), :], k_ref[...]) for k_ref in key_refs]
    pending = scores(0)
    for i in range(n_sub):
        s = pending
        if i + 1 < n_sub:
            pending = scores(i + 1)
        m = functools.reduce(jnp.maximum, [jnp.max(x, axis=-1, keepdims=True) for x in s])
        p = [jnp.exp(x - m) for x in s]
        l = sum(jnp.sum(x, axis=-1, keepdims=True) for x in p)
        o = sum(jnp.dot(x.astype(BF16), v_ref[...], preferred_element_type=F32) for x, v_ref in zip(p, val_refs))
        o_ref[rows(i), :] = (o / l).astype(o_ref.dtype)


def _attn_kernel(with_ctx_out, ql_ref, qc_ref, kl_ref, vl_ref, kc_ref, vc_ref, *o_refs):
    _attention_body(ql_ref, (kl_ref, kc_ref), (vl_ref, vc_ref), o_refs[0])
    if with_ctx_out:
        _attention_body(qc_ref, (kc_ref,), (vc_ref,), o_refs[1])


def mla_attention(q, k, v, batch, seq, ctx_len, with_ctx_out):
    ctx_blk = batch * seq // ctx_len
    lat = lambda w: pl.BlockSpec((seq, w), lambda b, h: (b, h))
    cx = lambda w: pl.BlockSpec((ctx_len, w), lambda b, h: (ctx_blk + b, h))
    out_specs = [pl.BlockSpec((seq, MLA_V), lambda b, h: (b, h))]
    out_shape = [jax.ShapeDtypeStruct((batch * seq, MLA_HEADS * MLA_V), BF16)]
    if with_ctx_out:
        out_specs.append(pl.BlockSpec((ctx_len, MLA_V), lambda b, h: (b, h)))
        out_shape.append(jax.ShapeDtypeStruct((batch * ctx_len, MLA_HEADS * MLA_V), BF16))
    outs = pl.pallas_call(
        functools.partial(_attn_kernel, with_ctx_out),
        grid=(batch, MLA_HEADS),
        in_specs=[lat(MLA_QK), cx(MLA_QK), lat(MLA_QK), lat(MLA_V), cx(MLA_QK), cx(MLA_V)],
        out_specs=out_specs, out_shape=out_shape,
        compiler_params=_params(2), name="mla_attention",
    )(q, q, k, v, k, v)
    return (outs[0], outs[1]) if with_ctx_out else (outs[0], None)


GDN_QKV = 2 * GDN_HEADS * GDN_DK + GDN_HEADS * GDN_DV
GDN_SCALARS = 2 * GDN_HEADS
GDN_ROWS = 16
LOG2_CHUNK = int(math.log2(GDN_CHUNK))
NEG_BIG = -1e30
GDN_GROUP = 128
GDN_HEADS_PER_STEP = 2
GDN_GROUPS_PER_STEP = 4


def _gdn_in_kernel(seq_len, x_ref, w_ref, cw_ref, o_ref):
    z = jnp.dot(x_ref[...], w_ref[...], preferred_element_type=F32)
    z = _centred_conv3(z, None, cw_ref[...], None, seq_len)
    z = z * jax.nn.sigmoid(z)
    j = pl.program_id(1)
    blocks_per_kind = GDN_HEADS * GDN_DK // z.shape[1]
    q_scale = jnp.where(j < blocks_per_kind, GDN_DK ** -0.5, 1.0)
    for hh in range(z.shape[1] // GDN_DK):
        zh = z[:, hh * GDN_DK:(hh + 1) * GDN_DK]
        zn = zh * lax.rsqrt(jnp.sum(zh * zh, axis=-1, keepdims=True) + RMS_EPS) * q_scale
        o_ref[:, hh * GDN_DK:(hh + 1) * GDN_DK] = jnp.where(j < 2 * blocks_per_kind, zn, zh).astype(o_ref.dtype)


def gdn_in(h, w_qkv, conv_w, length, n_seq, row_offset, seq_len):
    k = h.shape[1]
    n = w_qkv.shape[1]
    tn = 256
    off = row_offset // length
    return pl.pallas_call(
        functools.partial(_gdn_in_kernel, seq_len),
        grid=(n_seq, n // tn),
        in_specs=[pl.BlockSpec((length, k), lambda b, j: (b + off, 0)),
                  pl.BlockSpec((k, tn), lambda b, j: (0, j)),
                  pl.BlockSpec((3, tn), lambda b, j: (0, j))],
        out_specs=pl.BlockSpec((length, tn), lambda b, j: (b, j)),
        out_shape=jax.ShapeDtypeStruct((n_seq * length, n), BF16),
        compiler_params=_params(2), name="gdn_in_proj_conv3",
    )(h, w_qkv, conv_w)


def _gdn_gates_kernel(ab_ref, alog_ref, dtb_ref, beta_ref, ecum_ref, edec_ref, cum_ref, egl_ref):
    a = ab_ref[:, :V7X_LANES]
    b = ab_ref[:, V7X_LANES:]
    x = a + dtb_ref[...]
    softplus = jnp.maximum(x, 0.0) + jnp.log(1.0 + jnp.exp(-jnp.abs(x)))
    g = -jnp.exp(alog_ref[...]) * softplus
    tm = g.shape[0]
    r = lax.broadcasted_iota(jnp.int32, (tm, tm), 0)
    c = lax.broadcasted_iota(jnp.int32, (tm, tm), 1)
    same = (r >> LOG2_CHUNK) == (c >> LOG2_CHUNK)
    prefix = jnp.where(same & (c <= r), 1.0, 0.0)
    suffix = jnp.where(same & (c >= r), 1.0, 0.0)
    total = jnp.where(same, 1.0, 0.0)
    dot = lambda m: jnp.dot(m, g, precision=HIGHEST, preferred_element_type=F32)
    lane = lax.broadcasted_iota(jnp.int32, g.shape, 1)
    cum = jnp.where(lane < GDN_HEADS, dot(prefix), dot(suffix))
    g_last = dot(total)
    beta_ref[...] = jax.nn.sigmoid(b)
    ecum_ref[...] = jnp.exp(cum)
    edec_ref[...] = jnp.exp(g_last - cum)
    cum_ref[...] = cum
    egl_ref[...] = jnp.exp(g_last)


def gdn_gates(ab, a_log, dt_bias):
    rows = ab.shape[0]
    tm = GDN_CHUNK
    pad = lambda v: jnp.pad(v.reshape(1, -1), ((0, 0), (0, V7X_LANES - GDN_SCALARS)))
    spec = pl.BlockSpec((tm, V7X_LANES), lambda i: (i, 0))
    vec = pl.BlockSpec((1, V7X_LANES), lambda i: (0, 0))
    return pl.pallas_call(
        _gdn_gates_kernel,
        grid=(rows // tm,),
        in_specs=[pl.BlockSpec((tm, 2 * V7X_LANES), lambda i: (i, 0)), vec, vec],
        out_specs=[spec] * 5,
        out_shape=[jax.ShapeDtypeStruct((rows, V7X_LANES), F32)] * 5,
        compiler_params=_params(1), name="gdn_gates",
    )(ab, pad(a_log), pad(dt_bias))


def _bdot(a, b):
    return jnp.dot(a.astype(BF16), b.astype(BF16), preferred_element_type=F32)


def _bdot_nt(a, b):
    return lax.dot_general(a.astype(BF16), b.astype(BF16), (((1,), (1,)), ((), ())), preferred_element_type=F32)


def _bdot_tn(a, b):
    return lax.dot_general(a.astype(BF16), b.astype(BF16), (((0,), (0,)), ((), ())), preferred_element_type=F32)


def _gdn_group_terms(qkv_ref, sc_ref, problems, want_out):
    ri = lax.broadcasted_iota(jnp.int32, (GDN_GROUP, GDN_GROUP), 0)
    ci = lax.broadcasted_iota(jnp.int32, (GDN_GROUP, GDN_GROUP), 1)
    same = (ri >> LOG2_CHUNK) == (ci >> LOG2_CHUNK)
    incl = [same & (ri >= ci), same & (ri <= ci)]
    strict = [same & (ri > ci), same & (ri < ci)]
    eye = jnp.where(ri == ci, 1.0, 0.0)
    pair = (ri >> 1) == (ci >> 1)
    joins = [((ri >> (lg + 1)) == (ci >> (lg + 1))) & ((ri >> lg) != (ci >> lg)) for lg in range(1, LOG2_CHUNK)]

    scalars = {}

    def load(hh, gi, d):
        rows = pl.ds(pl.multiple_of(gi * GDN_GROUP, GDN_GROUP), GDN_GROUP)
        lanes = slice(hh * GDN_DK, (hh + 1) * GDN_DK)
        if (hh, id(gi)) not in scalars:
            sc = sc_ref[hh, :, rows]
            square = jnp.concatenate([sc, jnp.zeros((GDN_GROUP - GDN_ROWS, GDN_GROUP), F32)], axis=0)
            scalars[(hh, id(gi))] = (sc, square.T)
        sc, sc_t = scalars[(hh, id(gi))]
        base = 5 * d
        return dict(
            q=qkv_ref[0][rows, lanes].astype(F32), k=qkv_ref[1][rows, lanes].astype(F32),
            v=qkv_ref[2][rows, lanes].astype(F32),
            beta=sc_t[:, base + 0:base + 1], ecum=sc_t[:, base + 1:base + 2],
            edec=sc_t[:, base + 2:base + 3], cum_c=sc_t[:, base + 3:base + 4],
            cum_r=sc[base + 3:base + 4, :], d=d)

    ps = [load(hh, gi, d) for hh, gi, d in problems]
    for p in ps:
        p["gamma"] = jnp.exp(jnp.where(incl[p["d"]], p["cum_c"] - p["cum_r"], NEG_BIG))
        p["kb"] = p["k"] * p["beta"]
    for p in ps:
        p["a"] = jnp.where(strict[p["d"]], _bdot_nt(p["kb"], p["k"]) * p["gamma"], 0.0)
    for p in ps:
        p["inv"] = eye - jnp.where(pair, p["a"], 0.0)
    for join in joins:
        for p in ps:
            p["t"] = _bdot(jnp.where(join, p["a"], 0.0), p["inv"])
        for p in ps:
            p["inv"] = p["inv"] - _bdot(p["inv"], p["t"])
    for p in ps:
        p["uw"] = _bdot(p["inv"], jnp.concatenate([p["v"] * p["beta"], p["kb"] * p["ecum"]], axis=-1))
        p["kdec"] = p["k"] * p["edec"]
    if want_out:
        for p in ps:
            p["qk"] = jnp.where(incl[p["d"]], _bdot_nt(p["q"], p["k"]) * p["gamma"], 0.0)
    out = []
    for p in ps:
        kw_t, n_t = [], []
        for c in range(GDN_GROUP // GDN_CHUNK):
            sl = slice(c * GDN_CHUNK, (c + 1) * GDN_CHUNK)
            both = _bdot_tn(p["uw"][sl], p["kdec"][sl])
            n_t.append(both[:GDN_DV])
            kw_t.append(both[GDN_DV:])
        q_eff = o_local = None
        if want_out:
            corr = _bdot(p["qk"], p["uw"])
            q_eff, o_local = p["q"] * p["ecum"] - corr[:, GDN_DV:], corr[:, :GDN_DV]
        out.append((kw_t, n_t, q_eff, o_local))
    return out


def _gdn_scan_kernel(ql_ref, kl_ref, vl_ref, qc_ref, kc_ref, vc_ref, scl_ref, scc_ref,
                     gate_ref, onorm_ref, y_ref, kw_ref, nt_ref, qeff_ref, oloc_ref, out_ref):
    n_ctx = qc_ref.shape[0] // GDN_CHUNK
    n_lat = ql_ref.shape[0] // GDN_CHUNK
    per_group = GDN_GROUP // GDN_CHUNK
    heads = range(GDN_HEADS_PER_STEP)
    chains = [(hh, d) for hh in heads for d in range(2)]

    def precompute(qkv, sc_ref, slot0, want_out):
        groups = min(GDN_GROUPS_PER_STEP, qkv[0].shape[0] // GDN_GROUP)

        def body(it, carry):
            gis = [it * groups + gg for gg in range(groups)]
            problems = [(hh, gi, d) for hh in heads for gi in gis for d in range(2)]
            terms = _gdn_group_terms(qkv, sc_ref, problems, want_out)
            for (hh, gi, d), (kw_t, n_t, q_eff, o_local) in zip(problems, terms):
                for c in range(per_group):
                    kw_ref[hh, d, slot0 + gi * per_group + c] = kw_t[c].astype(kw_ref.dtype)
                    nt_ref[hh, d, slot0 + gi * per_group + c] = n_t[c]
                if want_out:
                    rows = pl.ds(pl.multiple_of(gi * GDN_GROUP, GDN_GROUP), GDN_GROUP)
                    qeff_ref[hh, d, rows, :] = q_eff.astype(qeff_ref.dtype)
                    oloc_ref[hh, d, rows, :] = o_local
            return carry
        lax.fori_loop(0, qkv[0].shape[0] // (GDN_GROUP * groups), body, 0)

    precompute((qc_ref, kc_ref, vc_ref), scc_ref, 0, False)
    precompute((ql_ref, kl_ref, vl_ref), scl_ref, n_ctx, True)

    def chunk_rows(c):
        return pl.ds(pl.multiple_of(c * GDN_CHUNK, GDN_CHUNK), GDN_CHUNK)

    def advance(states, sc_ref, slot0, chunk_of):
        prods = [jnp.dot(st.astype(BF16), kw_ref[hh, d, slot0 + chunk_of[d]], preferred_element_type=F32)
                 for st, (hh, d) in zip(states, chains)]
        new = []
        for st, pr, (hh, d) in zip(states, prods, chains):
            c = chunk_of[d]
            egl = sc_ref[hh, 5 * d + 4:5 * d + 5, chunk_rows(c)][:, 0:1]
            new.append(egl * st - pr + nt_ref[hh, d, slot0 + c])
        return tuple(new)

    def scan_ctx(step, states):
        return advance(states, scc_ref, 0, (step, n_ctx - 1 - step))

    def scan_lat(step, states):
        chunk_of = (step, n_lat - 1 - step)
        outs = [_bdot_nt(qeff_ref[hh, d, chunk_rows(chunk_of[d]), :], st) for st, (hh, d) in zip(states, chains)]
        for o, (hh, d) in zip(outs, chains):
            rows = chunk_rows(chunk_of[d])
            out_ref[hh, d, rows, :] = o + oloc_ref[hh, d, rows, :]
        return advance(states, scl_ref, n_ctx, chunk_of)

    zero = jnp.zeros((GDN_DV, GDN_DK), F32)
    states = lax.fori_loop(0, n_ctx, scan_ctx, tuple(zero for _ in chains))
    lax.fori_loop(0, n_lat, scan_lat, states)

    onorm = onorm_ref[...]

    def finish(gi, carry):
        rows = pl.ds(pl.multiple_of(gi * GDN_GROUP, GDN_GROUP), GDN_GROUP)
        for hh in heads:
            lanes = slice(hh * GDN_DV, (hh + 1) * GDN_DV)
            o = out_ref[hh, 0, rows, :] + out_ref[hh, 1, rows, :]
            o = o * lax.rsqrt(jnp.mean(o * o, axis=-1, keepdims=True) + RMS_EPS) * onorm
            gte = gate_ref[rows, lanes].astype(F32)
            y_ref[rows, lanes] = (o * gte * jax.nn.sigmoid(gte)).astype(y_ref.dtype)
        return carry

    lax.fori_loop(0, ql_ref.shape[0] // GDN_GROUP, finish, 0)


def gdn_scan(qkv_l, qkv_c, scalars, gate, o_norm, batch, seq, ctx_len):
    n_lat, n_ctx = seq // GDN_CHUNK, ctx_len // GDN_CHUNK
    ctx_blk = batch * seq // ctx_len
    hp = GDN_HEADS_PER_STEP
    n_hp = GDN_HEADS // hp
    lat = lambda q: pl.BlockSpec((seq, hp * GDN_DK), lambda b, h, q=q: (b, q * n_hp + h))
    cx = lambda q: pl.BlockSpec((ctx_len, hp * GDN_DK), lambda b, h, q=q: (b, q * n_hp + h))
    return pl.pallas_call(
        _gdn_scan_kernel,
        grid=(batch, n_hp),
        in_specs=[lat(0), lat(1), lat(2), cx(0), cx(1), cx(2),
                  pl.BlockSpec((hp, GDN_ROWS, seq), lambda b, h: (h, 0, b)),
                  pl.BlockSpec((hp, GDN_ROWS, ctx_len), lambda b, h: (h, 0, ctx_blk + b)),
                  pl.BlockSpec((seq, hp * GDN_DV), lambda b, h: (b, h)),
                  pl.BlockSpec((1, GDN_DV), lambda b, h: (0, 0))],
        out_specs=pl.BlockSpec((seq, hp * GDN_DV), lambda b, h: (b, h)),
        out_shape=jax.ShapeDtypeStruct((batch * seq, GDN_HEADS * GDN_DV), BF16),
        scratch_shapes=[pltpu.VMEM((hp, 2, n_ctx + n_lat, GDN_DK, GDN_DK), BF16),
                        pltpu.VMEM((hp, 2, n_ctx + n_lat, GDN_DV, GDN_DK), F32),
                        pltpu.VMEM((hp, 2, seq, GDN_DK), BF16),
                        pltpu.VMEM((hp, 2, seq, GDN_DV), F32),
                        pltpu.VMEM((hp, 2, seq, GDN_DV), F32)],
        compiler_params=_params(2), name="gdn_chunk_scan",
    )(qkv_l, qkv_l, qkv_l, qkv_c, qkv_c, qkv_c, scalars, scalars, gate, o_norm.reshape(1, GDN_DV))


def gdn_mixer_core(h, p, batch, seq, ctx_len):
    t_lat = batch * seq
    w_in = p["w_in"]
    hv = GDN_HEADS * GDN_DV
    w_qkv = w_in[:, :GDN_QKV].astype(BF16)
    qkv_l = gdn_in(h, w_qkv, p["conv_w"], seq, batch, 0, seq)
    qkv_c = gdn_in(h, w_qkv, p["conv_w"], batch * ctx_len, 1, t_lat, ctx_len)
    gate = project(h, w_in[:, GDN_QKV:GDN_QKV + hv].astype(BF16), out_dtype=BF16, rows=t_lat)
    lane_pad = ((0, 0), (0, V7X_LANES - GDN_SCALARS))
    w_a = jnp.pad(w_in[:, GDN_QKV + hv:GDN_QKV + hv + GDN_SCALARS], lane_pad)
    w_b = jnp.pad(w_in[:, GDN_QKV + hv + GDN_SCALARS:], lane_pad)
    ab = project(h, jnp.concatenate([w_a, w_b], axis=1).astype(BF16))
    parts = gdn_gates(ab, p["a_log"], p["dt_bias"])
    rows = ab.shape[0]
    stacked = jnp.stack([a[:, :GDN_SCALARS] for a in parts], axis=0).reshape(5, rows, 2, GDN_HEADS)
    scalars = stacked.transpose(3, 2, 0, 1).reshape(GDN_HEADS, 10, rows)
    scalars = jnp.pad(scalars, ((0, 0), (0, GDN_ROWS - 10), (0, 0)))
    return gdn_scan(qkv_l, qkv_c, scalars, gate, p["o_norm"], batch, seq, ctx_len)


def kernel(x, c, ctx, c_ctx, mod_w, mod_b, ln_g, ln_b, ffn_w_in, ffn_w_out, hy_w_in, hy_b_in, hy_conv_w, hy_conv_b, hy_fw1, hy_fb1, hy_fw2, hy_fb2, hy_fw3, hy_fb3, hy_fw4, hy_freq, hy_skip, hy_w_out, hy_b_out, mla_w_in, mla_q_norm, mla_kv_norm, mla_w_uq, mla_w_ukv, mla_w_out, gdn_w_in, gdn_conv_w, gdn_a_log, gdn_dt_bias, gdn_o_norm, gdn_w_out):
    batch, seq, d = x.shape
    ctx_len = ctx.shape[1]
    t_lat, t_ctx = batch * seq, batch * ctx_len
    t_all = t_lat + t_ctx
    assert t_lat % ROW_TILE == 0 and t_ctx % ROW_TILE == 0 and seq % ROW_TILE == 0

    n_mod = -(-(batch + 1) // 8) * 8
    cc = jnp.concatenate([c, c_ctx[None], jnp.zeros((n_mod - batch - 1, d), F32)], axis=0)
    mods = adaln_all(cc, mod_w, mod_b).reshape(DEPTH, n_mod, 6, 1, d).transpose(0, 2, 1, 3, 4)
    zero_bias = jnp.zeros((d,), F32)

    ffn_w_in_bf16, ffn_w_out_bf16 = ffn_w_in.astype(BF16), ffn_w_out.astype(BF16)
    s, h = assemble_stream(x.reshape(t_lat, d), ctx.reshape(t_ctx, d), mods[0, 1], mods[0, 0], seq, batch)
    for i in range(DEPTH):
        kind, j = i % N_MIXERS, i // N_MIXERS
        ctx_out = any(l % N_MIXERS != MIXER_HYENA for l in range(i + 1, DEPTH))
        rows_out = t_all if ctx_out else t_lat
        if kind == MIXER_HYENA:
            p = {"w_in": hy_w_in[j].astype(BF16), "b_in": hy_b_in[j], "conv_w": hy_conv_w[j], "conv_b": hy_conv_b[j],
                 "fw1": hy_fw1[j], "fb1": hy_fb1[j], "fw2": hy_fw2[j], "fb2": hy_fb2[j],
                 "fw3": hy_fw3[j], "fb3": hy_fb3[j], "fw4": hy_fw4[j], "freq": hy_freq[j], "skip": hy_skip[j]}
            y = hyena_mixer_core(h, p, seq, batch, 0, seq)
            y_ctx = hyena_mixer_core(h, p, ctx_len, batch, t_lat, t_ctx) if ctx_out else None
            w_out, b_out = hy_w_out, hy_b_out[j]
        elif kind == MIXER_MLA:
            p = {"w_in": mla_w_in[j], "q_norm": mla_q_norm[j], "kv_norm": mla_kv_norm[j],
                 "w_uq": mla_w_uq[j], "w_ukv": mla_w_ukv[j]}
            q, k, v = mla_qkv(h, p, t_all, seq, batch)
            y, y_ctx = mla_attention(q, k, v, batch, seq, ctx_len, ctx_out)
            w_out, b_out = mla_w_out, zero_bias
        else:
            assert not ctx_out
            p = {"w_in": gdn_w_in[j], "conv_w": gdn_conv_w[j], "a_log": gdn_a_log[j],
                 "dt_bias": gdn_dt_bias[j], "o_norm": gdn_o_norm[j]}
            y, y_ctx = gdn_mixer_core(h, p, batch, seq, ctx_len), None
            w_out, b_out = gdn_w_out, zero_bias
        s, h = out_ln(y, w_out.astype(BF16), j, b_out, s, mods[i, 2], ln_g[i, 0], ln_b[i, 0], mods[i, 4], mods[i, 3],
                      rows_out, seq, batch, ROW_TILE, x_ctx=y_ctx)
        a = ffn_in(h, ffn_w_in_bf16, i, rows_out)
        nxt = min(i + 1, DEPTH - 1)
        s, h = out_ln(a, ffn_w_out_bf16, i, zero_bias, s, mods[i, 5], ln_g[i, 1], ln_b[i, 1],
                      mods[nxt, 1], mods[nxt, 0], rows_out, seq, batch, ROW_TILE // 2)
    return s[:t_lat].reshape(batch, seq, d)
```

```python
import functools
import math

import numpy as np

import jax
import jax.numpy as jnp
from jax import lax
from jax.experimental import pallas as pl
from jax.experimental.pallas import tpu as pltpu

F32 = jnp.float32
BF16 = jnp.bfloat16
HIGHEST = lax.Precision.HIGHEST

D_MODEL = 1024
DEPTH = 4
GRID_W = 64
N_MIXERS = 3
MIXER_HYENA = 0
MIXER_MLA = 1
DEEPNORM_ALPHA = (2 * DEPTH) ** 0.25
LN_EPS = 1e-5
RMS_EPS = 1e-6
D_FF = -(-8 * D_MODEL // (3 * 256)) * 256
HYENA_EMB = 33
HYENA_FILTER_HIDDEN = 64
HYENA_DECAY_TARGET = 1e-2
HYENA_FAST_DECAY_PCT = 0.3
HYENA_SLOW_DECAY_PCT = 1.5
MLA_HEADS = 8
MLA_Q_LORA = 384
MLA_KV_LORA = 256
MLA_NOPE = 128
MLA_ROPE = 64
MLA_V = 128
MLA_SCALE = (MLA_NOPE + MLA_ROPE) ** -0.5
MLA_QK = 256
ROPE_THETA = 10000.0
GDN_HEADS = 8
GDN_DK = 128
GDN_DV = 128
GDN_CHUNK = 128

V7X_LANES = 128
V7X_MXU_DIM = 256
V7X_VMEM_LIMIT_BYTES = 56 * 1024 * 1024
ROW_TILE = 1024


def _params(n_axes):
    return pltpu.CompilerParams(dimension_semantics=("arbitrary",) * n_axes,
                                vmem_limit_bytes=V7X_VMEM_LIMIT_BYTES)


def _mod_row_map(tile, seq, batch):
    return lambda i: (jnp.minimum(i * tile // seq, batch), 0, 0)


def _adaln_kernel(c_ref, w_ref, b_ref, o_ref):
    x = c_ref[...]
    x = (x * jax.nn.sigmoid(x)).astype(BF16)
    o_ref[0] = jnp.dot(x, w_ref[0].astype(BF16), preferred_element_type=F32) + b_ref[0]


def adaln_all(cc, mod_w, mod_b):
    r, d = cc.shape
    depth, _, n = mod_w.shape
    tn = 1536
    return pl.pallas_call(
        _adaln_kernel,
        grid=(depth, n // tn),
        in_specs=[pl.BlockSpec((r, d), lambda l, j: (0, 0)),
                  pl.BlockSpec((1, d, tn), lambda l, j: (l, 0, j)),
                  pl.BlockSpec((1, 1, tn), lambda l, j: (l, 0, j))],
        out_specs=pl.BlockSpec((1, r, tn), lambda l, j: (l, 0, j)),
        out_shape=jax.ShapeDtypeStruct((depth, r, n), F32),
        compiler_params=_params(2), name="adaln_table",
    )(cc, mod_w, mod_b.reshape(depth, 1, n))


def _assemble_kernel(n_lat_tiles, x_ref, c_ref, scale_ref, shift_ref, s_ref, h_ref):
    s = jnp.where(pl.program_id(0) < n_lat_tiles, x_ref[...], c_ref[...])
    s_ref[...] = s
    h_ref[...] = (s * (1.0 + scale_ref[0]) + shift_ref[0]).astype(h_ref.dtype)


def assemble_stream(x2d, ctx2d, scale, shift, seq, batch):
    t_lat, d = x2d.shape
    tm = ROW_TILE
    n_lat, n_ctx = t_lat // tm, ctx2d.shape[0] // tm
    rows = t_lat + ctx2d.shape[0]
    mod_spec = pl.BlockSpec((1, 1, d), _mod_row_map(tm, seq, batch))
    row_spec = pl.BlockSpec((tm, d), lambda i: (i, 0))
    return pl.pallas_call(
        functools.partial(_assemble_kernel, n_lat),
        grid=(n_lat + n_ctx,),
        in_specs=[pl.BlockSpec((tm, d), lambda i: (jnp.minimum(i, n_lat - 1), 0)),
                  pl.BlockSpec((tm, d), lambda i: (jnp.maximum(i - n_lat, 0), 0)), mod_spec, mod_spec],
        out_specs=[row_spec, row_spec],
        out_shape=[jax.ShapeDtypeStruct((rows, d), F32), jax.ShapeDtypeStruct((rows, d), BF16)],
        compiler_params=_params(1), name="assemble_modulate",
    )(x2d, ctx2d, scale, shift)


def _proj_kernel(x_ref, w_ref, o_ref):
    o_ref[...] = jnp.dot(x_ref[...], w_ref[...], preferred_element_type=F32).astype(o_ref.dtype)


def project(x, w, out_dtype=F32, tm=ROW_TILE, tn=None, rows=None):
    m, k = x.shape
    m = m if rows is None else rows
    n = w.shape[1]
    tn = n if tn is None else tn
    return pl.pallas_call(
        _proj_kernel,
        grid=(m // tm, n // tn),
        in_specs=[pl.BlockSpec((tm, k), lambda i, j: (i, 0)),
                  pl.BlockSpec((k, tn), lambda i, j: (0, j))],
        out_specs=pl.BlockSpec((tm, tn), lambda i, j: (i, j)),
        out_shape=jax.ShapeDtypeStruct((m, n), out_dtype),
        compiler_params=_params(2), name="projection",
    )(x, w)


def _out_ln_kernel(x_ref, *rest):
    _out_ln_body(x_ref[...], *rest)


def _out_ln_split_kernel(n_lat_tiles, x_ref, xc_ref, *rest):
    _out_ln_body(jnp.where(pl.program_id(0) < n_lat_tiles, x_ref[...], xc_ref[...]), *rest)


def _out_ln_body(x, w_ref, b_ref, s_ref, gate_ref, lng_ref, lnb_ref, scale_ref, shift_ref, s_out_ref, h_out_ref):
    y = jnp.dot(x.astype(BF16), w_ref[0], preferred_element_type=F32) + b_ref[...]
    z = DEEPNORM_ALPHA * s_ref[...] + gate_ref[0] * y
    mu = jnp.mean(z, axis=-1, keepdims=True)
    zc = z - mu
    var = jnp.mean(zc * zc, axis=-1, keepdims=True)
    sn = zc * lax.rsqrt(var + LN_EPS) * lng_ref[...] + lnb_ref[...]
    s_out_ref[...] = sn
    h_out_ref[...] = (sn * (1.0 + scale_ref[0]) + shift_ref[0]).astype(h_out_ref.dtype)


def out_ln(x, w, layer, b, s, gate, ln_g, ln_b, scale, shift, rows, seq, batch, tm, x_ctx=None):
    k = x.shape[1]
    d = w.shape[2]
    mod_spec = pl.BlockSpec((1, 1, d), _mod_row_map(tm, seq, batch))
    vec_spec = pl.BlockSpec((1, d), lambda i: (0, 0))
    row_spec = pl.BlockSpec((tm, d), lambda i: (i, 0))
    if x_ctx is None:
        body, x_specs, x_args = _out_ln_kernel, [pl.BlockSpec((tm, k), lambda i: (i, 0))], (x,)
    else:
        n_lat = x.shape[0] // tm
        body = functools.partial(_out_ln_split_kernel, n_lat)
        x_specs = [pl.BlockSpec((tm, k), lambda i: (jnp.minimum(i, n_lat - 1), 0)),
                   pl.BlockSpec((tm, k), lambda i: (jnp.maximum(i - n_lat, 0), 0))]
        x_args = (x, x_ctx)
    return pl.pallas_call(
        body,
        grid=(rows // tm,),
        in_specs=x_specs + [pl.BlockSpec((1, k, d), lambda i: (layer, 0, 0)),
                            vec_spec, row_spec, mod_spec, vec_spec, vec_spec, mod_spec, mod_spec],
        out_specs=[row_spec, row_spec],
        out_shape=[jax.ShapeDtypeStruct((rows, d), F32), jax.ShapeDtypeStruct((rows, d), BF16)],
        compiler_params=_params(1), name="out_proj_ln",
    )(*x_args, w, b.reshape(1, d), s, gate, ln_g.reshape(1, d), ln_b.reshape(1, d), scale, shift)


def _ffn_in_kernel(x_ref, w_ref, a_ref):
    x = x_ref[...]
    dff = a_ref.shape[1]
    for c in range(dff // V7X_MXU_DIM):
        lo = c * V7X_MXU_DIM
        g = jnp.dot(x, w_ref[0, :, lo:lo + V7X_MXU_DIM], preferred_element_type=F32)
        u = jnp.dot(x, w_ref[0, :, dff + lo:dff + lo + V7X_MXU_DIM], preferred_element_type=F32)
        a_ref[:, lo:lo + V7X_MXU_DIM] = (g * jax.nn.sigmoid(g) * u).astype(a_ref.dtype)


def ffn_in(h, w_in, layer, rows):
    k = h.shape[1]
    dff = w_in.shape[2] // 2
    tm = ROW_TILE
    return pl.pallas_call(
        _ffn_in_kernel,
        grid=(rows // tm,),
        in_specs=[pl.BlockSpec((tm, k), lambda i: (i, 0)),
                  pl.BlockSpec((1, k, 2 * dff), lambda i: (layer, 0, 0))],
        out_specs=pl.BlockSpec((tm, dff), lambda i: (i, 0)),
        out_shape=jax.ShapeDtypeStruct((rows, dff), BF16),
        compiler_params=_params(1), name="ffn_swiglu_in",
    )(h, w_in)


V7X_SUBLANES = 8


def _centred_conv3(z, bias, cw, const, seq_len):
    length = z.shape[0]
    c0, c1, c2 = cw[0:1], cw[1:2], cw[2:3]
    if seq_len != length:
        if bias is not None:
            z = z + bias
        pos = lax.broadcasted_iota(jnp.int32, (length, 1), 0) & (seq_len - 1)
        prev = jnp.where(pos == 0, 0.0, pltpu.roll(z, 1, 0))
        nxt = jnp.where(pos == seq_len - 1, 0.0, pltpu.roll(z, length - 1, 0))
        out = prev * c0 + z * c1 + nxt * c2
        return out if const is None else out + const
    out = pltpu.roll(z, 1, 0) * c0 + z * c1 + pltpu.roll(z, length - 1, 0) * c2
    wrapped_last, wrapped_first = z[length - 1:length], z[0:1]
    if bias is not None:
        out = out + (const + bias * (c0 + c1 + c2))
        wrapped_last, wrapped_first = wrapped_last + bias, wrapped_first + bias
    e = V7X_SUBLANES
    r = lax.broadcasted_iota(jnp.int32, (e, 1), 0)
    top = jnp.where(r == 0, out[0:e] - wrapped_last * c0, out[0:e])
    bot = jnp.where(r == e - 1, out[length - e:] - wrapped_first * c2, out[length - e:])
    return jnp.concatenate([top, out[e:length - e], bot], axis=0)


def _hyena_in_kernel(seq_len, x_ref, w0_ref, w1_ref, w2_ref, b0_ref, b1_ref, b2_ref, cw0_ref, cw1_ref, cw2_ref,
                     cb0_ref, cb1_ref, cb2_ref, x0_ref, u_ref):
    x = x_ref[...]

    def branch(w_ref, b_ref, cw_ref, cb_ref):
        z = jnp.dot(x, w_ref[...], preferred_element_type=F32)
        return _centred_conv3(z, b_ref[...], cw_ref[...], cb_ref[...], seq_len)

    x0 = branch(w0_ref, b0_ref, cw0_ref, cb0_ref)
    x1 = branch(w1_ref, b1_ref, cw1_ref, cb1_ref)
    v = branch(w2_ref, b2_ref, cw2_ref, cb2_ref)
    x0_ref[...] = x0.astype(x0_ref.dtype)
    u_ref[...] = (x1 * v).astype(u_ref.dtype)


def hyena_in(h, w_in, b_in, conv_w, conv_b, length, n_seq, row_offset, seq_len):
    k = h.shape[1]
    d = w_in.shape[1] // 3
    tn = 256
    nb = d // tn
    off = row_offset // length
    w_specs = [pl.BlockSpec((k, tn), lambda b, j, q=q: (0, j + q * nb)) for q in range(3)]
    v_specs = [pl.BlockSpec((1, tn), lambda b, j, q=q: (0, j + q * nb)) for q in range(3)]
    cw_specs = [pl.BlockSpec((3, tn), lambda b, j, q=q: (0, j + q * nb)) for q in range(3)]
    out_spec = pl.BlockSpec((length, tn), lambda b, j: (b, j))
    b2, cb2 = b_in.reshape(1, 3 * d), conv_b.reshape(1, 3 * d)
    return pl.pallas_call(
        functools.partial(_hyena_in_kernel, seq_len),
        grid=(n_seq, nb),
        in_specs=[pl.BlockSpec((length, k), lambda b, j: (b + off, 0))] + w_specs + v_specs + cw_specs + v_specs,
        out_specs=[out_spec, out_spec],
        out_shape=[jax.ShapeDtypeStruct((n_seq * length, d), BF16)] * 2,
        compiler_params=_params(2), name="hyena_in_proj_conv3",
    )(h, w_in, w_in, w_in, b2, b2, b2, conv_w, conv_w, conv_w, cb2, cb2, cb2)


def _hyena_filter_kernel(feat_ref, w1_ref, b1_ref, w2_ref, b2_ref, w3_ref, b3_ref, w4_ref, freq_ref, win_ref, k_ref):
    freq = freq_ref[...]
    z = jnp.sin(freq * (jnp.dot(feat_ref[...], w1_ref[...], precision=HIGHEST, preferred_element_type=F32) + b1_ref[...]))
    z = jnp.sin(freq * (jnp.dot(z, w2_ref[...], precision=HIGHEST, preferred_element_type=F32) + b2_ref[...]))
    z = jnp.sin(freq * (jnp.dot(z, w3_ref[...], precision=HIGHEST, preferred_element_type=F32) + b3_ref[...]))
    taps = jnp.dot(z, w4_ref[...], precision=HIGHEST, preferred_element_type=F32)
    d = win_ref.shape[1]
    win = win_ref[...]
    k_fwd = taps[:, :d] * win
    k_bwd = taps[:, d:] * win
    tile = k_fwd.shape[0]
    first = (lax.broadcasted_iota(jnp.int32, (tile, 1), 0) + pl.program_id(0) * tile) == 0
    k_ref[:, :d] = jnp.where(first, k_fwd + k_bwd, k_fwd).astype(k_ref.dtype)
    k_ref[:, d:] = jnp.where(first, 0.0, k_bwd).astype(k_ref.dtype)


def _hyena_features(length):
    t01 = np.linspace(0.0, 1.0, length, dtype=np.float32)[:, None]
    bands = (HYENA_EMB - 1) // 2
    w = (np.float32(2.0 * math.pi / length) * np.arange(length, dtype=np.float32)).astype(np.float32)
    f = np.linspace(1e-4, bands - 1, bands, dtype=np.float32)
    ang = (w[:, None] * f[None, :]).astype(np.float32)
    feat = np.concatenate([t01, np.cos(ang), -np.sin(ang)], axis=-1).astype(np.float32)
    return np.pad(feat, ((0, 0), (0, V7X_LANES - HYENA_EMB)))


def _hyena_window(length):
    max_decay = math.log(HYENA_DECAY_TARGET) / HYENA_FAST_DECAY_PCT
    min_decay = math.log(HYENA_DECAY_TARGET) / HYENA_SLOW_DECAY_PCT
    deltas = np.abs(np.linspace(min_decay, max_decay, D_MODEL, dtype=np.float32))
    t = np.linspace(0.0, 1.0, length, dtype=np.float32)
    return np.exp(-t[:, None] * deltas[None, :]).astype(np.float32)


def hyena_filter_taps(length, p):
    hid = V7X_LANES
    pad_h = hid - HYENA_FILTER_HIDDEN
    padc = lambda a: jnp.pad(a, ((0, 0), (0, pad_h)))
    w1 = jnp.pad(p["fw1"], ((0, V7X_LANES - HYENA_EMB), (0, pad_h)))
    w2 = jnp.pad(p["fw2"], ((0, pad_h), (0, pad_h)))
    w3 = jnp.pad(p["fw3"], ((0, pad_h), (0, pad_h)))
    w4 = jnp.pad(p["fw4"], ((0, pad_h), (0, 0)))
    vec = lambda a: padc(a.reshape(1, -1))
    args = (jnp.asarray(_hyena_features(length)), w1, vec(p["fb1"]), w2, vec(p["fb2"]), w3, vec(p["fb3"]), w4,
            vec(p["freq"]), jnp.asarray(_hyena_window(length)))
    tl = min(256, length)
    full = lambda a: pl.BlockSpec(a.shape, lambda i: (0, 0))
    return pl.pallas_call(
        _hyena_filter_kernel,
        grid=(length // tl,),
        in_specs=[pl.BlockSpec((tl, hid), lambda i: (i, 0))] + [full(a) for a in args[1:9]]
                 + [pl.BlockSpec((tl, D_MODEL), lambda i: (i, 0))],
        out_specs=pl.BlockSpec((tl, 2 * D_MODEL), lambda i: (i, 0)),
        out_shape=jax.ShapeDtypeStruct((length, 2 * D_MODEL), BF16),
        compiler_params=_params(1), name="hyena_filter",
    )(*args)


@functools.lru_cache(maxsize=None)
def _negacyclic_dft(length):
    n = 2 * length
    k = np.arange(length, dtype=np.int64)
    turns = ((2 * k[:, None] + 1) * k[None, :]) % (2 * n)
    ang = turns.astype(np.float64) * (2.0 * math.pi / (2 * n))
    c, s = np.cos(ang), np.sin(ang)
    inv = 2.0 / n
    return c, s, (c.T * inv), (s.T * inv)


def _dft_spec_kernel(c_ref, s_ref, taps_ref, h_ref):
    taps = taps_ref[...]
    d = h_ref.shape[2]
    pc = jnp.dot(c_ref[...], taps, preferred_element_type=F32)
    ps = jnp.dot(s_ref[...], taps, preferred_element_type=F32)
    h_ref[0] = pc[:, :d] + pc[:, d:]
    h_ref[1] = ps[:, d:] - ps[:, :d]


def _dft_fwd_kernel(c_ref, s_ref, u_ref, hr_ref, hi_ref, y_ref):
    u = u_ref[...]
    ur = jnp.dot(c_ref[...], u, preferred_element_type=F32)
    us = jnp.dot(s_ref[...], u, preferred_element_type=F32)
    hr, hi = hr_ref[0], hi_ref[0]
    y_ref[0, 0] = (ur * hr + us * hi).astype(y_ref.dtype)
    y_ref[0, 1] = (us * hr - ur * hi).astype(y_ref.dtype)


def _dft_inv_kernel(ct_ref, st_ref, y_ref, u_ref, x0_ref, skip_ref, o_ref):
    y = (jnp.dot(ct_ref[...], y_ref[0, 0], preferred_element_type=F32)
         + jnp.dot(st_ref[...], y_ref[0, 1], preferred_element_type=F32))
    u = u_ref[...].astype(F32)
    o_ref[...] = ((y + u * skip_ref[...]) * x0_ref[...].astype(F32)).astype(o_ref.dtype)


def hyena_long_conv(x0, u, taps, skip, length, n_seq):
    d = u.shape[1]
    c, s, ct, st = (jnp.asarray(a, dtype=BF16) for a in _negacyclic_dft(length))
    tf = min(512, length)
    nf = length // tf
    spec = pl.pallas_call(
        _dft_spec_kernel,
        grid=(nf,),
        in_specs=[pl.BlockSpec((tf, length), lambda i: (i, 0)),
                  pl.BlockSpec((tf, length), lambda i: (i, 0)),
                  pl.BlockSpec((length, 2 * d), lambda i: (0, 0))],
        out_specs=pl.BlockSpec((2, tf, d), lambda i: (0, i, 0)),
        out_shape=jax.ShapeDtypeStruct((2, length, d), F32),
        compiler_params=_params(1), name="hyena_filter_spectrum",
    )(c, s, taps)
    y = pl.pallas_call(
        _dft_fwd_kernel,
        grid=(n_seq, nf),
        in_specs=[pl.BlockSpec((tf, length), lambda b, i: (i, 0)),
                  pl.BlockSpec((tf, length), lambda b, i: (i, 0)),
                  pl.BlockSpec((length, d), lambda b, i: (b, 0)),
                  pl.BlockSpec((1, tf, d), lambda b, i: (0, i, 0)),
                  pl.BlockSpec((1, tf, d), lambda b, i: (1, i, 0))],
        out_specs=pl.BlockSpec((1, 2, tf, d), lambda b, i: (b, 0, i, 0)),
        out_shape=jax.ShapeDtypeStruct((n_seq, 2, length, d), BF16),
        compiler_params=_params(2), name="hyena_dft_forward",
    )(c, s, u, spec, spec)
    return pl.pallas_call(
        _dft_inv_kernel,
        grid=(n_seq, nf),
        in_specs=[pl.BlockSpec((tf, length), lambda b, i: (i, 0)),
                  pl.BlockSpec((tf, length), lambda b, i: (i, 0)),
                  pl.BlockSpec((1, 2, length, d), lambda b, i: (b, 0, 0, 0)),
                  pl.BlockSpec((tf, d), lambda b, i: (b * nf + i, 0)),
                  pl.BlockSpec((tf, d), lambda b, i: (b * nf + i, 0)),
                  pl.BlockSpec((1, d), lambda b, i: (0, 0))],
        out_specs=pl.BlockSpec((tf, d), lambda b, i: (b * nf + i, 0)),
        out_shape=jax.ShapeDtypeStruct((n_seq * length, d), BF16),
        compiler_params=_params(2), name="hyena_dft_inverse",
    )(ct, st, y, u, x0, skip.reshape(1, d))


def hyena_mixer_core(h, p, length, n_seq, row_offset, tile_rows):
    x0, u = hyena_in(h, p["w_in"], p["b_in"], p["conv_w"], p["conv_b"], tile_rows, n_seq * length // tile_rows,
                     row_offset, length)
    taps = hyena_filter_taps(length, p)
    return hyena_long_conv(x0, u, taps, p["skip"], length, n_seq)


def _rope_swap_columns():
    half, quarter = MLA_ROPE // 2, MLA_ROPE // 4
    r = np.arange(MLA_ROPE)
    within = r % half
    src = np.where(within < quarter, r + quarter, r - quarter)
    sign = np.where(within < quarter, -1.0, 1.0).astype(np.float32)
    return src, sign


@functools.lru_cache(maxsize=None)
def _rope_tables(seq, ctx_rows):
    half, quarter = MLA_ROPE // 2, MLA_ROPE // 4
    inv_freq = (np.float32(ROPE_THETA) ** (-np.arange(quarter, dtype=np.float32) / quarter)).astype(np.float32)
    t = np.arange(seq)
    row, col = (t // GRID_W).astype(np.float32), (t % GRID_W).astype(np.float32)
    r = np.arange(MLA_ROPE)
    pos = np.where((r // half)[None, :] == 0, row[:, None], col[:, None]).astype(np.float32)
    ang = (pos * inv_freq[r % quarter][None, :]).astype(np.float32).astype(np.float64)
    cos = np.concatenate([np.cos(ang), np.ones((ctx_rows, MLA_ROPE))], axis=0)
    sin = np.concatenate([np.sin(ang), np.zeros((ctx_rows, MLA_ROPE))], axis=0)
    ones = np.ones((seq + ctx_rows, MLA_NOPE))
    tab_q = (np.concatenate([ones, cos, sin], axis=1) * MLA_SCALE).astype(np.float32)
    tab_k = np.concatenate([cos, sin], axis=1).astype(np.float32)
    return tab_q, tab_k


def _rms_bf16(x, g):
    return (x * lax.rsqrt(jnp.mean(x * x, axis=-1, keepdims=True) + RMS_EPS) * g).astype(BF16)


def _mla_qkv_kernel(h_ref, win_ref, qg_ref, kvg_ref, wq_ref, wkv_ref, tabq_ref, tabk_ref, q_ref, k_ref, v_ref):
    z = jnp.dot(h_ref[...], win_ref[...], preferred_element_type=F32)
    ckv, rope, cq = z[:, :MLA_KV_LORA], z[:, MLA_KV_LORA:MLA_KV_LORA + 2 * MLA_ROPE], z[:, MLA_KV_LORA + 2 * MLA_ROPE:]
    qa = jnp.dot(_rms_bf16(cq, qg_ref[...]), wq_ref[...], preferred_element_type=F32)
    tab = tabq_ref[...]
    for h in range(MLA_HEADS):
        q_ref[:, h * MLA_QK:(h + 1) * MLA_QK] = (qa[:, h * MLA_QK:(h + 1) * MLA_QK] * tab).astype(q_ref.dtype)
    kva = jnp.dot(_rms_bf16(ckv, kvg_ref[...]), wkv_ref[...], preferred_element_type=F32)
    prod = rope * tabk_ref[...]
    k_rot = (prod + pltpu.roll(prod, MLA_ROPE, 1)).astype(k_ref.dtype)
    hw = MLA_NOPE + MLA_V
    for h in range(MLA_HEADS):
        k_ref[:, h * MLA_QK:h * MLA_QK + MLA_NOPE] = kva[:, h * hw:h * hw + MLA_NOPE].astype(k_ref.dtype)
        k_ref[:, h * MLA_QK + MLA_NOPE:(h + 1) * MLA_QK] = k_rot
        v_ref[:, h * MLA_V:(h + 1) * MLA_V] = kva[:, h * hw + MLA_NOPE:(h + 1) * hw].astype(v_ref.dtype)


def mla_qkv(h, p, rows, seq, batch):
    tm = ROW_TILE
    d = h.shape[1]
    src, sign = _rope_swap_columns()
    w_in = p["w_in"]
    o_kv, o_rope = MLA_Q_LORA, MLA_Q_LORA + MLA_KV_LORA
    w_rope = w_in[:, o_rope:]
    w_cat = jnp.concatenate([w_in[:, o_kv:o_rope], w_rope, w_rope[:, src] * sign, w_in[:, :o_kv]], axis=1).astype(BF16)
    w_uq = p["w_uq"].reshape(MLA_Q_LORA, MLA_HEADS, MLA_NOPE + MLA_ROPE)
    w_uq_rope = w_uq[:, :, MLA_NOPE:]
    w_uq = jnp.concatenate([w_uq, w_uq_rope[:, :, src] * sign], axis=2).reshape(MLA_Q_LORA, MLA_HEADS * MLA_QK).astype(BF16)
    tab_q, tab_k = _rope_tables(seq, tm)
    lat_tiles = batch * seq // tm
    tab_map = lambda i: (jnp.where(i < lat_tiles, i % (seq // tm), seq // tm), 0)
    full = lambda a: pl.BlockSpec(a.shape, lambda i: (0, 0))
    w_ukv = p["w_ukv"].astype(BF16)
    q_norm, kv_norm = p["q_norm"].reshape(1, -1), p["kv_norm"].reshape(1, -1)
    row = lambda w: pl.BlockSpec((tm, w), lambda i: (i, 0))
    return pl.pallas_call(
        _mla_qkv_kernel,
        grid=(rows // tm,),
        in_specs=[row(d), full(w_cat), full(q_norm), full(kv_norm), full(w_uq), full(w_ukv),
                  pl.BlockSpec((tm, MLA_QK), tab_map), pl.BlockSpec((tm, 2 * MLA_ROPE), tab_map)],
        out_specs=[row(MLA_HEADS * MLA_QK), row(MLA_HEADS * MLA_QK), row(MLA_HEADS * MLA_V)],
        out_shape=[jax.ShapeDtypeStruct((rows, MLA_HEADS * MLA_QK), BF16),
                   jax.ShapeDtypeStruct((rows, MLA_HEADS * MLA_QK), BF16),
                   jax.ShapeDtypeStruct((rows, MLA_HEADS * MLA_V), BF16)],
        compiler_params=_params(1), name="mla_qkv_proj",
    )(h, w_cat, q_norm, kv_norm, w_uq, w_ukv, jnp.asarray(tab_q), jnp.asarray(tab_k))


ATTN_SUB = 256


def _dot_nt(a, b):
    return lax.dot_general(a, b, (((1,), (1,)), ((), ())), preferred_element_type=F32)


def _attention_body(q_ref, key_refs, val_refs, o_ref):
    n_sub = q_ref.shape[0] // ATTN_SUB
    rows = lambda i: slice(i * ATTN_SUB, (i + 1) * ATTN_SUB)
    scores = lambda i: [_dot_nt(q_ref[rows(i), :], k_ref[...]) for k_ref in key_refs]
    pending = scores(0)
    for i in range(n_sub):
        s = pending
        if i + 1 < n_sub:
            pending = scores(i + 1)
        m = functools.reduce(jnp.maximum, [jnp.max(x, axis=-1, keepdims=True) for x in s])
        p = [jnp.exp(x - m) for x in s]
        l = sum(jnp.sum(x, axis=-1, keepdims=True) for x in p)
        o = sum(jnp.dot(x.astype(BF16), v_ref[...], preferred_element_type=F32) for x, v_ref in zip(p, val_refs))
        o_ref[rows(i), :] = (o / l).astype(o_ref.dtype)


def _attn_kernel(with_ctx_out, ql_ref, qc_ref, kl_ref, vl_ref, kc_ref, vc_ref, *o_refs):
    _attention_body(ql_ref, (kl_ref, kc_ref), (vl_ref, vc_ref), o_refs[0])
    if with_ctx_out:
        _attention_body(qc_ref, (kc_ref,), (vc_ref,), o_refs[1])


def mla_attention(q, k, v, batch, seq, ctx_len, with_ctx_out):
    ctx_blk = batch * seq // ctx_len
    lat = lambda w: pl.BlockSpec((seq, w), lambda b, h: (b, h))
    cx = lambda w: pl.BlockSpec((ctx_len, w), lambda b, h: (ctx_blk + b, h))
    out_specs = [pl.BlockSpec((seq, MLA_V), lambda b, h: (b, h))]
    out_shape = [jax.ShapeDtypeStruct((batch * seq, MLA_HEADS * MLA_V), BF16)]
    if with_ctx_out:
        out_specs.append(pl.BlockSpec((ctx_len, MLA_V), lambda b, h: (b, h)))
        out_shape.append(jax.ShapeDtypeStruct((batch * ctx_len, MLA_HEADS * MLA_V), BF16))
    outs = pl.pallas_call(
        functools.partial(_attn_kernel, with_ctx_out),
        grid=(batch, MLA_HEADS),
        in_specs=[lat(MLA_QK), cx(MLA_QK), lat(MLA_QK), lat(MLA_V), cx(MLA_QK), cx(MLA_V)],
        out_specs=out_specs, out_shape=out_shape,
        compiler_params=_params(2), name="mla_attention",
    )(q, q, k, v, k, v)
    return (outs[0], outs[1]) if with_ctx_out else (outs[0], None)


GDN_QKV = 2 * GDN_HEADS * GDN_DK + GDN_HEADS * GDN_DV
GDN_SCALARS = 2 * GDN_HEADS
GDN_ROWS = 16
LOG2_CHUNK = int(math.log2(GDN_CHUNK))
NEG_BIG = -1e30
GDN_GROUP = 128
GDN_HEADS_PER_STEP = 2
GDN_GROUPS_PER_STEP = 4


def _gdn_in_kernel(seq_len, x_ref, w_ref, cw_ref, o_ref):
    z = jnp.dot(x_ref[...], w_ref[...], preferred_element_type=F32)
    z = _centred_conv3(z, None, cw_ref[...], None, seq_len)
    z = z * jax.nn.sigmoid(z)
    j = pl.program_id(1)
    blocks_per_kind = GDN_HEADS * GDN_DK // z.shape[1]
    q_scale = jnp.where(j < blocks_per_kind, GDN_DK ** -0.5, 1.0)
    for hh in range(z.shape[1] // GDN_DK):
        zh = z[:, hh * GDN_DK:(hh + 1) * GDN_DK]
        zn = zh * lax.rsqrt(jnp.sum(zh * zh, axis=-1, keepdims=True) + RMS_EPS) * q_scale
        o_ref[:, hh * GDN_DK:(hh + 1) * GDN_DK] = jnp.where(j < 2 * blocks_per_kind, zn, zh).astype(o_ref.dtype)


def gdn_in(h, w_qkv, conv_w, length, n_seq, row_offset, seq_len):
    k = h.shape[1]
    n = w_qkv.shape[1]
    tn = 256
    off = row_offset // length
    return pl.pallas_call(
        functools.partial(_gdn_in_kernel, seq_len),
        grid=(n_seq, n // tn),
        in_specs=[pl.BlockSpec((length, k), lambda b, j: (b + off, 0)),
                  pl.BlockSpec((k, tn), lambda b, j: (0, j)),
                  pl.BlockSpec((3, tn), lambda b, j: (0, j))],
        out_specs=pl.BlockSpec((length, tn), lambda b, j: (b, j)),
        out_shape=jax.ShapeDtypeStruct((n_seq * length, n), BF16),
        compiler_params=_params(2), name="gdn_in_proj_conv3",
    )(h, w_qkv, conv_w)


def _gdn_gates_kernel(ab_ref, alog_ref, dtb_ref, beta_ref, ecum_ref, edec_ref, cum_ref, egl_ref):
    r = lax.broadcasted_iota(jnp.int32, (GDN_CHUNK, GDN_CHUNK), 0)
    c = lax.broadcasted_iota(jnp.int32, (GDN_CHUNK, GDN_CHUNK), 1)
    prefix = jnp.where(c <= r, 1.0, 0.0)
    suffix = jnp.where(c >= r, 1.0, 0.0)
    total = jnp.ones((GDN_CHUNK, GDN_CHUNK), F32)
    lane = lax.broadcasted_iota(jnp.int32, (GDN_CHUNK, V7X_LANES), 1)
    alog, dtb = alog_ref[...], dtb_ref[...]
    for ch in range(ab_ref.shape[0] // GDN_CHUNK):
        rows = slice(ch * GDN_CHUNK, (ch + 1) * GDN_CHUNK)
        x = ab_ref[rows, :V7X_LANES] + dtb
        softplus = jnp.maximum(x, 0.0) + jnp.log(1.0 + jnp.exp(-jnp.abs(x)))
        g = -jnp.exp(alog) * softplus
        dot = lambda m: jnp.dot(m, g, precision=HIGHEST, preferred_element_type=F32)
        cum = jnp.where(lane < GDN_HEADS, dot(prefix), dot(suffix))
        g_last = dot(total)
        beta_ref[rows, :] = jax.nn.sigmoid(ab_ref[rows, V7X_LANES:])
        ecum_ref[rows, :] = jnp.exp(cum)
        edec_ref[rows, :] = jnp.exp(g_last - cum)
        cum_ref[rows, :] = cum
        egl_ref[rows, :] = jnp.exp(g_last)


def gdn_gates(ab, a_log, dt_bias):
    rows = ab.shape[0]
    tm = ROW_TILE
    pad = lambda v: jnp.pad(v.reshape(1, -1), ((0, 0), (0, V7X_LANES - GDN_SCALARS)))
    spec = pl.BlockSpec((tm, V7X_LANES), lambda i: (i, 0))
    vec = pl.BlockSpec((1, V7X_LANES), lambda i: (0, 0))
    return pl.pallas_call(
        _gdn_gates_kernel,
        grid=(rows // tm,),
        in_specs=[pl.BlockSpec((tm, 2 * V7X_LANES), lambda i: (i, 0)), vec, vec],
        out_specs=[spec] * 5,
        out_shape=[jax.ShapeDtypeStruct((rows, V7X_LANES), F32)] * 5,
        compiler_params=_params(1), name="gdn_gates",
    )(ab, pad(a_log), pad(dt_bias))


def _bdot(a, b):
    return jnp.dot(a.astype(BF16), b.astype(BF16), preferred_element_type=F32)


def _bdot_nt(a, b):
    return lax.dot_general(a.astype(BF16), b.astype(BF16), (((1,), (1,)), ((), ())), preferred_element_type=F32)


def _bdot_tn(a, b):
    return lax.dot_general(a.astype(BF16), b.astype(BF16), (((0,), (0,)), ((), ())), preferred_element_type=F32)


def _gdn_group_terms(qkv_ref, sc_ref, problems, want_out):
    ri = lax.broadcasted_iota(jnp.int32, (GDN_GROUP, GDN_GROUP), 0)
    ci = lax.broadcasted_iota(jnp.int32, (GDN_GROUP, GDN_GROUP), 1)
    same = (ri >> LOG2_CHUNK) == (ci >> LOG2_CHUNK)
    incl = [same & (ri >= ci), same & (ri <= ci)]
    strict = [same & (ri > ci), same & (ri < ci)]
    eye = jnp.where(ri == ci, 1.0, 0.0)
    pair = (ri >> 1) == (ci >> 1)
    joins = [((ri >> (lg + 1)) == (ci >> (lg + 1))) & ((ri >> lg) != (ci >> lg)) for lg in range(1, LOG2_CHUNK)]

    scalars = {}

    def load(hh, gi, d):
        rows = pl.ds(pl.multiple_of(gi * GDN_GROUP, GDN_GROUP), GDN_GROUP)
        lanes = slice(hh * GDN_DK, (hh + 1) * GDN_DK)
        if (hh, id(gi)) not in scalars:
            sc = sc_ref[hh, :, rows]
            square = jnp.concatenate([sc, jnp.zeros((GDN_GROUP - GDN_ROWS, GDN_GROUP), F32)], axis=0)
            scalars[(hh, id(gi))] = (sc, square.T)
        sc, sc_t = scalars[(hh, id(gi))]
        base = 5 * d
        return dict(
            q=qkv_ref[0][rows, lanes].astype(F32), k=qkv_ref[1][rows, lanes].astype(F32),
            v=qkv_ref[2][rows, lanes].astype(F32),
            beta=sc_t[:, base + 0:base + 1], ecum=sc_t[:, base + 1:base + 2],
            edec=sc_t[:, base + 2:base + 3], cum_c=sc_t[:, base + 3:base + 4],
            cum_r=sc[base + 3:base + 4, :], d=d, group=(hh, id(gi)))

    ps = [load(hh, gi, d) for hh, gi, d in problems]
    for p in ps:
        p["gamma"] = jnp.exp(jnp.where(incl[p["d"]], p["cum_c"] - p["cum_r"], NEG_BIG))
        p["kb"] = p["k"] * p["beta"]
    kk, qk_raw = {}, {}
    for p in ps:
        if p["group"] not in kk:
            kk[p["group"]] = _bdot_nt(p["k"], p["k"])
            if want_out:
                qk_raw[p["group"]] = _bdot_nt(p["q"], p["k"])
    for p in ps:
        p["a"] = jnp.where(strict[p["d"]], kk[p["group"]] * p["beta"] * p["gamma"], 0.0)
    for p in ps:
        p["inv"] = eye - jnp.where(pair, p["a"], 0.0)
    for join in joins:
        for p in ps:
            p["t"] = _bdot(jnp.where(join, p["a"], 0.0), p["inv"])
        for p in ps:
            p["inv"] = p["inv"] - _bdot(p["inv"], p["t"])
    for p in ps:
        p["uw"] = _bdot(p["inv"], jnp.concatenate([p["v"] * p["beta"], p["kb"] * p["ecum"]], axis=-1))
        p["kdec"] = p["k"] * p["edec"]
    if want_out:
        for p in ps:
            p["qk"] = jnp.where(incl[p["d"]], qk_raw[p["group"]] * p["gamma"], 0.0)
    out = []
    for p in ps:
        kw_t, n_t = [], []
        for c in range(GDN_GROUP // GDN_CHUNK):
            sl = slice(c * GDN_CHUNK, (c + 1) * GDN_CHUNK)
            both = _bdot_tn(p["uw"][sl], p["kdec"][sl])
            n_t.append(both[:GDN_DV])
            kw_t.append(both[GDN_DV:])
        q_eff = o_local = None
        if want_out:
            corr = _bdot(p["qk"], p["uw"])
            q_eff, o_local = p["q"] * p["ecum"] - corr[:, GDN_DV:], corr[:, :GDN_DV]
        out.append((kw_t, n_t, q_eff, o_local))
    return out


def _gdn_scan_kernel(ql_ref, kl_ref, vl_ref, qc_ref, kc_ref, vc_ref, scl_ref, scc_ref,
                     gate_ref, onorm_ref, y_ref, kw_ref, nt_ref, qeff_ref, oloc_ref, out_ref):
    n_ctx = qc_ref.shape[0] // GDN_CHUNK
    n_lat = ql_ref.shape[0] // GDN_CHUNK
    per_group = GDN_GROUP // GDN_CHUNK
    heads = range(GDN_HEADS_PER_STEP)
    chains = [(hh, d) for hh in heads for d in range(2)]

    def precompute(qkv, sc_ref, slot0, want_out):
        groups = min(GDN_GROUPS_PER_STEP, qkv[0].shape[0] // GDN_GROUP)

        def body(it, carry):
            gis = [it * groups + gg for gg in range(groups)]
            problems = [(hh, gi, d) for hh in heads for gi in gis for d in range(2)]
            terms = _gdn_group_terms(qkv, sc_ref, problems, want_out)
            for (hh, gi, d), (kw_t, n_t, q_eff, o_local) in zip(problems, terms):
                for c in range(per_group):
                    kw_ref[hh, d, slot0 + gi * per_group + c] = kw_t[c].astype(kw_ref.dtype)
                    nt_ref[hh, d, slot0 + gi * per_group + c] = n_t[c]
                if want_out:
                    rows = pl.ds(pl.multiple_of(gi * GDN_GROUP, GDN_GROUP), GDN_GROUP)
                    qeff_ref[hh, d, rows, :] = q_eff.astype(qeff_ref.dtype)
                    oloc_ref[hh, d, rows, :] = o_local
            return carry
        lax.fori_loop(0, qkv[0].shape[0] // (GDN_GROUP * groups), body, 0)

    precompute((qc_ref, kc_ref, vc_ref), scc_ref, 0, False)
    precompute((ql_ref, kl_ref, vl_ref), scl_ref, n_ctx, True)

    def chunk_rows(c):
        return pl.ds(pl.multiple_of(c * GDN_CHUNK, GDN_CHUNK), GDN_CHUNK)

    def advance(states, sc_ref, slot0, chunk_of):
        prods = [jnp.dot(st.astype(BF16), kw_ref[hh, d, slot0 + chunk_of[d]], preferred_element_type=F32)
                 for st, (hh, d) in zip(states, chains)]
        new = []
        for st, pr, (hh, d) in zip(states, prods, chains):
            c = chunk_of[d]
            egl = sc_ref[hh, 5 * d + 4:5 * d + 5, chunk_rows(c)][:, 0:1]
            new.append(egl * st - pr + nt_ref[hh, d, slot0 + c])
        return tuple(new)

    def scan_ctx(step, states):
        return advance(states, scc_ref, 0, (step, n_ctx - 1 - step))

    def scan_lat(step, states):
        chunk_of = (step, n_lat - 1 - step)
        outs = [_bdot_nt(qeff_ref[hh, d, chunk_rows(chunk_of[d]), :], st) for st, (hh, d) in zip(states, chains)]
        for o, (hh, d) in zip(outs, chains):
            rows = chunk_rows(chunk_of[d])
            out_ref[hh, d, rows, :] = o + oloc_ref[hh, d, rows, :]
        return advance(states, scl_ref, n_ctx, chunk_of)

    zero = jnp.zeros((GDN_DV, GDN_DK), F32)
    states = lax.fori_loop(0, n_ctx, scan_ctx, tuple(zero for _ in chains))
    lax.fori_loop(0, n_lat, scan_lat, states)

    onorm = onorm_ref[...]

    def finish(gi, carry):
        rows = pl.ds(pl.multiple_of(gi * GDN_GROUP, GDN_GROUP), GDN_GROUP)
        for hh in heads:
            lanes = slice(hh * GDN_DV, (hh + 1) * GDN_DV)
            o = out_ref[hh, 0, rows, :] + out_ref[hh, 1, rows, :]
            o = o * lax.rsqrt(jnp.mean(o * o, axis=-1, keepdims=True) + RMS_EPS) * onorm
            gte = gate_ref[rows, lanes].astype(F32)
            y_ref[rows, lanes] = (o * gte * jax.nn.sigmoid(gte)).astype(y_ref.dtype)
        return carry

    lax.fori_loop(0, ql_ref.shape[0] // GDN_GROUP, finish, 0)


def gdn_scan(qkv_l, qkv_c, scalars, gate, o_norm, batch, seq, ctx_len):
    n_lat, n_ctx = seq // GDN_CHUNK, ctx_len // GDN_CHUNK
    ctx_blk = batch * seq // ctx_len
    hp = GDN_HEADS_PER_STEP
    n_hp = GDN_HEADS // hp
    lat = lambda q: pl.BlockSpec((seq, hp * GDN_DK), lambda b, h, q=q: (b, q * n_hp + h))
    cx = lambda q: pl.BlockSpec((ctx_len, hp * GDN_DK), lambda b, h, q=q: (b, q * n_hp + h))
    return pl.pallas_call(
        _gdn_scan_kernel,
        grid=(batch, n_hp),
        in_specs=[lat(0), lat(1), lat(2), cx(0), cx(1), cx(2),
                  pl.BlockSpec((hp, GDN_ROWS, seq), lambda b, h: (h, 0, b)),
                  pl.BlockSpec((hp, GDN_ROWS, ctx_len), lambda b, h: (h, 0, ctx_blk + b)),
                  pl.BlockSpec((seq, hp * GDN_DV), lambda b, h: (b, h)),
                  pl.BlockSpec((1, GDN_DV), lambda b, h: (0, 0))],
        out_specs=pl.BlockSpec((seq, hp * GDN_DV), lambda b, h: (b, h)),
        out_shape=jax.ShapeDtypeStruct((batch * seq, GDN_HEADS * GDN_DV), BF16),
        scratch_shapes=[pltpu.VMEM((hp, 2, n_ctx + n_lat, GDN_DK, GDN_DK), BF16),
                        pltpu.VMEM((hp, 2, n_ctx + n_lat, GDN_DV, GDN_DK), F32),
                        pltpu.VMEM((hp, 2, seq, GDN_DK), BF16),
                        pltpu.VMEM((hp, 2, seq, GDN_DV), F32),
                        pltpu.VMEM((hp, 2, seq, GDN_DV), F32)],
        compiler_params=_params(2), name="gdn_chunk_scan",
    )(qkv_l, qkv_l, qkv_l, qkv_c, qkv_c, qkv_c, scalars, scalars, gate, o_norm.reshape(1, GDN_DV))


def gdn_mixer_core(h, p, batch, seq, ctx_len):
    t_lat = batch * seq
    w_in = p["w_in"]
    hv = GDN_HEADS * GDN_DV
    w_qkv = w_in[:, :GDN_QKV].astype(BF16)
    qkv_l = gdn_in(h, w_qkv, p["conv_w"], seq, batch, 0, seq)
    qkv_c = gdn_in(h, w_qkv, p["conv_w"], batch * ctx_len, 1, t_lat, ctx_len)
    gate = project(h, w_in[:, GDN_QKV:GDN_QKV + hv].astype(BF16), out_dtype=BF16, rows=t_lat)
    lane_pad = ((0, 0), (0, V7X_LANES - GDN_SCALARS))
    w_a = jnp.pad(w_in[:, GDN_QKV + hv:GDN_QKV + hv + GDN_SCALARS], lane_pad)
    w_b = jnp.pad(w_in[:, GDN_QKV + hv + GDN_SCALARS:], lane_pad)
    ab = project(h, jnp.concatenate([w_a, w_b], axis=1).astype(BF16))
    parts = gdn_gates(ab, p["a_log"], p["dt_bias"])
    rows = ab.shape[0]
    stacked = jnp.stack([a[:, :GDN_SCALARS] for a in parts], axis=0).reshape(5, rows, 2, GDN_HEADS)
    scalars = stacked.transpose(3, 2, 0, 1).reshape(GDN_HEADS, 10, rows)
    scalars = jnp.pad(scalars, ((0, 0), (0, GDN_ROWS - 10), (0, 0)))
    return gdn_scan(qkv_l, qkv_c, scalars, gate, p["o_norm"], batch, seq, ctx_len)


def kernel(x, c, ctx, c_ctx, mod_w, mod_b, ln_g, ln_b, ffn_w_in, ffn_w_out, hy_w_in, hy_b_in, hy_conv_w, hy_conv_b, hy_fw1, hy_fb1, hy_fw2, hy_fb2, hy_fw3, hy_fb3, hy_fw4, hy_freq, hy_skip, hy_w_out, hy_b_out, mla_w_in, mla_q_norm, mla_kv_norm, mla_w_uq, mla_w_ukv, mla_w_out, gdn_w_in, gdn_conv_w, gdn_a_log, gdn_dt_bias, gdn_o_norm, gdn_w_out):
    batch, seq, d = x.shape
    ctx_len = ctx.shape[1]
    t_lat, t_ctx = batch * seq, batch * ctx_len
    t_all = t_lat + t_ctx
    assert t_lat % ROW_TILE == 0 and t_ctx % ROW_TILE == 0 and seq % ROW_TILE == 0

    n_mod = -(-(batch + 1) // 8) * 8
    cc = jnp.concatenate([c, c_ctx[None], jnp.zeros((n_mod - batch - 1, d), F32)], axis=0)
    mods = adaln_all(cc, mod_w, mod_b).reshape(DEPTH, n_mod, 6, 1, d).transpose(0, 2, 1, 3, 4)
    zero_bias = jnp.zeros((d,), F32)

    ffn_w_in_bf16, ffn_w_out_bf16 = ffn_w_in.astype(BF16), ffn_w_out.astype(BF16)
    s, h = assemble_stream(x.reshape(t_lat, d), ctx.reshape(t_ctx, d), mods[0, 1], mods[0, 0], seq, batch)
    for i in range(DEPTH):
        kind, j = i % N_MIXERS, i // N_MIXERS
        ctx_out = any(l % N_MIXERS != MIXER_HYENA for l in range(i + 1, DEPTH))
        rows_out = t_all if ctx_out else t_lat
        if kind == MIXER_HYENA:
            p = {"w_in": hy_w_in[j].astype(BF16), "b_in": hy_b_in[j], "conv_w": hy_conv_w[j], "conv_b": hy_conv_b[j],
                 "fw1": hy_fw1[j], "fb1": hy_fb1[j], "fw2": hy_fw2[j], "fb2": hy_fb2[j],
                 "fw3": hy_fw3[j], "fb3": hy_fb3[j], "fw4": hy_fw4[j], "freq": hy_freq[j], "skip": hy_skip[j]}
            y = hyena_mixer_core(h, p, seq, batch, 0, seq)
            y_ctx = hyena_mixer_core(h, p, ctx_len, batch, t_lat, t_ctx) if ctx_out else None
            w_out, b_out = hy_w_out, hy_b_out[j]
        elif kind == MIXER_MLA:
            p = {"w_in": mla_w_in[j], "q_norm": mla_q_norm[j], "kv_norm": mla_kv_norm[j],
                 "w_uq": mla_w_uq[j], "w_ukv": mla_w_ukv[j]}
            q, k, v = mla_qkv(h, p, t_all, seq, batch)
            y, y_ctx = mla_attention(q, k, v, batch, seq, ctx_len, ctx_out)
            w_out, b_out = mla_w_out, zero_bias
        else:
            assert not ctx_out
            p = {"w_in": gdn_w_in[j], "conv_w": gdn_conv_w[j], "a_log": gdn_a_log[j],
                 "dt_bias": gdn_dt_bias[j], "o_norm": gdn_o_norm[j]}
            y, y_ctx = gdn_mixer_core(h, p, batch, seq, ctx_len), None
            w_out, b_out = gdn_w_out, zero_bias
        s, h = out_ln(y, w_out.astype(BF16), j, b_out, s, mods[i, 2], ln_g[i, 0], ln_b[i, 0], mods[i, 4], mods[i, 3],
                      rows_out, seq, batch, ROW_TILE, x_ctx=y_ctx)
        a = ffn_in(h, ffn_w_in_bf16, i, rows_out)
        nxt = min(i + 1, DEPTH - 1)
        s, h = out_ln(a, ffn_w_out_bf16, i, zero_bias, s, mods[i, 5], ln_g[i, 1], ln_b[i, 1],
                      mods[nxt, 1], mods[nxt, 0], rows_out, seq, batch, ROW_TILE // 2)
    return s[:t_lat].reshape(batch, seq, d)
```

```python
import functools
import math

import numpy as np

import jax
import jax.numpy as jnp
from jax import lax
from jax.experimental import pallas as pl
from jax.experimental.pallas import tpu as pltpu

F32 = jnp.float32
BF16 = jnp.bfloat16
HIGHEST = lax.Precision.HIGHEST

D_MODEL = 1024
DEPTH = 4
GRID_W = 64
N_MIXERS = 3
MIXER_HYENA = 0
MIXER_MLA = 1
DEEPNORM_ALPHA = (2 * DEPTH) ** 0.25
LN_EPS = 1e-5
RMS_EPS = 1e-6
D_FF = -(-8 * D_MODEL // (3 * 256)) * 256
HYENA_EMB = 33
HYENA_FILTER_HIDDEN = 64
HYENA_DECAY_TARGET = 1e-2
HYENA_FAST_DECAY_PCT = 0.3
HYENA_SLOW_DECAY_PCT = 1.5
MLA_HEADS = 8
MLA_Q_LORA = 384
MLA_KV_LORA = 256
MLA_NOPE = 128
MLA_ROPE = 64
MLA_V = 128
MLA_SCALE = (MLA_NOPE + MLA_ROPE) ** -0.5
MLA_QK = 256
ROPE_THETA = 10000.0
GDN_HEADS = 8
GDN_DK = 128
GDN_DV = 128
GDN_CHUNK = 128

V7X_LANES = 128
V7X_MXU_DIM = 256
V7X_VMEM_LIMIT_BYTES = 56 * 1024 * 1024
ROW_TILE = 1024


def _params(n_axes):
    return pltpu.CompilerParams(dimension_semantics=("arbitrary",) * n_axes,
                                vmem_limit_bytes=V7X_VMEM_LIMIT_BYTES)


def _mod_row_map(tile, seq, batch):
    return lambda i: (jnp.minimum(i * tile // seq, batch), 0, 0)


def _adaln_kernel(c_ref, w_ref, b_ref, o_ref):
    x = c_ref[...]
    x = (x * jax.nn.sigmoid(x)).astype(BF16)
    o_ref[0] = jnp.dot(x, w_ref[0].astype(BF16), preferred_element_type=F32) + b_ref[0]


def adaln_all(cc, mod_w, mod_b):
    r, d = cc.shape
    depth, _, n = mod_w.shape
    tn = 1536
    return pl.pallas_call(
        _adaln_kernel,
        grid=(depth, n // tn),
        in_specs=[pl.BlockSpec((r, d), lambda l, j: (0, 0)),
                  pl.BlockSpec((1, d, tn), lambda l, j: (l, 0, j)),
                  pl.BlockSpec((1, 1, tn), lambda l, j: (l, 0, j))],
        out_specs=pl.BlockSpec((1, r, tn), lambda l, j: (l, 0, j)),
        out_shape=jax.ShapeDtypeStruct((depth, r, n), F32),
        compiler_params=_params(2), name="adaln_table",
    )(cc, mod_w, mod_b.reshape(depth, 1, n))


def _assemble_kernel(n_lat_tiles, x_ref, c_ref, scale_ref, shift_ref, s_ref, h_ref):
    s = jnp.where(pl.program_id(0) < n_lat_tiles, x_ref[...], c_ref[...])
    s_ref[...] = s
    h_ref[...] = (s * (1.0 + scale_ref[0]) + shift_ref[0]).astype(h_ref.dtype)


def assemble_stream(x2d, ctx2d, scale, shift, seq, batch):
    t_lat, d = x2d.shape
    tm = ROW_TILE
    n_lat, n_ctx = t_lat // tm, ctx2d.shape[0] // tm
    rows = t_lat + ctx2d.shape[0]
    mod_spec = pl.BlockSpec((1, 1, d), _mod_row_map(tm, seq, batch))
    row_spec = pl.BlockSpec((tm, d), lambda i: (i, 0))
    return pl.pallas_call(
        functools.partial(_assemble_kernel, n_lat),
        grid=(n_lat + n_ctx,),
        in_specs=[pl.BlockSpec((tm, d), lambda i: (jnp.minimum(i, n_lat - 1), 0)),
                  pl.BlockSpec((tm, d), lambda i: (jnp.maximum(i - n_lat, 0), 0)), mod_spec, mod_spec],
        out_specs=[row_spec, row_spec],
        out_shape=[jax.ShapeDtypeStruct((rows, d), F32), jax.ShapeDtypeStruct((rows, d), BF16)],
        compiler_params=_params(1), name="assemble_modulate",
    )(x2d, ctx2d, scale, shift)


def _proj_kernel(x_ref, w_ref, o_ref):
    o_ref[...] = jnp.dot(x_ref[...], w_ref[...], preferred_element_type=F32).astype(o_ref.dtype)


def project(x, w, out_dtype=F32, tm=ROW_TILE, tn=None, rows=None):
    m, k = x.shape
    m = m if rows is None else rows
    n = w.shape[1]
    tn = n if tn is None else tn
    return pl.pallas_call(
        _proj_kernel,
        grid=(m // tm, n // tn),
        in_specs=[pl.BlockSpec((tm, k), lambda i, j: (i, 0)),
                  pl.BlockSpec((k, tn), lambda i, j: (0, j))],
        out_specs=pl.BlockSpec((tm, tn), lambda i, j: (i, j)),
        out_shape=jax.ShapeDtypeStruct((m, n), out_dtype),
        compiler_params=_params(2), name="projection",
    )(x, w)


def _out_ln_kernel(x_ref, *rest):
    _out_ln_body(x_ref[...], *rest)


def _out_ln_split_kernel(n_lat_tiles, x_ref, xc_ref, *rest):
    _out_ln_body(jnp.where(pl.program_id(0) < n_lat_tiles, x_ref[...], xc_ref[...]), *rest)


def _out_ln_body(x, w_ref, b_ref, s_ref, gate_ref, lng_ref, lnb_ref, scale_ref, shift_ref, s_out_ref, h_out_ref):
    y = jnp.dot(x.astype(BF16), w_ref[0], preferred_element_type=F32) + b_ref[...]
    z = DEEPNORM_ALPHA * s_ref[...] + gate_ref[0] * y
    mu = jnp.mean(z, axis=-1, keepdims=True)
    zc = z - mu
    var = jnp.mean(zc * zc, axis=-1, keepdims=True)
    sn = zc * lax.rsqrt(var + LN_EPS) * lng_ref[...] + lnb_ref[...]
    s_out_ref[...] = sn
    h_out_ref[...] = (sn * (1.0 + scale_ref[0]) + shift_ref[0]).astype(h_out_ref.dtype)


def out_ln(x, w, layer, b, s, gate, ln_g, ln_b, scale, shift, rows, seq, batch, tm, x_ctx=None):
    k = x.shape[1]
    d = w.shape[2]
    mod_spec = pl.BlockSpec((1, 1, d), _mod_row_map(tm, seq, batch))
    vec_spec = pl.BlockSpec((1, d), lambda i: (0, 0))
    row_spec = pl.BlockSpec((tm, d), lambda i: (i, 0))
    if x_ctx is None:
        body, x_specs, x_args = _out_ln_kernel, [pl.BlockSpec((tm, k), lambda i: (i, 0))], (x,)
    else:
        n_lat = x.shape[0] // tm
        body = functools.partial(_out_ln_split_kernel, n_lat)
        x_specs = [pl.BlockSpec((tm, k), lambda i: (jnp.minimum(i, n_lat - 1), 0)),
                   pl.BlockSpec((tm, k), lambda i: (jnp.maximum(i - n_lat, 0), 0))]
        x_args = (x, x_ctx)
    return pl.pallas_call(
        body,
        grid=(rows // tm,),
        in_specs=x_specs + [pl.BlockSpec((1, k, d), lambda i: (layer, 0, 0)),
                            vec_spec, row_spec, mod_spec, vec_spec, vec_spec, mod_spec, mod_spec],
        out_specs=[row_spec, row_spec],
        out_shape=[jax.ShapeDtypeStruct((rows, d), F32), jax.ShapeDtypeStruct((rows, d), BF16)],
        compiler_params=_params(1), name="out_proj_ln",
    )(*x_args, w, b.reshape(1, d), s, gate, ln_g.reshape(1, d), ln_b.reshape(1, d), scale, shift)


def _ffn_in_kernel(x_ref, w_ref, a_ref):
    x = x_ref[...]
    dff = a_ref.shape[1]
    for c in range(dff // V7X_MXU_DIM):
        lo = c * V7X_MXU_DIM
        g = jnp.dot(x, w_ref[0, :, lo:lo + V7X_MXU_DIM], preferred_element_type=F32)
        u = jnp.dot(x, w_ref[0, :, dff + lo:dff + lo + V7X_MXU_DIM], preferred_element_type=F32)
        a_ref[:, lo:lo + V7X_MXU_DIM] = (g * jax.nn.sigmoid(g) * u).astype(a_ref.dtype)


def ffn_in(h, w_in, layer, rows):
    k = h.shape[1]
    dff = w_in.shape[2] // 2
    tm = ROW_TILE
    return pl.pallas_call(
        _ffn_in_kernel,
        grid=(rows // tm,),
        in_specs=[pl.BlockSpec((tm, k), lambda i: (i, 0)),
                  pl.BlockSpec((1, k, 2 * dff), lambda i: (layer, 0, 0))],
        out_specs=pl.BlockSpec((tm, dff), lambda i: (i, 0)),
        out_shape=jax.ShapeDtypeStruct((rows, dff), BF16),
        compiler_params=_params(1), name="ffn_swiglu_in",
    )(h, w_in)


V7X_SUBLANES = 8


def _centred_conv3(z, bias, cw, const, seq_len):
    length = z.shape[0]
    c0, c1, c2 = cw[0:1], cw[1:2], cw[2:3]
    if seq_len != length:
        if bias is not None:
            z = z + bias
        pos = lax.broadcasted_iota(jnp.int32, (length, 1), 0) & (seq_len - 1)
        prev = jnp.where(pos == 0, 0.0, pltpu.roll(z, 1, 0))
        nxt = jnp.where(pos == seq_len - 1, 0.0, pltpu.roll(z, length - 1, 0))
        out = prev * c0 + z * c1 + nxt * c2
        return out if const is None else out + const
    out = pltpu.roll(z, 1, 0) * c0 + z * c1 + pltpu.roll(z, length - 1, 0) * c2
    wrapped_last, wrapped_first = z[length - 1:length], z[0:1]
    if bias is not None:
        out = out + (const + bias * (c0 + c1 + c2))
        wrapped_last, wrapped_first = wrapped_last + bias, wrapped_first + bias
    e = V7X_SUBLANES
    r = lax.broadcasted_iota(jnp.int32, (e, 1), 0)
    top = jnp.where(r == 0, out[0:e] - wrapped_last * c0, out[0:e])
    bot = jnp.where(r == e - 1, out[length - e:] - wrapped_first * c2, out[length - e:])
    return jnp.concatenate([top, out[e:length - e], bot], axis=0)


def _hyena_in_kernel(seq_len, x_ref, w0_ref, w1_ref, w2_ref, b0_ref, b1_ref, b2_ref, cw0_ref, cw1_ref, cw2_ref,
                     cb0_ref, cb1_ref, cb2_ref, x0_ref, u_ref):
    x = x_ref[...]

    def branch(w_ref, b_ref, cw_ref, cb_ref):
        z = jnp.dot(x, w_ref[...], preferred_element_type=F32)
        return _centred_conv3(z, b_ref[...], cw_ref[...], cb_ref[...], seq_len)

    x0 = branch(w0_ref, b0_ref, cw0_ref, cb0_ref)
    x1 = branch(w1_ref, b1_ref, cw1_ref, cb1_ref)
    v = branch(w2_ref, b2_ref, cw2_ref, cb2_ref)
    x0_ref[...] = x0.astype(x0_ref.dtype)
    u_ref[...] = (x1 * v).astype(u_ref.dtype)


def hyena_in(h, w_in, b_in, conv_w, conv_b, length, n_seq, row_offset, seq_len):
    k = h.shape[1]
    d = w_in.shape[1] // 3
    tn = 256
    nb = d // tn
    off = row_offset // length
    w_specs = [pl.BlockSpec((k, tn), lambda b, j, q=q: (0, j + q * nb)) for q in range(3)]
    v_specs = [pl.BlockSpec((1, tn), lambda b, j, q=q: (0, j + q * nb)) for q in range(3)]
    cw_specs = [pl.BlockSpec((3, tn), lambda b, j, q=q: (0, j + q * nb)) for q in range(3)]
    out_spec = pl.BlockSpec((length, tn), lambda b, j: (b, j))
    b2, cb2 = b_in.reshape(1, 3 * d), conv_b.reshape(1, 3 * d)
    return pl.pallas_call(
        functools.partial(_hyena_in_kernel, seq_len),
        grid=(n_seq, nb),
        in_specs=[pl.BlockSpec((length, k), lambda b, j: (b + off, 0))] + w_specs + v_specs + cw_specs + v_specs,
        out_specs=[out_spec, out_spec],
        out_shape=[jax.ShapeDtypeStruct((n_seq * length, d), BF16)] * 2,
        compiler_params=_params(2), name="hyena_in_proj_conv3",
    )(h, w_in, w_in, w_in, b2, b2, b2, conv_w, conv_w, conv_w, cb2, cb2, cb2)


def _hyena_filter_kernel(feat_ref, w1_ref, b1_ref, w2_ref, b2_ref, w3_ref, b3_ref, w4_ref, freq_ref, win_ref, k_ref):
    freq = freq_ref[...]
    z = jnp.sin(freq * (jnp.dot(feat_ref[...], w1_ref[...], precision=HIGHEST, preferred_element_type=F32) + b1_ref[...]))
    z = jnp.sin(freq * (jnp.dot(z, w2_ref[...], precision=HIGHEST, preferred_element_type=F32) + b2_ref[...]))
    z = jnp.sin(freq * (jnp.dot(z, w3_ref[...], precision=HIGHEST, preferred_element_type=F32) + b3_ref[...]))
    taps = jnp.dot(z, w4_ref[...], precision=HIGHEST, preferred_element_type=F32)
    d = win_ref.shape[1]
    win = win_ref[...]
    k_fwd = taps[:, :d] * win
    k_bwd = taps[:, d:] * win
    tile = k_fwd.shape[0]
    first = (lax.broadcasted_iota(jnp.int32, (tile, 1), 0) + pl.program_id(0) * tile) == 0
    k_ref[:, :d] = (k_fwd + k_bwd).astype(k_ref.dtype)
    k_ref[:, d:] = jnp.where(first, 0.0, k_bwd - k_fwd).astype(k_ref.dtype)


def _hyena_features(length):
    t01 = np.linspace(0.0, 1.0, length, dtype=np.float32)[:, None]
    bands = (HYENA_EMB - 1) // 2
    w = (np.float32(2.0 * math.pi / length) * np.arange(length, dtype=np.float32)).astype(np.float32)
    f = np.linspace(1e-4, bands - 1, bands, dtype=np.float32)
    ang = (w[:, None] * f[None, :]).astype(np.float32)
    feat = np.concatenate([t01, np.cos(ang), -np.sin(ang)], axis=-1).astype(np.float32)
    return np.pad(feat, ((0, 0), (0, V7X_LANES - HYENA_EMB)))


def _hyena_window(length):
    max_decay = math.log(HYENA_DECAY_TARGET) / HYENA_FAST_DECAY_PCT
    min_decay = math.log(HYENA_DECAY_TARGET) / HYENA_SLOW_DECAY_PCT
    deltas = np.abs(np.linspace(min_decay, max_decay, D_MODEL, dtype=np.float32))
    t = np.linspace(0.0, 1.0, length, dtype=np.float32)
    return np.exp(-t[:, None] * deltas[None, :]).astype(np.float32)


def hyena_filter_taps(length, p):
    hid = V7X_LANES
    pad_h = hid - HYENA_FILTER_HIDDEN
    padc = lambda a: jnp.pad(a, ((0, 0), (0, pad_h)))
    w1 = jnp.pad(p["fw1"], ((0, V7X_LANES - HYENA_EMB), (0, pad_h)))
    w2 = jnp.pad(p["fw2"], ((0, pad_h), (0, pad_h)))
    w3 = jnp.pad(p["fw3"], ((0, pad_h), (0, pad_h)))
    w4 = jnp.pad(p["fw4"], ((0, pad_h), (0, 0)))
    vec = lambda a: padc(a.reshape(1, -1))
    args = (jnp.asarray(_hyena_features(length)), w1, vec(p["fb1"]), w2, vec(p["fb2"]), w3, vec(p["fb3"]), w4,
            vec(p["freq"]), jnp.asarray(_hyena_window(length)))
    tl = min(256, length)
    full = lambda a: pl.BlockSpec(a.shape, lambda i: (0, 0))
    return pl.pallas_call(
        _hyena_filter_kernel,
        grid=(length // tl,),
        in_specs=[pl.BlockSpec((tl, hid), lambda i: (i, 0))] + [full(a) for a in args[1:9]]
                 + [pl.BlockSpec((tl, D_MODEL), lambda i: (i, 0))],
        out_specs=pl.BlockSpec((tl, 2 * D_MODEL), lambda i: (i, 0)),
        out_shape=jax.ShapeDtypeStruct((length, 2 * D_MODEL), BF16),
        compiler_params=_params(1), name="hyena_filter",
    )(*args)


@functools.lru_cache(maxsize=None)
def _negacyclic_dft(length):
    n = 2 * length
    k = np.arange(length, dtype=np.int64)
    turns = ((2 * k[:, None] + 1) * k[None, :]) % (2 * n)
    ang = turns.astype(np.float64) * (2.0 * math.pi / (2 * n))
    c, s = np.cos(ang), np.sin(ang)
    inv = 2.0 / n
    return c, s, (c.T * inv), (s.T * inv)


def _dft_spec_kernel(c_ref, s_ref, taps_ref, h_ref):
    d = h_ref.shape[2]
    h_ref[0] = jnp.dot(c_ref[...], taps_ref[:, :d], preferred_element_type=F32)
    h_ref[1] = jnp.dot(s_ref[...], taps_ref[:, d:], preferred_element_type=F32)


def _dft_fwd_kernel(c_ref, s_ref, u_ref, hr_ref, hi_ref, y_ref):
    u = u_ref[...]
    ur = jnp.dot(c_ref[...], u, preferred_element_type=F32)
    us = jnp.dot(s_ref[...], u, preferred_element_type=F32)
    hr, hi = hr_ref[0], hi_ref[0]
    y_ref[0, 0] = (ur * hr + us * hi).astype(y_ref.dtype)
    y_ref[0, 1] = (us * hr - ur * hi).astype(y_ref.dtype)


def _dft_inv_kernel(ct_ref, st_ref, y_ref, u_ref, x0_ref, skip_ref, o_ref):
    y = (jnp.dot(ct_ref[...], y_ref[0, 0], preferred_element_type=F32)
         + jnp.dot(st_ref[...], y_ref[0, 1], preferred_element_type=F32))
    u = u_ref[...].astype(F32)
    o_ref[...] = ((y + u * skip_ref[...]) * x0_ref[...].astype(F32)).astype(o_ref.dtype)


def hyena_long_conv(x0, u, taps, skip, length, n_seq):
    d = u.shape[1]
    c, s, ct, st = (jnp.asarray(a, dtype=BF16) for a in _negacyclic_dft(length))
    tf = min(512, length)
    nf = length // tf
    spec = pl.pallas_call(
        _dft_spec_kernel,
        grid=(nf,),
        in_specs=[pl.BlockSpec((tf, length), lambda i: (i, 0)),
                  pl.BlockSpec((tf, length), lambda i: (i, 0)),
                  pl.BlockSpec((length, 2 * d), lambda i: (0, 0))],
        out_specs=pl.BlockSpec((2, tf, d), lambda i: (0, i, 0)),
        out_shape=jax.ShapeDtypeStruct((2, length, d), F32),
        compiler_params=_params(1), name="hyena_filter_spectrum",
    )(c, s, taps)
    y = pl.pallas_call(
        _dft_fwd_kernel,
        grid=(n_seq, nf),
        in_specs=[pl.BlockSpec((tf, length), lambda b, i: (i, 0)),
                  pl.BlockSpec((tf, length), lambda b, i: (i, 0)),
                  pl.BlockSpec((length, d), lambda b, i: (b, 0)),
                  pl.BlockSpec((1, tf, d), lambda b, i: (0, i, 0)),
                  pl.BlockSpec((1, tf, d), lambda b, i: (1, i, 0))],
        out_specs=pl.BlockSpec((1, 2, tf, d), lambda b, i: (b, 0, i, 0)),
        out_shape=jax.ShapeDtypeStruct((n_seq, 2, length, d), BF16),
        compiler_params=_params(2), name="hyena_dft_forward",
    )(c, s, u, spec, spec)
    return pl.pallas_call(
        _dft_inv_kernel,
        grid=(n_seq, nf),
        in_specs=[pl.BlockSpec((tf, length), lambda b, i: (i, 0)),
                  pl.BlockSpec((tf, length), lambda b, i: (i, 0)),
                  pl.BlockSpec((1, 2, length, d), lambda b, i: (b, 0, 0, 0)),
                  pl.BlockSpec((tf, d), lambda b, i: (b * nf + i, 0)),
                  pl.BlockSpec((tf, d), lambda b, i: (b * nf + i, 0)),
                  pl.BlockSpec((1, d), lambda b, i: (0, 0))],
        out_specs=pl.BlockSpec((tf, d), lambda b, i: (b * nf + i, 0)),
        out_shape=jax.ShapeDtypeStruct((n_seq * length, d), BF16),
        compiler_params=_params(2), name="hyena_dft_inverse",
    )(ct, st, y, u, x0, skip.reshape(1, d))


def hyena_mixer_core(h, p, length, n_seq, row_offset, tile_rows):
    x0, u = hyena_in(h, p["w_in"], p["b_in"], p["conv_w"], p["conv_b"], tile_rows, n_seq * length // tile_rows,
                     row_offset, length)
    taps = hyena_filter_taps(length, p)
    return hyena_long_conv(x0, u, taps, p["skip"], length, n_seq)


def _rope_swap_columns():
    half, quarter = MLA_ROPE // 2, MLA_ROPE // 4
    r = np.arange(MLA_ROPE)
    within = r % half
    src = np.where(within < quarter, r + quarter, r - quarter)
    sign = np.where(within < quarter, -1.0, 1.0).astype(np.float32)
    return src, sign


@functools.lru_cache(maxsize=None)
def _rope_tables(seq, ctx_rows):
    half, quarter = MLA_ROPE // 2, MLA_ROPE // 4
    inv_freq = (np.float32(ROPE_THETA) ** (-np.arange(quarter, dtype=np.float32) / quarter)).astype(np.float32)
    t = np.arange(seq)
    row, col = (t // GRID_W).astype(np.float32), (t % GRID_W).astype(np.float32)
    r = np.arange(MLA_ROPE)
    pos = np.where((r // half)[None, :] == 0, row[:, None], col[:, None]).astype(np.float32)
    ang = (pos * inv_freq[r % quarter][None, :]).astype(np.float32).astype(np.float64)
    cos = np.concatenate([np.cos(ang), np.ones((ctx_rows, MLA_ROPE))], axis=0)
    sin = np.concatenate([np.sin(ang), np.zeros((ctx_rows, MLA_ROPE))], axis=0)
    ones = np.ones((seq + ctx_rows, MLA_NOPE))
    tab_q = (np.concatenate([ones, cos, sin], axis=1) * MLA_SCALE).astype(np.float32)
    tab_k = np.concatenate([cos, sin], axis=1).astype(np.float32)
    return tab_q, tab_k


def _rms_bf16(x, g):
    return (x * lax.rsqrt(jnp.mean(x * x, axis=-1, keepdims=True) + RMS_EPS) * g).astype(BF16)


def _mla_qkv_kernel(h_ref, win_ref, qg_ref, kvg_ref, wq_ref, wkv_ref, tabq_ref, tabk_ref, q_ref, k_ref, v_ref):
    z = jnp.dot(h_ref[...], win_ref[...], preferred_element_type=F32)
    ckv, rope, cq = z[:, :MLA_KV_LORA], z[:, MLA_KV_LORA:MLA_KV_LORA + 2 * MLA_ROPE], z[:, MLA_KV_LORA + 2 * MLA_ROPE:]
    qa = jnp.dot(_rms_bf16(cq, qg_ref[...]), wq_ref[...], preferred_element_type=F32)
    tab = tabq_ref[...]
    for h in range(MLA_HEADS):
        q_ref[:, h * MLA_QK:(h + 1) * MLA_QK] = (qa[:, h * MLA_QK:(h + 1) * MLA_QK] * tab).astype(q_ref.dtype)
    kva = jnp.dot(_rms_bf16(ckv, kvg_ref[...]), wkv_ref[...], preferred_element_type=F32)
    prod = rope * tabk_ref[...]
    k_rot = (prod + pltpu.roll(prod, MLA_ROPE, 1)).astype(k_ref.dtype)
    hw = MLA_NOPE + MLA_V
    for h in range(MLA_HEADS):
        k_ref[:, h * MLA_QK:h * MLA_QK + MLA_NOPE] = kva[:, h * hw:h * hw + MLA_NOPE].astype(k_ref.dtype)
        k_ref[:, h * MLA_QK + MLA_NOPE:(h + 1) * MLA_QK] = k_rot
        v_ref[:, h * MLA_V:(h + 1) * MLA_V] = kva[:, h * hw + MLA_NOPE:(h + 1) * hw].astype(v_ref.dtype)


def mla_qkv(h, p, rows, seq, batch):
    tm = ROW_TILE
    d = h.shape[1]
    src, sign = _rope_swap_columns()
    w_in = p["w_in"]
    o_kv, o_rope = MLA_Q_LORA, MLA_Q_LORA + MLA_KV_LORA
    w_rope = w_in[:, o_rope:]
    w_cat = jnp.concatenate([w_in[:, o_kv:o_rope], w_rope, w_rope[:, src] * sign, w_in[:, :o_kv]], axis=1).astype(BF16)
    w_uq = p["w_uq"].reshape(MLA_Q_LORA, MLA_HEADS, MLA_NOPE + MLA_ROPE)
    w_uq_rope = w_uq[:, :, MLA_NOPE:]
    w_uq = jnp.concatenate([w_uq, w_uq_rope[:, :, src] * sign], axis=2).reshape(MLA_Q_LORA, MLA_HEADS * MLA_QK).astype(BF16)
    tab_q, tab_k = _rope_tables(seq, tm)
    lat_tiles = batch * seq // tm
    tab_map = lambda i: (jnp.where(i < lat_tiles, i % (seq // tm), seq // tm), 0)
    full = lambda a: pl.BlockSpec(a.shape, lambda i: (0, 0))
    w_ukv = p["w_ukv"].astype(BF16)
    q_norm, kv_norm = p["q_norm"].reshape(1, -1), p["kv_norm"].reshape(1, -1)
    row = lambda w: pl.BlockSpec((tm, w), lambda i: (i, 0))
    return pl.pallas_call(
        _mla_qkv_kernel,
        grid=(rows // tm,),
        in_specs=[row(d), full(w_cat), full(q_norm), full(kv_norm), full(w_uq), full(w_ukv),
                  pl.BlockSpec((tm, MLA_QK), tab_map), pl.BlockSpec((tm, 2 * MLA_ROPE), tab_map)],
        out_specs=[row(MLA_HEADS * MLA_QK), row(MLA_HEADS * MLA_QK), row(MLA_HEADS * MLA_V)],
        out_shape=[jax.ShapeDtypeStruct((rows, MLA_HEADS * MLA_QK), BF16),
                   jax.ShapeDtypeStruct((rows, MLA_HEADS * MLA_QK), BF16),
                   jax.ShapeDtypeStruct((rows, MLA_HEADS * MLA_V), BF16)],
        compiler_params=_params(1), name="mla_qkv_proj",
    )(h, w_cat, q_norm, kv_norm, w_uq, w_ukv, jnp.asarray(tab_q), jnp.asarray(tab_k))


ATTN_SUB = 256


def _dot_nt(a, b):
    return lax.dot_general(a, b, (((1,), (1,)), ((), ())), preferred_element_type=F32)


def _attention_body(q_ref, key_refs, val_refs, o_ref):
    n_sub = q_ref.shape[0] // ATTN_SUB
    rows = lambda i: slice(i * ATTN_SUB, (i + 1) * ATTN_SUB)
    scores = lambda i: [_dot_nt(q_ref[rows(i), :], k_ref[...]) for k_ref in key_refs]
    pending = scores(0)
    for i in range(n_sub):
        s = pending
        if i + 1 < n_sub:
            pending = scores(i + 1)
        m = functools.reduce(jnp.maximum, [jnp.max(x, axis=-1, keepdims=True) for x in s])
        p = [jnp.exp(x - m) for x in s]
        l = sum(jnp.sum(x, axis=-1, keepdims=True) for x in p)
        o = sum(jnp.dot(x.astype(BF16), v_ref[...], preferred_element_type=F32) for x, v_ref in zip(p, val_refs))
        o_ref[rows(i), :] = (o / l).astype(o_ref.dtype)


def _attn_kernel(with_ctx_out, ql_ref, qc_ref, kl_ref, vl_ref, kc_ref, vc_ref, *o_refs):
    _attention_body(ql_ref, (kl_ref, kc_ref), (vl_ref, vc_ref), o_refs[0])
    if with_ctx_out:
        _attention_body(qc_ref, (kc_ref,), (vc_ref,), o_refs[1])


def mla_attention(q, k, v, batch, seq, ctx_len, with_ctx_out):
    ctx_blk = batch * seq // ctx_len
    lat = lambda w: pl.BlockSpec((seq, w), lambda b, h: (b, h))
    cx = lambda w: pl.BlockSpec((ctx_len, w), lambda b, h: (ctx_blk + b, h))
    out_specs = [pl.BlockSpec((seq, MLA_V), lambda b, h: (b, h))]
    out_shape = [jax.ShapeDtypeStruct((batch * seq, MLA_HEADS * MLA_V), BF16)]
    if with_ctx_out:
        out_specs.append(pl.BlockSpec((ctx_len, MLA_V), lambda b, h: (b, h)))
        out_shape.append(jax.ShapeDtypeStruct((batch * ctx_len, MLA_HEADS * MLA_V), BF16))
    outs = pl.pallas_call(
        functools.partial(_attn_kernel, with_ctx_out),
        grid=(batch, MLA_HEADS),
        in_specs=[lat(MLA_QK), cx(MLA_QK), lat(MLA_QK), lat(MLA_V), cx(MLA_QK), cx(MLA_V)],
        out_specs=out_specs, out_shape=out_shape,
        compiler_params=_params(2), name="mla_attention",
    )(q, q, k, v, k, v)
    return (outs[0], outs[1]) if with_ctx_out else (outs[0], None)


GDN_QKV = 2 * GDN_HEADS * GDN_DK + GDN_HEADS * GDN_DV
GDN_SCALARS = 2 * GDN_HEADS
GDN_ROWS = 16
LOG2_CHUNK = int(math.log2(GDN_CHUNK))
NEG_BIG = -1e30
GDN_GROUP = 128
GDN_HEADS_PER_STEP = 2
GDN_GROUPS_PER_STEP = 4


def _gdn_in_kernel(seq_len, x_ref, w_ref, cw_ref, o_ref):
    z = jnp.dot(x_ref[...], w_ref[...], preferred_element_type=F32)
    z = _centred_conv3(z, None, cw_ref[...], None, seq_len)
    z = z * jax.nn.sigmoid(z)
    j = pl.program_id(1)
    blocks_per_kind = GDN_HEADS * GDN_DK // z.shape[1]
    q_scale = jnp.where(j < blocks_per_kind, GDN_DK ** -0.5, 1.0)
    for hh in range(z.shape[1] // GDN_DK):
        zh = z[:, hh * GDN_DK:(hh + 1) * GDN_DK]
        zn = zh * lax.rsqrt(jnp.sum(zh * zh, axis=-1, keepdims=True) + RMS_EPS) * q_scale
        o_ref[:, hh * GDN_DK:(hh + 1) * GDN_DK] = jnp.where(j < 2 * blocks_per_kind, zn, zh).astype(o_ref.dtype)


def gdn_in(h, w_qkv, conv_w, length, n_seq, row_offset, seq_len):
    k = h.shape[1]
    n = w_qkv.shape[1]
    tn = 256
    off = row_offset // length
    return pl.pallas_call(
        functools.partial(_gdn_in_kernel, seq_len),
        grid=(n_seq, n // tn),
        in_specs=[pl.BlockSpec((length, k), lambda b, j: (b + off, 0)),
                  pl.BlockSpec((k, tn), lambda b, j: (0, j)),
                  pl.BlockSpec((3, tn), lambda b, j: (0, j))],
        out_specs=pl.BlockSpec((length, tn), lambda b, j: (b, j)),
        out_shape=jax.ShapeDtypeStruct((n_seq * length, n), BF16),
        compiler_params=_params(2), name="gdn_in_proj_conv3",
    )(h, w_qkv, conv_w)


def _gdn_gates_kernel(ab_ref, alog_ref, dtb_ref, beta_ref, ecum_ref, edec_ref, cum_ref, egl_ref):
    r = lax.broadcasted_iota(jnp.int32, (GDN_CHUNK, GDN_CHUNK), 0)
    c = lax.broadcasted_iota(jnp.int32, (GDN_CHUNK, GDN_CHUNK), 1)
    prefix = jnp.where(c <= r, 1.0, 0.0)
    suffix = jnp.where(c >= r, 1.0, 0.0)
    total = jnp.ones((GDN_CHUNK, GDN_CHUNK), F32)
    lane = lax.broadcasted_iota(jnp.int32, (GDN_CHUNK, V7X_LANES), 1)
    alog, dtb = alog_ref[...], dtb_ref[...]
    for ch in range(ab_ref.shape[0] // GDN_CHUNK):
        rows = slice(ch * GDN_CHUNK, (ch + 1) * GDN_CHUNK)
        x = ab_ref[rows, :V7X_LANES] + dtb
        softplus = jnp.maximum(x, 0.0) + jnp.log(1.0 + jnp.exp(-jnp.abs(x)))
        g = -jnp.exp(alog) * softplus
        dot = lambda m: jnp.dot(m, g, precision=HIGHEST, preferred_element_type=F32)
        cum = jnp.where(lane < GDN_HEADS, dot(prefix), dot(suffix))
        g_last = dot(total)
        beta_ref[rows, :] = jax.nn.sigmoid(ab_ref[rows, V7X_LANES:])
        ecum_ref[rows, :] = jnp.exp(cum)
        edec_ref[rows, :] = jnp.exp(g_last - cum)
        cum_ref[rows, :] = cum
        egl_ref[rows, :] = jnp.exp(g_last)


def gdn_gates(ab, a_log, dt_bias):
    rows = ab.shape[0]
    tm = ROW_TILE
    pad = lambda v: jnp.pad(v.reshape(1, -1), ((0, 0), (0, V7X_LANES - GDN_SCALARS)))
    spec = pl.BlockSpec((tm, V7X_LANES), lambda i: (i, 0))
    vec = pl.BlockSpec((1, V7X_LANES), lambda i: (0, 0))
    return pl.pallas_call(
        _gdn_gates_kernel,
        grid=(rows // tm,),
        in_specs=[pl.BlockSpec((tm, 2 * V7X_LANES), lambda i: (i, 0)), vec, vec],
        out_specs=[spec] * 5,
        out_shape=[jax.ShapeDtypeStruct((rows, V7X_LANES), F32)] * 5,
        compiler_params=_params(1), name="gdn_gates",
    )(ab, pad(a_log), pad(dt_bias))


def _bdot(a, b):
    return jnp.dot(a.astype(BF16), b.astype(BF16), preferred_element_type=F32)


def _bdot_nt(a, b):
    return lax.dot_general(a.astype(BF16), b.astype(BF16), (((1,), (1,)), ((), ())), preferred_element_type=F32)


def _bdot_tn(a, b):
    return lax.dot_general(a.astype(BF16), b.astype(BF16), (((0,), (0,)), ((), ())), preferred_element_type=F32)


def _gdn_group_terms(qkv_ref, sc_ref, problems, want_out):
    ri = lax.broadcasted_iota(jnp.int32, (GDN_GROUP, GDN_GROUP), 0)
    ci = lax.broadcasted_iota(jnp.int32, (GDN_GROUP, GDN_GROUP), 1)
    same = (ri >> LOG2_CHUNK) == (ci >> LOG2_CHUNK)
    incl = [same & (ri >= ci), same & (ri <= ci)]
    strict = [same & (ri > ci), same & (ri < ci)]
    eye = jnp.where(ri == ci, 1.0, 0.0)
    pair = (ri >> 1) == (ci >> 1)
    joins = [((ri >> (lg + 1)) == (ci >> (lg + 1))) & ((ri >> lg) != (ci >> lg)) for lg in range(1, LOG2_CHUNK)]

    scalars = {}

    def load(hh, gi, d):
        rows = pl.ds(pl.multiple_of(gi * GDN_GROUP, GDN_GROUP), GDN_GROUP)
        lanes = slice(hh * GDN_DK, (hh + 1) * GDN_DK)
        if (hh, id(gi)) not in scalars:
            sc = sc_ref[hh, :, rows]
            square = jnp.concatenate([sc, jnp.zeros((GDN_GROUP - GDN_ROWS, GDN_GROUP), F32)], axis=0)
            scalars[(hh, id(gi))] = (sc, square.T)
        sc, sc_t = scalars[(hh, id(gi))]
        base = 5 * d
        return dict(
            q=qkv_ref[0][rows, lanes].astype(F32), k=qkv_ref[1][rows, lanes].astype(F32),
            v=qkv_ref[2][rows, lanes].astype(F32),
            beta=sc_t[:, base + 0:base + 1], ecum=sc_t[:, base + 1:base + 2],
            edec=sc_t[:, base + 2:base + 3], cum_c=sc_t[:, base + 3:base + 4],
            cum_r=sc[base + 3:base + 4, :], d=d, group=(hh, id(gi)))

    ps = [load(hh, gi, d) for hh, gi, d in problems]
    for p in ps:
        p["gamma"] = jnp.exp(jnp.where(incl[p["d"]], p["cum_c"] - p["cum_r"], NEG_BIG))
        p["kb"] = p["k"] * p["beta"]
    kk, qk_raw = {}, {}
    for p in ps:
        if p["group"] not in kk:
            kk[p["group"]] = _bdot_nt(p["k"], p["k"])
            if want_out:
                qk_raw[p["group"]] = _bdot_nt(p["q"], p["k"])
    for p in ps:
        p["a"] = jnp.where(strict[p["d"]], kk[p["group"]] * p["beta"] * p["gamma"], 0.0)
    for p in ps:
        p["inv"] = eye - jnp.where(pair, p["a"], 0.0)
    for join in joins:
        for p in ps:
            p["t"] = _bdot(jnp.where(join, p["a"], 0.0), p["inv"])
        for p in ps:
            p["inv"] = p["inv"] - _bdot(p["inv"], p["t"])
    for p in ps:
        p["uw"] = _bdot(p["inv"], jnp.concatenate([p["v"] * p["beta"], p["kb"] * p["ecum"]], axis=-1))
        p["kdec"] = p["k"] * p["edec"]
    if want_out:
        for p in ps:
            p["qk"] = jnp.where(incl[p["d"]], qk_raw[p["group"]] * p["gamma"], 0.0)
    out = []
    for p in ps:
        kw_t, n_t = [], []
        for c in range(GDN_GROUP // GDN_CHUNK):
            sl = slice(c * GDN_CHUNK, (c + 1) * GDN_CHUNK)
            both = _bdot_tn(p["uw"][sl], p["kdec"][sl])
            n_t.append(both[:GDN_DV])
            kw_t.append(both[GDN_DV:])
        q_eff = o_local = None
        if want_out:
            corr = _bdot(p["qk"], p["uw"])
            q_eff, o_local = p["q"] * p["ecum"] - corr[:, GDN_DV:], corr[:, :GDN_DV]
        out.append((kw_t, n_t, q_eff, o_local))
    return out


def _gdn_scan_kernel(ql_ref, kl_ref, vl_ref, qc_ref, kc_ref, vc_ref, scl_ref, scc_ref,
                     gate_ref, onorm_ref, y_ref, kw_ref, nt_ref, qeff_ref, oloc_ref, out_ref):
    n_ctx = qc_ref.shape[0] // GDN_CHUNK
    n_lat = ql_ref.shape[0] // GDN_CHUNK
    per_group = GDN_GROUP // GDN_CHUNK
    heads = range(GDN_HEADS_PER_STEP)
    chains = [(hh, d) for hh in heads for d in range(2)]

    def precompute(qkv, sc_ref, slot0, want_out):
        groups = min(GDN_GROUPS_PER_STEP, qkv[0].shape[0] // GDN_GROUP)

        def body(it, carry):
            gis = [it * groups + gg for gg in range(groups)]
            problems = [(hh, gi, d) for hh in heads for gi in gis for d in range(2)]
            terms = _gdn_group_terms(qkv, sc_ref, problems, want_out)
            for (hh, gi, d), (kw_t, n_t, q_eff, o_local) in zip(problems, terms):
                for c in range(per_group):
                    kw_ref[hh, d, slot0 + gi * per_group + c] = kw_t[c].astype(kw_ref.dtype)
                    nt_ref[hh, d, slot0 + gi * per_group + c] = n_t[c]
                if want_out:
                    rows = pl.ds(pl.multiple_of(gi * GDN_GROUP, GDN_GROUP), GDN_GROUP)
                    qeff_ref[hh, d, rows, :] = q_eff.astype(qeff_ref.dtype)
                    oloc_ref[hh, d, rows, :] = o_local
            return carry
        lax.fori_loop(0, qkv[0].shape[0] // (GDN_GROUP * groups), body, 0)

    precompute((qc_ref, kc_ref, vc_ref), scc_ref, 0, False)
    precompute((ql_ref, kl_ref, vl_ref), scl_ref, n_ctx, True)

    def chunk_rows(c):
        return pl.ds(pl.multiple_of(c * GDN_CHUNK, GDN_CHUNK), GDN_CHUNK)

    def advance(states, sc_ref, slot0, chunk_of):
        prods = [jnp.dot(st.astype(BF16), kw_ref[hh, d, slot0 + chunk_of[d]], preferred_element_type=F32)
                 for st, (hh, d) in zip(states, chains)]
        new = []
        for st, pr, (hh, d) in zip(states, prods, chains):
            c = chunk_of[d]
            egl = sc_ref[hh, 5 * d + 4:5 * d + 5, chunk_rows(c)][:, 0:1]
            new.append(egl * st - pr + nt_ref[hh, d, slot0 + c])
        return tuple(new)

    def scan_ctx(step, states):
        return advance(states, scc_ref, 0, (step, n_ctx - 1 - step))

    def scan_lat(step, states):
        chunk_of = (step, n_lat - 1 - step)
        outs = [_bdot_nt(qeff_ref[hh, d, chunk_rows(chunk_of[d]), :], st) for st, (hh, d) in zip(states, chains)]
        for o, (hh, d) in zip(outs, chains):
            rows = chunk_rows(chunk_of[d])
            out_ref[hh, d, rows, :] = o + oloc_ref[hh, d, rows, :]
        return advance(states, scl_ref, n_ctx, chunk_of)

    zero = jnp.zeros((GDN_DV, GDN_DK), F32)
    states = lax.fori_loop(0, n_ctx, scan_ctx, tuple(zero for _ in chains))
    lax.fori_loop(0, n_lat, scan_lat, states)

    onorm = onorm_ref[...]

    def finish(gi, carry):
        rows = pl.ds(pl.multiple_of(gi * GDN_GROUP, GDN_GROUP), GDN_GROUP)
        for hh in heads:
            lanes = slice(hh * GDN_DV, (hh + 1) * GDN_DV)
            o = out_ref[hh, 0, rows, :] + out_ref[hh, 1, rows, :]
            o = o * lax.rsqrt(jnp.mean(o * o, axis=-1, keepdims=True) + RMS_EPS) * onorm
            gte = gate_ref[rows, lanes].astype(F32)
            y_ref[rows, lanes] = (o * gte * jax.nn.sigmoid(gte)).astype(y_ref.dtype)
        return carry

    lax.fori_loop(0, ql_ref.shape[0] // GDN_GROUP, finish, 0)


def gdn_scan(qkv_l, qkv_c, scalars, gate, o_norm, batch, seq, ctx_len):
    n_lat, n_ctx = seq // GDN_CHUNK, ctx_len // GDN_CHUNK
    ctx_blk = batch * seq // ctx_len
    hp = GDN_HEADS_PER_STEP
    n_hp = GDN_HEADS // hp
    lat = lambda q: pl.BlockSpec((seq, hp * GDN_DK), lambda b, h, q=q: (b, q * n_hp + h))
    cx = lambda q: pl.BlockSpec((ctx_len, hp * GDN_DK), lambda b, h, q=q: (b, q * n_hp + h))
    return pl.pallas_call(
        _gdn_scan_kernel,
        grid=(batch, n_hp),
        in_specs=[lat(0), lat(1), lat(2), cx(0), cx(1), cx(2),
                  pl.BlockSpec((hp, GDN_ROWS, seq), lambda b, h: (h, 0, b)),
                  pl.BlockSpec((hp, GDN_ROWS, ctx_len), lambda b, h: (h, 0, ctx_blk + b)),
                  pl.BlockSpec((seq, hp * GDN_DV), lambda b, h: (b, h)),
                  pl.BlockSpec((1, GDN_DV), lambda b, h: (0, 0))],
        out_specs=pl.BlockSpec((seq, hp * GDN_DV), lambda b, h: (b, h)),
        out_shape=jax.ShapeDtypeStruct((batch * seq, GDN_HEADS * GDN_DV), BF16),
        scratch_shapes=[pltpu.VMEM((hp, 2, n_ctx + n_lat, GDN_DK, GDN_DK), BF16),
                        pltpu.VMEM((hp, 2, n_ctx + n_lat, GDN_DV, GDN_DK), F32),
                        pltpu.VMEM((hp, 2, seq, GDN_DK), BF16),
                        pltpu.VMEM((hp, 2, seq, GDN_DV), F32),
                        pltpu.VMEM((hp, 2, seq, GDN_DV), F32)],
        compiler_params=_params(2), name="gdn_chunk_scan",
    )(qkv_l, qkv_l, qkv_l, qkv_c, qkv_c, qkv_c, scalars, scalars, gate, o_norm.reshape(1, GDN_DV))


def gdn_mixer_core(h, p, batch, seq, ctx_len):
    t_lat = batch * seq
    w_in = p["w_in"]
    hv = GDN_HEADS * GDN_DV
    w_qkv = w_in[:, :GDN_QKV].astype(BF16)
    qkv_l = gdn_in(h, w_qkv, p["conv_w"], seq, batch, 0, seq)
    qkv_c = gdn_in(h, w_qkv, p["conv_w"], batch * ctx_len, 1, t_lat, ctx_len)
    gate = project(h, w_in[:, GDN_QKV:GDN_QKV + hv].astype(BF16), out_dtype=BF16, rows=t_lat)
    lane_pad = ((0, 0), (0, V7X_LANES - GDN_SCALARS))
    w_a = jnp.pad(w_in[:, GDN_QKV + hv:GDN_QKV + hv + GDN_SCALARS], lane_pad)
    w_b = jnp.pad(w_in[:, GDN_QKV + hv + GDN_SCALARS:], lane_pad)
    ab = project(h, jnp.concatenate([w_a, w_b], axis=1).astype(BF16))
    parts = gdn_gates(ab, p["a_log"], p["dt_bias"])
    rows = ab.shape[0]
    stacked = jnp.stack([a[:, :GDN_SCALARS] for a in parts], axis=0).reshape(5, rows, 2, GDN_HEADS)
    scalars = stacked.transpose(3, 2, 0, 1).reshape(GDN_HEADS, 10, rows)
    scalars = jnp.pad(scalars, ((0, 0), (0, GDN_ROWS - 10), (0, 0)))
    return gdn_scan(qkv_l, qkv_c, scalars, gate, p["o_norm"], batch, seq, ctx_len)


def kernel(x, c, ctx, c_ctx, mod_w, mod_b, ln_g, ln_b, ffn_w_in, ffn_w_out, hy_w_in, hy_b_in, hy_conv_w, hy_conv_b, hy_fw1, hy_fb1, hy_fw2, hy_fb2, hy_fw3, hy_fb3, hy_fw4, hy_freq, hy_skip, hy_w_out, hy_b_out, mla_w_in, mla_q_norm, mla_kv_norm, mla_w_uq, mla_w_ukv, mla_w_out, gdn_w_in, gdn_conv_w, gdn_a_log, gdn_dt_bias, gdn_o_norm, gdn_w_out):
    batch, seq, d = x.shape
    ctx_len = ctx.shape[1]
    t_lat, t_ctx = batch * seq, batch * ctx_len
    t_all = t_lat + t_ctx
    assert t_lat % ROW_TILE == 0 and t_ctx % ROW_TILE == 0 and seq % ROW_TILE == 0

    n_mod = -(-(batch + 1) // 8) * 8
    cc = jnp.concatenate([c, c_ctx[None], jnp.zeros((n_mod - batch - 1, d), F32)], axis=0)
    mods = adaln_all(cc, mod_w, mod_b).reshape(DEPTH, n_mod, 6, 1, d).transpose(0, 2, 1, 3, 4)
    zero_bias = jnp.zeros((d,), F32)

    ffn_w_in_bf16, ffn_w_out_bf16 = ffn_w_in.astype(BF16), ffn_w_out.astype(BF16)
    s, h = assemble_stream(x.reshape(t_lat, d), ctx.reshape(t_ctx, d), mods[0, 1], mods[0, 0], seq, batch)
    for i in range(DEPTH):
        kind, j = i % N_MIXERS, i // N_MIXERS
        ctx_out = any(l % N_MIXERS != MIXER_HYENA for l in range(i + 1, DEPTH))
        rows_out = t_all if ctx_out else t_lat
        if kind == MIXER_HYENA:
            p = {"w_in": hy_w_in[j].astype(BF16), "b_in": hy_b_in[j], "conv_w": hy_conv_w[j], "conv_b": hy_conv_b[j],
                 "fw1": hy_fw1[j], "fb1": hy_fb1[j], "fw2": hy_fw2[j], "fb2": hy_fb2[j],
                 "fw3": hy_fw3[j], "fb3": hy_fb3[j], "fw4": hy_fw4[j], "freq": hy_freq[j], "skip": hy_skip[j]}
            y = hyena_mixer_core(h, p, seq, batch, 0, seq)
            y_ctx = hyena_mixer_core(h, p, ctx_len, batch, t_lat, t_ctx) if ctx_out else None
            w_out, b_out = hy_w_out, hy_b_out[j]
        elif kind == MIXER_MLA:
            p = {"w_in": mla_w_in[j], "q_norm": mla_q_norm[j], "kv_norm": mla_kv_norm[j],
                 "w_uq": mla_w_uq[j], "w_ukv": mla_w_ukv[j]}
            q, k, v = mla_qkv(h, p, t_all, seq, batch)
            y, y_ctx = mla_attention(q, k, v, batch, seq, ctx_len, ctx_out)
            w_out, b_out = mla_w_out, zero_bias
        else:
            assert not ctx_out
            p = {"w_in": gdn_w_in[j], "conv_w": gdn_conv_w[j], "a_log": gdn_a_log[j],
                 "dt_bias": gdn_dt_bias[j], "o_norm": gdn_o_norm[j]}
            y, y_ctx = gdn_mixer_core(h, p, batch, seq, ctx_len), None
            w_out, b_out = gdn_w_out, zero_bias
        s, h = out_ln(y, w_out.astype(BF16), j, b_out, s, mods[i, 2], ln_g[i, 0], ln_b[i, 0], mods[i, 4], mods[i, 3],
                      rows_out, seq, batch, ROW_TILE, x_ctx=y_ctx)
        a = ffn_in(h, ffn_w_in_bf16, i, rows_out)
        nxt = min(i + 1, DEPTH - 1)
        s, h = out_ln(a, ffn_w_out_bf16, i, zero_bias, s, mods[i, 5], ln_g[i, 1], ln_b[i, 1],
                      mods[nxt, 1], mods[nxt, 0], rows_out, seq, batch, ROW_TILE // 2)
    return s[:t_lat].reshape(batch, seq, d)
```

```python
import functools
import math

import numpy as np

import jax
import jax.numpy as jnp
from jax import lax
from jax.experimental import pallas as pl
from jax.experimental.pallas import tpu as pltpu

F32 = jnp.float32
BF16 = jnp.bfloat16
HIGHEST = lax.Precision.HIGHEST

D_MODEL = 1024
DEPTH = 4
GRID_W = 64
N_MIXERS = 3
MIXER_HYENA = 0
MIXER_MLA = 1
DEEPNORM_ALPHA = (2 * DEPTH) ** 0.25
LN_EPS = 1e-5
RMS_EPS = 1e-6
D_FF = -(-8 * D_MODEL // (3 * 256)) * 256
HYENA_EMB = 33
HYENA_FILTER_HIDDEN = 64
HYENA_DECAY_TARGET = 1e-2
HYENA_FAST_DECAY_PCT = 0.3
HYENA_SLOW_DECAY_PCT = 1.5
MLA_HEADS = 8
MLA_Q_LORA = 384
MLA_KV_LORA = 256
MLA_NOPE = 128
MLA_ROPE = 64
MLA_V = 128
MLA_SCALE = (MLA_NOPE + MLA_ROPE) ** -0.5
MLA_QK = 256
ROPE_THETA = 10000.0
GDN_HEADS = 8
GDN_DK = 128
GDN_DV = 128
GDN_CHUNK = 128

V7X_LANES = 128
V7X_MXU_DIM = 256
V7X_VMEM_LIMIT_BYTES = 56 * 1024 * 1024
ROW_TILE = 1024


def _params(n_axes):
    return pltpu.CompilerParams(dimension_semantics=("arbitrary",) * n_axes,
                                vmem_limit_bytes=V7X_VMEM_LIMIT_BYTES)


def _mod_row_map(tile, seq, batch):
    return lambda i: (jnp.minimum(i * tile // seq, batch), 0, 0)


def _adaln_kernel(c_ref, w_ref, b_ref, o_ref):
    x = c_ref[...]
    x = (x * jax.nn.sigmoid(x)).astype(BF16)
    o_ref[0] = jnp.dot(x, w_ref[0].astype(BF16), preferred_element_type=F32) + b_ref[0]


def adaln_all(cc, mod_w, mod_b):
    r, d = cc.shape
    depth, _, n = mod_w.shape
    tn = 1536
    return pl.pallas_call(
        _adaln_kernel,
        grid=(depth, n // tn),
        in_specs=[pl.BlockSpec((r, d), lambda l, j: (0, 0)),
                  pl.BlockSpec((1, d, tn), lambda l, j: (l, 0, j)),
                  pl.BlockSpec((1, 1, tn), lambda l, j: (l, 0, j))],
        out_specs=pl.BlockSpec((1, r, tn), lambda l, j: (l, 0, j)),
        out_shape=jax.ShapeDtypeStruct((depth, r, n), F32),
        compiler_params=_params(2), name="adaln_table",
    )(cc, mod_w, mod_b.reshape(depth, 1, n))


def _assemble_kernel(n_lat_tiles, x_ref, c_ref, scale_ref, shift_ref, s_ref, h_ref):
    s = jnp.where(pl.program_id(0) < n_lat_tiles, x_ref[...], c_ref[...])
    s_ref[...] = s
    h_ref[...] = (s * (1.0 + scale_ref[0]) + shift_ref[0]).astype(h_ref.dtype)


def assemble_stream(x2d, ctx2d, scale, shift, seq, batch):
    t_lat, d = x2d.shape
    tm = ROW_TILE
    n_lat, n_ctx = t_lat // tm, ctx2d.shape[0] // tm
    rows = t_lat + ctx2d.shape[0]
    mod_spec = pl.BlockSpec((1, 1, d), _mod_row_map(tm, seq, batch))
    row_spec = pl.BlockSpec((tm, d), lambda i: (i, 0))
    return pl.pallas_call(
        functools.partial(_assemble_kernel, n_lat),
        grid=(n_lat + n_ctx,),
        in_specs=[pl.BlockSpec((tm, d), lambda i: (jnp.minimum(i, n_lat - 1), 0)),
                  pl.BlockSpec((tm, d), lambda i: (jnp.maximum(i - n_lat, 0), 0)), mod_spec, mod_spec],
        out_specs=[row_spec, row_spec],
        out_shape=[jax.ShapeDtypeStruct((rows, d), F32), jax.ShapeDtypeStruct((rows, d), BF16)],
        compiler_params=_params(1), name="assemble_modulate",
    )(x2d, ctx2d, scale, shift)


def _proj_kernel(x_ref, w_ref, o_ref):
    o_ref[...] = jnp.dot(x_ref[...], w_ref[...], preferred_element_type=F32).astype(o_ref.dtype)


def project(x, w, out_dtype=F32, tm=ROW_TILE, tn=None, rows=None):
    m, k = x.shape
    m = m if rows is None else rows
    n = w.shape[1]
    tn = n if tn is None else tn
    return pl.pallas_call(
        _proj_kernel,
        grid=(m // tm, n // tn),
        in_specs=[pl.BlockSpec((tm, k), lambda i, j: (i, 0)),
                  pl.BlockSpec((k, tn), lambda i, j: (0, j))],
        out_specs=pl.BlockSpec((tm, tn), lambda i, j: (i, j)),
        out_shape=jax.ShapeDtypeStruct((m, n), out_dtype),
        compiler_params=_params(2), name="projection",
    )(x, w)


OUT_LN_SUB = 256


def _out_ln_kernel(x_ref, *rest):
    _out_ln_body(x_ref[...], *rest)


def _out_ln_split_kernel(n_lat_tiles, x_ref, xc_ref, *rest):
    _out_ln_body(jnp.where(pl.program_id(0) < n_lat_tiles, x_ref[...], xc_ref[...]), *rest)


def _out_ln_body(x, w_ref, b_ref, s_ref, gate_ref, lng_ref, lnb_ref, scale_ref, shift_ref, s_out_ref, h_out_ref):
    xb = x.astype(BF16)
    w = w_ref[0]
    n_sub = max(x.shape[0] // OUT_LN_SUB, 1)
    sub = x.shape[0] // n_sub
    rows = lambda i: slice(i * sub, (i + 1) * sub)
    product = lambda i: jnp.dot(xb[rows(i)], w, preferred_element_type=F32)
    pending = product(0)
    for i in range(n_sub):
        y = pending + b_ref[...]
        if i + 1 < n_sub:
            pending = product(i + 1)
        z = DEEPNORM_ALPHA * s_ref[rows(i), :] + gate_ref[0] * y
        mu = jnp.mean(z, axis=-1, keepdims=True)
        zc = z - mu
        var = jnp.mean(zc * zc, axis=-1, keepdims=True)
        sn = zc * lax.rsqrt(var + LN_EPS) * lng_ref[...] + lnb_ref[...]
        s_out_ref[rows(i), :] = sn
        h_out_ref[rows(i), :] = (sn * (1.0 + scale_ref[0]) + shift_ref[0]).astype(h_out_ref.dtype)


def out_ln(x, w, layer, b, s, gate, ln_g, ln_b, scale, shift, rows, seq, batch, tm, x_ctx=None):
    k = x.shape[1]
    d = w.shape[2]
    mod_spec = pl.BlockSpec((1, 1, d), _mod_row_map(tm, seq, batch))
    vec_spec = pl.BlockSpec((1, d), lambda i: (0, 0))
    row_spec = pl.BlockSpec((tm, d), lambda i: (i, 0))
    if x_ctx is None:
        body, x_specs, x_args = _out_ln_kernel, [pl.BlockSpec((tm, k), lambda i: (i, 0))], (x,)
    else:
        n_lat = x.shape[0] // tm
        body = functools.partial(_out_ln_split_kernel, n_lat)
        x_specs = [pl.BlockSpec((tm, k), lambda i: (jnp.minimum(i, n_lat - 1), 0)),
                   pl.BlockSpec((tm, k), lambda i: (jnp.maximum(i - n_lat, 0), 0))]
        x_args = (x, x_ctx)
    return pl.pallas_call(
        body,
        grid=(rows // tm,),
        in_specs=x_specs + [pl.BlockSpec((1, k, d), lambda i: (layer, 0, 0)),
                            vec_spec, row_spec, mod_spec, vec_spec, vec_spec, mod_spec, mod_spec],
        out_specs=[row_spec, row_spec],
        out_shape=[jax.ShapeDtypeStruct((rows, d), F32), jax.ShapeDtypeStruct((rows, d), BF16)],
        compiler_params=_params(1), name="out_proj_ln",
    )(*x_args, w, b.reshape(1, d), s, gate, ln_g.reshape(1, d), ln_b.reshape(1, d), scale, shift)


def _ffn_in_kernel(x_ref, w_ref, a_ref):
    x = x_ref[...]
    dff = a_ref.shape[1]
    for c in range(dff // V7X_MXU_DIM):
        lo = c * V7X_MXU_DIM
        g = jnp.dot(x, w_ref[0, :, lo:lo + V7X_MXU_DIM], preferred_element_type=F32)
        u = jnp.dot(x, w_ref[0, :, dff + lo:dff + lo + V7X_MXU_DIM], preferred_element_type=F32)
        a_ref[:, lo:lo + V7X_MXU_DIM] = (g * jax.nn.sigmoid(g) * u).astype(a_ref.dtype)


def ffn_in(h, w_in, layer, rows):
    k = h.shape[1]
    dff = w_in.shape[2] // 2
    tm = ROW_TILE
    return pl.pallas_call(
        _ffn_in_kernel,
        grid=(rows // tm,),
        in_specs=[pl.BlockSpec((tm, k), lambda i: (i, 0)),
                  pl.BlockSpec((1, k, 2 * dff), lambda i: (layer, 0, 0))],
        out_specs=pl.BlockSpec((tm, dff), lambda i: (i, 0)),
        out_shape=jax.ShapeDtypeStruct((rows, dff), BF16),
        compiler_params=_params(1), name="ffn_swiglu_in",
    )(h, w_in)


V7X_SUBLANES = 8


def _centred_conv3(z, bias, cw, const, seq_len):
    length = z.shape[0]
    c0, c1, c2 = cw[0:1], cw[1:2], cw[2:3]
    if seq_len != length:
        if bias is not None:
            z = z + bias
        pos = lax.broadcasted_iota(jnp.int32, (length, 1), 0) & (seq_len - 1)
        prev = jnp.where(pos == 0, 0.0, pltpu.roll(z, 1, 0))
        nxt = jnp.where(pos == seq_len - 1, 0.0, pltpu.roll(z, length - 1, 0))
        out = prev * c0 + z * c1 + nxt * c2
        return out if const is None else out + const
    out = pltpu.roll(z, 1, 0) * c0 + z * c1 + pltpu.roll(z, length - 1, 0) * c2
    wrapped_last, wrapped_first = z[length - 1:length], z[0:1]
    if bias is not None:
        out = out + (const + bias * (c0 + c1 + c2))
        wrapped_last, wrapped_first = wrapped_last + bias, wrapped_first + bias
    e = V7X_SUBLANES
    r = lax.broadcasted_iota(jnp.int32, (e, 1), 0)
    top = jnp.where(r == 0, out[0:e] - wrapped_last * c0, out[0:e])
    bot = jnp.where(r == e - 1, out[length - e:] - wrapped_first * c2, out[length - e:])
    return jnp.concatenate([top, out[e:length - e], bot], axis=0)


def _hyena_in_kernel(seq_len, x_ref, w0_ref, w1_ref, w2_ref, b0_ref, b1_ref, b2_ref, cw0_ref, cw1_ref, cw2_ref,
                     cb0_ref, cb1_ref, cb2_ref, x0_ref, u_ref):
    x = x_ref[...]

    def branch(w_ref, b_ref, cw_ref, cb_ref):
        z = jnp.dot(x, w_ref[...], preferred_element_type=F32)
        return _centred_conv3(z, b_ref[...], cw_ref[...], cb_ref[...], seq_len)

    x0 = branch(w0_ref, b0_ref, cw0_ref, cb0_ref)
    x1 = branch(w1_ref, b1_ref, cw1_ref, cb1_ref)
    v = branch(w2_ref, b2_ref, cw2_ref, cb2_ref)
    x0_ref[...] = x0.astype(x0_ref.dtype)
    u_ref[...] = (x1 * v).astype(u_ref.dtype)


def hyena_in(h, w_in, b_in, conv_w, conv_b, length, n_seq, row_offset, seq_len):
    k = h.shape[1]
    d = w_in.shape[1] // 3
    tn = 256
    nb = d // tn
    off = row_offset // length
    w_specs = [pl.BlockSpec((k, tn), lambda b, j, q=q: (0, j + q * nb)) for q in range(3)]
    v_specs = [pl.BlockSpec((1, tn), lambda b, j, q=q: (0, j + q * nb)) for q in range(3)]
    cw_specs = [pl.BlockSpec((3, tn), lambda b, j, q=q: (0, j + q * nb)) for q in range(3)]
    out_spec = pl.BlockSpec((length, tn), lambda b, j: (b, j))
    b2, cb2 = b_in.reshape(1, 3 * d), conv_b.reshape(1, 3 * d)
    return pl.pallas_call(
        functools.partial(_hyena_in_kernel, seq_len),
        grid=(n_seq, nb),
        in_specs=[pl.BlockSpec((length, k), lambda b, j: (b + off, 0))] + w_specs + v_specs + cw_specs + v_specs,
        out_specs=[out_spec, out_spec],
        out_shape=[jax.ShapeDtypeStruct((n_seq * length, d), BF16)] * 2,
        compiler_params=_params(2), name="hyena_in_proj_conv3",
    )(h, w_in, w_in, w_in, b2, b2, b2, conv_w, conv_w, conv_w, cb2, cb2, cb2)


def _hyena_filter_kernel(feat_ref, w1_ref, b1_ref, w2_ref, b2_ref, w3_ref, b3_ref, w4_ref, freq_ref, win_ref, k_ref):
    freq = freq_ref[...]
    z = jnp.sin(freq * (jnp.dot(feat_ref[...], w1_ref[...], precision=HIGHEST, preferred_element_type=F32) + b1_ref[...]))
    z = jnp.sin(freq * (jnp.dot(z, w2_ref[...], precision=HIGHEST, preferred_element_type=F32) + b2_ref[...]))
    z = jnp.sin(freq * (jnp.dot(z, w3_ref[...], precision=HIGHEST, preferred_element_type=F32) + b3_ref[...]))
    taps = jnp.dot(z, w4_ref[...], precision=HIGHEST, preferred_element_type=F32)
    d = win_ref.shape[1]
    win = win_ref[...]
    k_fwd = taps[:, :d] * win
    k_bwd = taps[:, d:] * win
    tile = k_fwd.shape[0]
    first = (lax.broadcasted_iota(jnp.int32, (tile, 1), 0) + pl.program_id(0) * tile) == 0
    k_ref[:, :d] = (k_fwd + k_bwd).astype(k_ref.dtype)
    k_ref[:, d:] = jnp.where(first, 0.0, k_bwd - k_fwd).astype(k_ref.dtype)


def _hyena_features(length):
    t01 = np.linspace(0.0, 1.0, length, dtype=np.float32)[:, None]
    bands = (HYENA_EMB - 1) // 2
    w = (np.float32(2.0 * math.pi / length) * np.arange(length, dtype=np.float32)).astype(np.float32)
    f = np.linspace(1e-4, bands - 1, bands, dtype=np.float32)
    ang = (w[:, None] * f[None, :]).astype(np.float32)
    feat = np.concatenate([t01, np.cos(ang), -np.sin(ang)], axis=-1).astype(np.float32)
    return np.pad(feat, ((0, 0), (0, V7X_LANES - HYENA_EMB)))


def _hyena_window(length):
    max_decay = math.log(HYENA_DECAY_TARGET) / HYENA_FAST_DECAY_PCT
    min_decay = math.log(HYENA_DECAY_TARGET) / HYENA_SLOW_DECAY_PCT
    deltas = np.abs(np.linspace(min_decay, max_decay, D_MODEL, dtype=np.float32))
    t = np.linspace(0.0, 1.0, length, dtype=np.float32)
    return np.exp(-t[:, None] * deltas[None, :]).astype(np.float32)


def hyena_filter_taps(length, p):
    hid = V7X_LANES
    pad_h = hid - HYENA_FILTER_HIDDEN
    padc = lambda a: jnp.pad(a, ((0, 0), (0, pad_h)))
    w1 = jnp.pad(p["fw1"], ((0, V7X_LANES - HYENA_EMB), (0, pad_h)))
    w2 = jnp.pad(p["fw2"], ((0, pad_h), (0, pad_h)))
    w3 = jnp.pad(p["fw3"], ((0, pad_h), (0, pad_h)))
    w4 = jnp.pad(p["fw4"], ((0, pad_h), (0, 0)))
    vec = lambda a: padc(a.reshape(1, -1))
    args = (jnp.asarray(_hyena_features(length)), w1, vec(p["fb1"]), w2, vec(p["fb2"]), w3, vec(p["fb3"]), w4,
            vec(p["freq"]), jnp.asarray(_hyena_window(length)))
    tl = min(256, length)
    full = lambda a: pl.BlockSpec(a.shape, lambda i: (0, 0))
    return pl.pallas_call(
        _hyena_filter_kernel,
        grid=(length // tl,),
        in_specs=[pl.BlockSpec((tl, hid), lambda i: (i, 0))] + [full(a) for a in args[1:9]]
                 + [pl.BlockSpec((tl, D_MODEL), lambda i: (i, 0))],
        out_specs=pl.BlockSpec((tl, 2 * D_MODEL), lambda i: (i, 0)),
        out_shape=jax.ShapeDtypeStruct((length, 2 * D_MODEL), BF16),
        compiler_params=_params(1), name="hyena_filter",
    )(*args)


@functools.lru_cache(maxsize=None)
def _negacyclic_dft(length):
    n = 2 * length
    k = np.arange(length, dtype=np.int64)
    turns = ((2 * k[:, None] + 1) * k[None, :]) % (2 * n)
    ang = turns.astype(np.float64) * (2.0 * math.pi / (2 * n))
    c, s = np.cos(ang), np.sin(ang)
    inv = 2.0 / n
    return c, s, (c.T * inv), (s.T * inv)


def _dft_spec_kernel(c_ref, s_ref, taps_ref, h_ref):
    d = h_ref.shape[2]
    h_ref[0] = jnp.dot(c_ref[...], taps_ref[:, :d], preferred_element_type=F32)
    h_ref[1] = jnp.dot(s_ref[...], taps_ref[:, d:], preferred_element_type=F32)


def _dft_fwd_kernel(c_ref, s_ref, u_ref, hr_ref, hi_ref, y_ref):
    u = u_ref[...]
    ur = jnp.dot(c_ref[...], u, preferred_element_type=F32)
    us = jnp.dot(s_ref[...], u, preferred_element_type=F32)
    hr, hi = hr_ref[0], hi_ref[0]
    y_ref[0, 0] = (ur * hr + us * hi).astype(y_ref.dtype)
    y_ref[0, 1] = (us * hr - ur * hi).astype(y_ref.dtype)


def _dft_inv_kernel(ct_ref, st_ref, y_ref, u_ref, x0_ref, skip_ref, o_ref):
    y = (jnp.dot(ct_ref[...], y_ref[0, 0], preferred_element_type=F32)
         + jnp.dot(st_ref[...], y_ref[0, 1], preferred_element_type=F32))
    u = u_ref[...].astype(F32)
    o_ref[...] = ((y + u * skip_ref[...]) * x0_ref[...].astype(F32)).astype(o_ref.dtype)


def hyena_long_conv(x0, u, taps, skip, length, n_seq):
    d = u.shape[1]
    c, s, ct, st = (jnp.asarray(a, dtype=BF16) for a in _negacyclic_dft(length))
    tf = min(512, length)
    nf = length // tf
    spec = pl.pallas_call(
        _dft_spec_kernel,
        grid=(nf,),
        in_specs=[pl.BlockSpec((tf, length), lambda i: (i, 0)),
                  pl.BlockSpec((tf, length), lambda i: (i, 0)),
                  pl.BlockSpec((length, 2 * d), lambda i: (0, 0))],
        out_specs=pl.BlockSpec((2, tf, d), lambda i: (0, i, 0)),
        out_shape=jax.ShapeDtypeStruct((2, length, d), F32),
        compiler_params=_params(1), name="hyena_filter_spectrum",
    )(c, s, taps)
    y = pl.pallas_call(
        _dft_fwd_kernel,
        grid=(n_seq, nf),
        in_specs=[pl.BlockSpec((tf, length), lambda b, i: (i, 0)),
                  pl.BlockSpec((tf, length), lambda b, i: (i, 0)),
                  pl.BlockSpec((length, d), lambda b, i: (b, 0)),
                  pl.BlockSpec((1, tf, d), lambda b, i: (0, i, 0)),
                  pl.BlockSpec((1, tf, d), lambda b, i: (1, i, 0))],
        out_specs=pl.BlockSpec((1, 2, tf, d), lambda b, i: (b, 0, i, 0)),
        out_shape=jax.ShapeDtypeStruct((n_seq, 2, length, d), BF16),
        compiler_params=_params(2), name="hyena_dft_forward",
    )(c, s, u, spec, spec)
    return pl.pallas_call(
        _dft_inv_kernel,
        grid=(n_seq, nf),
        in_specs=[pl.BlockSpec((tf, length), lambda b, i: (i, 0)),
                  pl.BlockSpec((tf, length), lambda b, i: (i, 0)),
                  pl.BlockSpec((1, 2, length, d), lambda b, i: (b, 0, 0, 0)),
                  pl.BlockSpec((tf, d), lambda b, i: (b * nf + i, 0)),
                  pl.BlockSpec((tf, d), lambda b, i: (b * nf + i, 0)),
                  pl.BlockSpec((1, d), lambda b, i: (0, 0))],
        out_specs=pl.BlockSpec((tf, d), lambda b, i: (b * nf + i, 0)),
        out_shape=jax.ShapeDtypeStruct((n_seq * length, d), BF16),
        compiler_params=_params(2), name="hyena_dft_inverse",
    )(ct, st, y, u, x0, skip.reshape(1, d))


def hyena_mixer_core(h, p, length, n_seq, row_offset, tile_rows):
    x0, u = hyena_in(h, p["w_in"], p["b_in"], p["conv_w"], p["conv_b"], tile_rows, n_seq * length // tile_rows,
                     row_offset, length)
    taps = hyena_filter_taps(length, p)
    return hyena_long_conv(x0, u, taps, p["skip"], length, n_seq)


def _rope_swap_columns():
    half, quarter = MLA_ROPE // 2, MLA_ROPE // 4
    r = np.arange(MLA_ROPE)
    within = r % half
    src = np.where(within < quarter, r + quarter, r - quarter)
    sign = np.where(within < quarter, -1.0, 1.0).astype(np.float32)
    return src, sign


@functools.lru_cache(maxsize=None)
def _rope_tables(seq, ctx_rows):
    half, quarter = MLA_ROPE // 2, MLA_ROPE // 4
    inv_freq = (np.float32(ROPE_THETA) ** (-np.arange(quarter, dtype=np.float32) / quarter)).astype(np.float32)
    t = np.arange(seq)
    row, col = (t // GRID_W).astype(np.float32), (t % GRID_W).astype(np.float32)
    r = np.arange(MLA_ROPE)
    pos = np.where((r // half)[None, :] == 0, row[:, None], col[:, None]).astype(np.float32)
    ang = (pos * inv_freq[r % quarter][None, :]).astype(np.float32).astype(np.float64)
    cos = np.concatenate([np.cos(ang), np.ones((ctx_rows, MLA_ROPE))], axis=0)
    sin = np.concatenate([np.sin(ang), np.zeros((ctx_rows, MLA_ROPE))], axis=0)
    ones = np.ones((seq + ctx_rows, MLA_NOPE))
    tab_q = (np.concatenate([ones, cos, sin], axis=1) * MLA_SCALE).astype(np.float32)
    tab_k = np.concatenate([cos, sin], axis=1).astype(np.float32)
    return tab_q, tab_k


def _rms_bf16(x, g):
    return (x * lax.rsqrt(jnp.mean(x * x, axis=-1, keepdims=True) + RMS_EPS) * g).astype(BF16)


def _mla_qkv_kernel(h_ref, win_ref, qg_ref, kvg_ref, wq_ref, wkv_ref, tabq_ref, tabk_ref, q_ref, k_ref, v_ref):
    z = jnp.dot(h_ref[...], win_ref[...], preferred_element_type=F32)
    ckv, rope, cq = z[:, :MLA_KV_LORA], z[:, MLA_KV_LORA:MLA_KV_LORA + 2 * MLA_ROPE], z[:, MLA_KV_LORA + 2 * MLA_ROPE:]
    qa = jnp.dot(_rms_bf16(cq, qg_ref[...]), wq_ref[...], preferred_element_type=F32)
    tab = tabq_ref[...]
    for h in range(MLA_HEADS):
        q_ref[:, h * MLA_QK:(h + 1) * MLA_QK] = (qa[:, h * MLA_QK:(h + 1) * MLA_QK] * tab).astype(q_ref.dtype)
    kva = jnp.dot(_rms_bf16(ckv, kvg_ref[...]), wkv_ref[...], preferred_element_type=F32)
    prod = rope * tabk_ref[...]
    k_rot = (prod + pltpu.roll(prod, MLA_ROPE, 1)).astype(k_ref.dtype)
    hw = MLA_NOPE + MLA_V
    for h in range(MLA_HEADS):
        k_ref[:, h * MLA_QK:h * MLA_QK + MLA_NOPE] = kva[:, h * hw:h * hw + MLA_NOPE].astype(k_ref.dtype)
        k_ref[:, h * MLA_QK + MLA_NOPE:(h + 1) * MLA_QK] = k_rot
        v_ref[:, h * MLA_V:(h + 1) * MLA_V] = kva[:, h * hw + MLA_NOPE:(h + 1) * hw].astype(v_ref.dtype)


def mla_qkv(h, p, rows, seq, batch):
    tm = ROW_TILE
    d = h.shape[1]
    src, sign = _rope_swap_columns()
    w_in = p["w_in"]
    o_kv, o_rope = MLA_Q_LORA, MLA_Q_LORA + MLA_KV_LORA
    w_rope = w_in[:, o_rope:]
    w_cat = jnp.concatenate([w_in[:, o_kv:o_rope], w_rope, w_rope[:, src] * sign, w_in[:, :o_kv]], axis=1).astype(BF16)
    w_uq = p["w_uq"].reshape(MLA_Q_LORA, MLA_HEADS, MLA_NOPE + MLA_ROPE)
    w_uq_rope = w_uq[:, :, MLA_NOPE:]
    w_uq = jnp.concatenate([w_uq, w_uq_rope[:, :, src] * sign], axis=2).reshape(MLA_Q_LORA, MLA_HEADS * MLA_QK).astype(BF16)
    tab_q, tab_k = _rope_tables(seq, tm)
    lat_tiles = batch * seq // tm
    tab_map = lambda i: (jnp.where(i < lat_tiles, i % (seq // tm), seq // tm), 0)
    full = lambda a: pl.BlockSpec(a.shape, lambda i: (0, 0))
    w_ukv = p["w_ukv"].astype(BF16)
    q_norm, kv_norm = p["q_norm"].reshape(1, -1), p["kv_norm"].reshape(1, -1)
    row = lambda w: pl.BlockSpec((tm, w), lambda i: (i, 0))
    return pl.pallas_call(
        _mla_qkv_kernel,
        grid=(rows // tm,),
        in_specs=[row(d), full(w_cat), full(q_norm), full(kv_norm), full(w_uq), full(w_ukv),
                  pl.BlockSpec((tm, MLA_QK), tab_map), pl.BlockSpec((tm, 2 * MLA_ROPE), tab_map)],
        out_specs=[row(MLA_HEADS * MLA_QK), row(MLA_HEADS * MLA_QK), row(MLA_HEADS * MLA_V)],
        out_shape=[jax.ShapeDtypeStruct((rows, MLA_HEADS * MLA_QK), BF16),
                   jax.ShapeDtypeStruct((rows, MLA_HEADS * MLA_QK), BF16),
                   jax.ShapeDtypeStruct((rows, MLA_HEADS * MLA_V), BF16)],
        compiler_params=_params(1), name="mla_qkv_proj",
    )(h, w_cat, q_norm, kv_norm, w_uq, w_ukv, jnp.asarray(tab_q), jnp.asarray(tab_k))


ATTN_SUB = 256


def _dot_nt(a, b):
    return lax.dot_general(a, b, (((1,), (1,)), ((), ())), preferred_element_type=F32)


def _attention_body(q_ref, key_refs, val_refs, o_ref):
    n_sub = q_ref.shape[0] // ATTN_SUB
    rows = lambda i: slice(i * ATTN_SUB, (i + 1) * ATTN_SUB)
    scores = lambda i: [_dot_nt(q_ref[rows(i), :], k_ref[...]) for k_ref in key_refs]
    pending = scores(0)
    for i in range(n_sub):
        s = pending
        if i + 1 < n_sub:
            pending = scores(i + 1)
        m = functools.reduce(jnp.maximum, [jnp.max(x, axis=-1, keepdims=True) for x in s])
        p = [jnp.exp(x - m) for x in s]
        l = sum(jnp.sum(x, axis=-1, keepdims=True) for x in p)
        o = sum(jnp.dot(x.astype(BF16), v_ref[...], preferred_element_type=F32) for x, v_ref in zip(p, val_refs))
        o_ref[rows(i), :] = (o / l).astype(o_ref.dtype)


def _attn_kernel(with_ctx_out, ql_ref, qc_ref, kl_ref, vl_ref, kc_ref, vc_ref, *o_refs):
    _attention_body(ql_ref, (kl_ref, kc_ref), (vl_ref, vc_ref), o_refs[0])
    if with_ctx_out:
        _attention_body(qc_ref, (kc_ref,), (vc_ref,), o_refs[1])


def mla_attention(q, k, v, batch, seq, ctx_len, with_ctx_out):
    ctx_blk = batch * seq // ctx_len
    lat = lambda w: pl.BlockSpec((seq, w), lambda b, h: (b, h))
    cx = lambda w: pl.BlockSpec((ctx_len, w), lambda b, h: (ctx_blk + b, h))
    out_specs = [pl.BlockSpec((seq, MLA_V), lambda b, h: (b, h))]
    out_shape = [jax.ShapeDtypeStruct((batch * seq, MLA_HEADS * MLA_V), BF16)]
    if with_ctx_out:
        out_specs.append(pl.BlockSpec((ctx_len, MLA_V), lambda b, h: (b, h)))
        out_shape.append(jax.ShapeDtypeStruct((batch * ctx_len, MLA_HEADS * MLA_V), BF16))
    outs = pl.pallas_call(
        functools.partial(_attn_kernel, with_ctx_out),
        grid=(batch, MLA_HEADS),
        in_specs=[lat(MLA_QK), cx(MLA_QK), lat(MLA_QK), lat(MLA_V), cx(MLA_QK), cx(MLA_V)],
        out_specs=out_specs, out_shape=out_shape,
        compiler_params=_params(2), name="mla_attention",
    )(q, q, k, v, k, v)
    return (outs[0], outs[1]) if with_ctx_out else (outs[0], None)


GDN_QKV = 2 * GDN_HEADS * GDN_DK + GDN_HEADS * GDN_DV
GDN_SCALARS = 2 * GDN_HEADS
GDN_ROWS = 16
LOG2_CHUNK = int(math.log2(GDN_CHUNK))
NEG_BIG = -1e30
GDN_GROUP = 128
GDN_HEADS_PER_STEP = 2
GDN_GROUPS_PER_STEP = 4


def _gdn_in_kernel(seq_len, x_ref, w_ref, cw_ref, o_ref):
    z = jnp.dot(x_ref[...], w_ref[...], preferred_element_type=F32)
    z = _centred_conv3(z, None, cw_ref[...], None, seq_len)
    z = z * jax.nn.sigmoid(z)
    j = pl.program_id(1)
    blocks_per_kind = GDN_HEADS * GDN_DK // z.shape[1]
    q_scale = jnp.where(j < blocks_per_kind, GDN_DK ** -0.5, 1.0)
    for hh in range(z.shape[1] // GDN_DK):
        zh = z[:, hh * GDN_DK:(hh + 1) * GDN_DK]
        zn = zh * lax.rsqrt(jnp.sum(zh * zh, axis=-1, keepdims=True) + RMS_EPS) * q_scale
        o_ref[:, hh * GDN_DK:(hh + 1) * GDN_DK] = jnp.where(j < 2 * blocks_per_kind, zn, zh).astype(o_ref.dtype)


def gdn_in(h, w_qkv, conv_w, length, n_seq, row_offset, seq_len):
    k = h.shape[1]
    n = w_qkv.shape[1]
    tn = 256
    off = row_offset // length
    return pl.pallas_call(
        functools.partial(_gdn_in_kernel, seq_len),
        grid=(n_seq, n // tn),
        in_specs=[pl.BlockSpec((length, k), lambda b, j: (b + off, 0)),
                  pl.BlockSpec((k, tn), lambda b, j: (0, j)),
                  pl.BlockSpec((3, tn), lambda b, j: (0, j))],
        out_specs=pl.BlockSpec((length, tn), lambda b, j: (b, j)),
        out_shape=jax.ShapeDtypeStruct((n_seq * length, n), BF16),
        compiler_params=_params(2), name="gdn_in_proj_conv3",
    )(h, w_qkv, conv_w)


def _gdn_gates_kernel(ab_ref, alog_ref, dtb_ref, beta_ref, ecum_ref, edec_ref, cum_ref, egl_ref):
    r = lax.broadcasted_iota(jnp.int32, (GDN_CHUNK, GDN_CHUNK), 0)
    c = lax.broadcasted_iota(jnp.int32, (GDN_CHUNK, GDN_CHUNK), 1)
    prefix = jnp.where(c <= r, 1.0, 0.0)
    suffix = jnp.where(c >= r, 1.0, 0.0)
    total = jnp.ones((GDN_CHUNK, GDN_CHUNK), F32)
    lane = lax.broadcasted_iota(jnp.int32, (GDN_CHUNK, V7X_LANES), 1)
    alog, dtb = alog_ref[...], dtb_ref[...]
    for ch in range(ab_ref.shape[0] // GDN_CHUNK):
        rows = slice(ch * GDN_CHUNK, (ch + 1) * GDN_CHUNK)
        x = ab_ref[rows, :V7X_LANES] + dtb
        softplus = jnp.maximum(x, 0.0) + jnp.log(1.0 + jnp.exp(-jnp.abs(x)))
        g = -jnp.exp(alog) * softplus
        dot = lambda m: jnp.dot(m, g, precision=HIGHEST, preferred_element_type=F32)
        cum = jnp.where(lane < GDN_HEADS, dot(prefix), dot(suffix))
        g_last = dot(total)
        beta_ref[rows, :] = jax.nn.sigmoid(ab_ref[rows, V7X_LANES:])
        ecum_ref[rows, :] = jnp.exp(cum)
        edec_ref[rows, :] = jnp.exp(g_last - cum)
        cum_ref[rows, :] = cum
        egl_ref[rows, :] = jnp.exp(g_last)


def gdn_gates(ab, a_log, dt_bias):
    rows = ab.shape[0]
    tm = ROW_TILE
    pad = lambda v: jnp.pad(v.reshape(1, -1), ((0, 0), (0, V7X_LANES - GDN_SCALARS)))
    spec = pl.BlockSpec((tm, V7X_LANES), lambda i: (i, 0))
    vec = pl.BlockSpec((1, V7X_LANES), lambda i: (0, 0))
    return pl.pallas_call(
        _gdn_gates_kernel,
        grid=(rows // tm,),
        in_specs=[pl.BlockSpec((tm, 2 * V7X_LANES), lambda i: (i, 0)), vec, vec],
        out_specs=[spec] * 5,
        out_shape=[jax.ShapeDtypeStruct((rows, V7X_LANES), F32)] * 5,
        compiler_params=_params(1), name="gdn_gates",
    )(ab, pad(a_log), pad(dt_bias))


def _bdot(a, b):
    return jnp.dot(a.astype(BF16), b.astype(BF16), preferred_element_type=F32)


def _bdot_nt(a, b):
    return lax.dot_general(a.astype(BF16), b.astype(BF16), (((1,), (1,)), ((), ())), preferred_element_type=F32)


def _bdot_tn(a, b):
    return lax.dot_general(a.astype(BF16), b.astype(BF16), (((0,), (0,)), ((), ())), preferred_element_type=F32)


def _gdn_group_terms(qkv_ref, sc_ref, problems, want_out):
    ri = lax.broadcasted_iota(jnp.int32, (GDN_GROUP, GDN_GROUP), 0)
    ci = lax.broadcasted_iota(jnp.int32, (GDN_GROUP, GDN_GROUP), 1)
    same = (ri >> LOG2_CHUNK) == (ci >> LOG2_CHUNK)
    incl = [same & (ri >= ci), same & (ri <= ci)]
    strict = [same & (ri > ci), same & (ri < ci)]
    eye = jnp.where(ri == ci, 1.0, 0.0)
    pair = (ri >> 1) == (ci >> 1)
    joins = [((ri >> (lg + 1)) == (ci >> (lg + 1))) & ((ri >> lg) != (ci >> lg)) for lg in range(1, LOG2_CHUNK)]

    scalars = {}

    def load(hh, gi, d):
        rows = pl.ds(pl.multiple_of(gi * GDN_GROUP, GDN_GROUP), GDN_GROUP)
        lanes = slice(hh * GDN_DK, (hh + 1) * GDN_DK)
        if (hh, id(gi)) not in scalars:
            sc = sc_ref[hh, :, rows]
            square = jnp.concatenate([sc, jnp.zeros((GDN_GROUP - GDN_ROWS, GDN_GROUP), F32)], axis=0)
            scalars[(hh, id(gi))] = (sc, square.T)
        sc, sc_t = scalars[(hh, id(gi))]
        base = 5 * d
        return dict(
            q=qkv_ref[0][rows, lanes].astype(F32), k=qkv_ref[1][rows, lanes].astype(F32),
            v=qkv_ref[2][rows, lanes].astype(F32),
            beta=sc_t[:, base + 0:base + 1], ecum=sc_t[:, base + 1:base + 2],
            edec=sc_t[:, base + 2:base + 3], cum_c=sc_t[:, base + 3:base + 4],
            cum_r=sc[base + 3:base + 4, :], d=d, group=(hh, id(gi)))

    ps = [load(hh, gi, d) for hh, gi, d in problems]
    for p in ps:
        p["gamma"] = jnp.exp(jnp.where(incl[p["d"]], p["cum_c"] - p["cum_r"], NEG_BIG))
        p["kb"] = p["k"] * p["beta"]
    kk, qk_raw = {}, {}
    for p in ps:
        if p["group"] not in kk:
            kk[p["group"]] = _bdot_nt(p["k"], p["k"])
            if want_out:
                qk_raw[p["group"]] = _bdot_nt(p["q"], p["k"])
    for p in ps:
        p["a"] = jnp.where(strict[p["d"]], kk[p["group"]] * p["beta"] * p["gamma"], 0.0)
    for p in ps:
        p["inv"] = eye - jnp.where(pair, p["a"], 0.0)
    for join in joins:
        for p in ps:
            p["t"] = _bdot(jnp.where(join, p["a"], 0.0), p["inv"])
        for p in ps:
            p["inv"] = p["inv"] - _bdot(p["inv"], p["t"])
    for p in ps:
        p["uw"] = _bdot(p["inv"], jnp.concatenate([p["v"] * p["beta"], p["kb"] * p["ecum"]], axis=-1))
        p["kdec"] = p["k"] * p["edec"]
    if want_out:
        for p in ps:
            p["qk"] = jnp.where(incl[p["d"]], qk_raw[p["group"]] * p["gamma"], 0.0)
    out = []
    for p in ps:
        kw_t, n_t = [], []
        for c in range(GDN_GROUP // GDN_CHUNK):
            sl = slice(c * GDN_CHUNK, (c + 1) * GDN_CHUNK)
            both = _bdot_tn(p["uw"][sl], p["kdec"][sl])
            n_t.append(both[:GDN_DV])
            kw_t.append(both[GDN_DV:])
        q_eff = o_local = None
        if want_out:
            corr = _bdot(p["qk"], p["uw"])
            q_eff, o_local = p["q"] * p["ecum"] - corr[:, GDN_DV:], corr[:, :GDN_DV]
        out.append((kw_t, n_t, q_eff, o_local))
    return out


def _gdn_scan_kernel(ql_ref, kl_ref, vl_ref, qc_ref, kc_ref, vc_ref, scl_ref, scc_ref,
                     gate_ref, onorm_ref, y_ref, kw_ref, nt_ref, qeff_ref, oloc_ref, out_ref):
    n_ctx = qc_ref.shape[0] // GDN_CHUNK
    n_lat = ql_ref.shape[0] // GDN_CHUNK
    per_group = GDN_GROUP // GDN_CHUNK
    heads = range(GDN_HEADS_PER_STEP)
    chains = [(hh, d) for hh in heads for d in range(2)]

    def precompute(qkv, sc_ref, slot0, want_out):
        groups = min(GDN_GROUPS_PER_STEP, qkv[0].shape[0] // GDN_GROUP)

        def body(it, carry):
            gis = [it * groups + gg for gg in range(groups)]
            problems = [(hh, gi, d) for hh in heads for gi in gis for d in range(2)]
            terms = _gdn_group_terms(qkv, sc_ref, problems, want_out)
            for (hh, gi, d), (kw_t, n_t, q_eff, o_local) in zip(problems, terms):
                for c in range(per_group):
                    kw_ref[hh, d, slot0 + gi * per_group + c] = kw_t[c].astype(kw_ref.dtype)
                    nt_ref[hh, d, slot0 + gi * per_group + c] = n_t[c]
                if want_out:
                    rows = pl.ds(pl.multiple_of(gi * GDN_GROUP, GDN_GROUP), GDN_GROUP)
                    qeff_ref[hh, d, rows, :] = q_eff.astype(qeff_ref.dtype)
                    oloc_ref[hh, d, rows, :] = o_local
            return carry
        lax.fori_loop(0, qkv[0].shape[0] // (GDN_GROUP * groups), body, 0)

    precompute((qc_ref, kc_ref, vc_ref), scc_ref, 0, False)
    precompute((ql_ref, kl_ref, vl_ref), scl_ref, n_ctx, True)

    def chunk_rows(c):
        return pl.ds(pl.multiple_of(c * GDN_CHUNK, GDN_CHUNK), GDN_CHUNK)

    def advance(states, sc_ref, slot0, chunk_of):
        prods = [jnp.dot(st.astype(BF16), kw_ref[hh, d, slot0 + chunk_of[d]], preferred_element_type=F32)
                 for st, (hh, d) in zip(states, chains)]
        new = []
        for st, pr, (hh, d) in zip(states, prods, chains):
            c = chunk_of[d]
            egl = sc_ref[hh, 5 * d + 4:5 * d + 5, chunk_rows(c)][:, 0:1]
            new.append(egl * st - pr + nt_ref[hh, d, slot0 + c])
        return tuple(new)

    def scan_ctx(step, states):
        return advance(states, scc_ref, 0, (step, n_ctx - 1 - step))

    def scan_lat(step, states):
        chunk_of = (step, n_lat - 1 - step)
        outs = [_bdot_nt(qeff_ref[hh, d, chunk_rows(chunk_of[d]), :], st) for st, (hh, d) in zip(states, chains)]
        for o, (hh, d) in zip(outs, chains):
            rows = chunk_rows(chunk_of[d])
            out_ref[hh, d, rows, :] = o + oloc_ref[hh, d, rows, :]
        return advance(states, scl_ref, n_ctx, chunk_of)

    zero = jnp.zeros((GDN_DV, GDN_DK), F32)
    states = lax.fori_loop(0, n_ctx, scan_ctx, tuple(zero for _ in chains))
    lax.fori_loop(0, n_lat, scan_lat, states)

    onorm = onorm_ref[...]

    def finish(gi, carry):
        rows = pl.ds(pl.multiple_of(gi * GDN_GROUP, GDN_GROUP), GDN_GROUP)
        for hh in heads:
            lanes = slice(hh * GDN_DV, (hh + 1) * GDN_DV)
            o = out_ref[hh, 0, rows, :] + out_ref[hh, 1, rows, :]
            o = o * lax.rsqrt(jnp.mean(o * o, axis=-1, keepdims=True) + RMS_EPS) * onorm
            gte = gate_ref[rows, lanes].astype(F32)
            y_ref[rows, lanes] = (o * gte * jax.nn.sigmoid(gte)).astype(y_ref.dtype)
        return carry

    lax.fori_loop(0, ql_ref.shape[0] // GDN_GROUP, finish, 0)


def gdn_scan(qkv_l, qkv_c, scalars, gate, o_norm, batch, seq, ctx_len):
    n_lat, n_ctx = seq // GDN_CHUNK, ctx_len // GDN_CHUNK
    ctx_blk = batch * seq // ctx_len
    hp = GDN_HEADS_PER_STEP
    n_hp = GDN_HEADS // hp
    lat = lambda q: pl.BlockSpec((seq, hp * GDN_DK), lambda b, h, q=q: (b, q * n_hp + h))
    cx = lambda q: pl.BlockSpec((ctx_len, hp * GDN_DK), lambda b, h, q=q: (b, q * n_hp + h))
    return pl.pallas_call(
        _gdn_scan_kernel,
        grid=(batch, n_hp),
        in_specs=[lat(0), lat(1), lat(2), cx(0), cx(1), cx(2),
                  pl.BlockSpec((hp, GDN_ROWS, seq), lambda b, h: (h, 0, b)),
                  pl.BlockSpec((hp, GDN_ROWS, ctx_len), lambda b, h: (h, 0, ctx_blk + b)),
                  pl.BlockSpec((seq, hp * GDN_DV), lambda b, h: (b, h)),
                  pl.BlockSpec((1, GDN_DV), lambda b, h: (0, 0))],
        out_specs=pl.BlockSpec((seq, hp * GDN_DV), lambda b, h: (b, h)),
        out_shape=jax.ShapeDtypeStruct((batch * seq, GDN_HEADS * GDN_DV), BF16),
        scratch_shapes=[pltpu.VMEM((hp, 2, n_ctx + n_lat, GDN_DK, GDN_DK), BF16),
                        pltpu.VMEM((hp, 2, n_ctx + n_lat, GDN_DV, GDN_DK), F32),
                        pltpu.VMEM((hp, 2, seq, GDN_DK), BF16),
                        pltpu.VMEM((hp, 2, seq, GDN_DV), F32),
                        pltpu.VMEM((hp, 2, seq, GDN_DV), F32)],
        compiler_params=_params(2), name="gdn_chunk_scan",
    )(qkv_l, qkv_l, qkv_l, qkv_c, qkv_c, qkv_c, scalars, scalars, gate, o_norm.reshape(1, GDN_DV))


def gdn_mixer_core(h, p, batch, seq, ctx_len):
    t_lat = batch * seq
    w_in = p["w_in"]
    hv = GDN_HEADS * GDN_DV
    w_qkv = w_in[:, :GDN_QKV].astype(BF16)
    qkv_l = gdn_in(h, w_qkv, p["conv_w"], seq, batch, 0, seq)
    qkv_c = gdn_in(h, w_qkv, p["conv_w"], batch * ctx_len, 1, t_lat, ctx_len)
    gate = project(h, w_in[:, GDN_QKV:GDN_QKV + hv].astype(BF16), out_dtype=BF16, rows=t_lat)
    lane_pad = ((0, 0), (0, V7X_LANES - GDN_SCALARS))
    w_a = jnp.pad(w_in[:, GDN_QKV + hv:GDN_QKV + hv + GDN_SCALARS], lane_pad)
    w_b = jnp.pad(w_in[:, GDN_QKV + hv + GDN_SCALARS:], lane_pad)
    ab = project(h, jnp.concatenate([w_a, w_b], axis=1).astype(BF16))
    parts = gdn_gates(ab, p["a_log"], p["dt_bias"])
    rows = ab.shape[0]
    stacked = jnp.stack([a[:, :GDN_SCALARS] for a in parts], axis=0).reshape(5, rows, 2, GDN_HEADS)
    scalars = stacked.transpose(3, 2, 0, 1).reshape(GDN_HEADS, 10, rows)
    scalars = jnp.pad(scalars, ((0, 0), (0, GDN_ROWS - 10), (0, 0)))
    return gdn_scan(qkv_l, qkv_c, scalars, gate, p["o_norm"], batch, seq, ctx_len)


def kernel(x, c, ctx, c_ctx, mod_w, mod_b, ln_g, ln_b, ffn_w_in, ffn_w_out, hy_w_in, hy_b_in, hy_conv_w, hy_conv_b, hy_fw1, hy_fb1, hy_fw2, hy_fb2, hy_fw3, hy_fb3, hy_fw4, hy_freq, hy_skip, hy_w_out, hy_b_out, mla_w_in, mla_q_norm, mla_kv_norm, mla_w_uq, mla_w_ukv, mla_w_out, gdn_w_in, gdn_conv_w, gdn_a_log, gdn_dt_bias, gdn_o_norm, gdn_w_out):
    batch, seq, d = x.shape
    ctx_len = ctx.shape[1]
    t_lat, t_ctx = batch * seq, batch * ctx_len
    t_all = t_lat + t_ctx
    assert t_lat % ROW_TILE == 0 and t_ctx % ROW_TILE == 0 and seq % ROW_TILE == 0

    n_mod = -(-(batch + 1) // 8) * 8
    cc = jnp.concatenate([c, c_ctx[None], jnp.zeros((n_mod - batch - 1, d), F32)], axis=0)
    mods = adaln_all(cc, mod_w, mod_b).reshape(DEPTH, n_mod, 6, 1, d).transpose(0, 2, 1, 3, 4)
    zero_bias = jnp.zeros((d,), F32)

    ffn_w_in_bf16, ffn_w_out_bf16 = ffn_w_in.astype(BF16), ffn_w_out.astype(BF16)
    s, h = assemble_stream(x.reshape(t_lat, d), ctx.reshape(t_ctx, d), mods[0, 1], mods[0, 0], seq, batch)
    for i in range(DEPTH):
        kind, j = i % N_MIXERS, i // N_MIXERS
        ctx_out = any(l % N_MIXERS != MIXER_HYENA for l in range(i + 1, DEPTH))
        rows_out = t_all if ctx_out else t_lat
        if kind == MIXER_HYENA:
            p = {"w_in": hy_w_in[j].astype(BF16), "b_in": hy_b_in[j], "conv_w": hy_conv_w[j], "conv_b": hy_conv_b[j],
                 "fw1": hy_fw1[j], "fb1": hy_fb1[j], "fw2": hy_fw2[j], "fb2": hy_fb2[j],
                 "fw3": hy_fw3[j], "fb3": hy_fb3[j], "fw4": hy_fw4[j], "freq": hy_freq[j], "skip": hy_skip[j]}
            y = hyena_mixer_core(h, p, seq, batch, 0, seq)
            y_ctx = hyena_mixer_core(h, p, ctx_len, batch, t_lat, t_ctx) if ctx_out else None
            w_out, b_out = hy_w_out, hy_b_out[j]
        elif kind == MIXER_MLA:
            p = {"w_in": mla_w_in[j], "q_norm": mla_q_norm[j], "kv_norm": mla_kv_norm[j],
                 "w_uq": mla_w_uq[j], "w_ukv": mla_w_ukv[j]}
            q, k, v = mla_qkv(h, p, t_all, seq, batch)
            y, y_ctx = mla_attention(q, k, v, batch, seq, ctx_len, ctx_out)
            w_out, b_out = mla_w_out, zero_bias
        else:
            assert not ctx_out
            p = {"w_in": gdn_w_in[j], "conv_w": gdn_conv_w[j], "a_log": gdn_a_log[j],
                 "dt_bias": gdn_dt_bias[j], "o_norm": gdn_o_norm[j]}
            y, y_ctx = gdn_mixer_core(h, p, batch, seq, ctx_len), None
            w_out, b_out = gdn_w_out, zero_bias
        s, h = out_ln(y, w_out.astype(BF16), j, b_out, s, mods[i, 2], ln_g[i, 0], ln_b[i, 0], mods[i, 4], mods[i, 3],
                      rows_out, seq, batch, ROW_TILE, x_ctx=y_ctx)
        a = ffn_in(h, ffn_w_in_bf16, i, rows_out)
        nxt = min(i + 1, DEPTH - 1)
        s, h = out_ln(a, ffn_w_out_bf16, i, zero_bias, s, mods[i, 5], ln_g[i, 1], ln_b[i, 1],
                      mods[nxt, 1], mods[nxt, 0], rows_out, seq, batch, ROW_TILE // 2)
    return s[:t_lat].reshape(batch, seq, d)
```
